```python
import math
import jax, jax.numpy as jnp
from jax import lax
import numpy as np

D_MODEL = 4096
BATCH = 4
SEQ = 2048
DEPTH = 2

HEAD_DIM = 128
D_MIX = D_MODEL
BRANCH_WIDTH = D_MIX // 4
GLA_HEADS = BRANCH_WIDTH // HEAD_DIM
GLA_DK = HEAD_DIM // 2
GLA_DV = HEAD_DIM
GLA_RANK = 16
GLA_TAU = 16.0
GLA_CHUNK = 64
SWA_HEADS = BRANCH_WIDTH // HEAD_DIM
SWA_KV_HEADS = SWA_HEADS // 4
SWA_GROUP = SWA_HEADS // SWA_KV_HEADS
SWA_WINDOW = 128
DIFF_HEADS = BRANCH_WIDTH // HEAD_DIM
DIFF_DQK = HEAD_DIM // 2
DIFF_DV = HEAD_DIM
Q_BLOCK = 128
LRU_WIDTH = BRANCH_WIDTH
LRU_BLOCKS = BRANCH_WIDTH // HEAD_DIM
LRU_BLOCK_DIM = LRU_WIDTH // LRU_BLOCKS
LRU_C = 8.0
CONV_WIDTH = 4
NORM_EPS = 1e-6

IN_SPLITS = (
    GLA_HEADS * GLA_DK, GLA_HEADS * GLA_DK, GLA_HEADS * GLA_DV, GLA_RANK,
    SWA_HEADS * HEAD_DIM, SWA_KV_HEADS * HEAD_DIM, SWA_KV_HEADS * HEAD_DIM,
    DIFF_HEADS * 2 * DIFF_DQK, DIFF_HEADS * 2 * DIFF_DQK, DIFF_HEADS * DIFF_DV,
    LRU_WIDTH, D_MIX,
)
D_IN = sum(IN_SPLITS)

kernel_name = 'hybrid_parallel_heads_block'


def rms_norm(x, w):
    xf = x.astype(jnp.float32)
    y = xf * lax.rsqrt(jnp.mean(xf * xf, axis=-1, keepdims=True) + NORM_EPS)
    return (y * w.astype(jnp.float32)).astype(x.dtype)


def alibi_slopes(n):
    return 2.0 ** (-8.0 * jnp.arange(1, n + 1, dtype=jnp.float32) / n)


def split_cols(t, sizes):
    offs, acc = [], 0
    for s in sizes[:-1]:
        acc += s
        offs.append(acc)
    return jnp.split(t, offs, axis=-1)


def gla_mixer(q, k, v, a_lr, w_up, b_up, norm_w):
    B, S, _ = q.shape
    C, H, dk, dv = GLA_CHUNK, GLA_HEADS, GLA_DK, GLA_DV
    nc = S // C
    f32 = jnp.float32
    qf = q.astype(f32).reshape(B, nc, C, H, dk) * dk ** -0.5
    kf = k.astype(f32).reshape(B, nc, C, H, dk)
    vf = v.astype(f32).reshape(B, nc, C, H, dv)
    logit = jnp.einsum('bsr,rk->bsk', a_lr.astype(f32), w_up.astype(f32)) + b_up.astype(f32)
    g = (jax.nn.log_sigmoid(logit) / GLA_TAU).reshape(B, nc, C, H, dk)
    b = jnp.cumsum(g, axis=2)
    b_last = b[:, :, -1:]
    q_dec = qf * jnp.exp(b)
    k_dec = kf * jnp.exp(-b)
    causal = jnp.tril(jnp.ones((C, C), dtype=bool))
    A = jnp.einsum('bnihd,bnjhd->bnhij', q_dec, k_dec)
    A = jnp.where(causal, A, 0.0)
    o_intra = jnp.einsum('bnhij,bnjhv->bnihv', A, vf)
    k_state = kf * jnp.exp(b_last - b)
    U = jnp.einsum('bnjhd,bnjhv->bnhdv', k_state, vf)
    decay = jnp.exp(b_last[:, :, 0])

    def step(state, inp):
        d, u = inp
        return d[..., None] * state + u, state

    s0 = jnp.zeros((B, H, dk, dv), f32)
    _, s_start = lax.scan(step, s0, (jnp.swapaxes(decay, 0, 1), jnp.swapaxes(U, 0, 1)))
    s_start = jnp.swapaxes(s_start, 0, 1)
    o_inter = jnp.einsum('bnihd,bnhdv->bnihv', q_dec, s_start)
    o = (o_intra + o_inter).reshape(B, S, H, dv)
    o = rms_norm(o, norm_w)
    return o.reshape(B, S, H * dv).astype(v.dtype)


def with_prev_block(t, W):
    B, S, Hk, d = t.shape
    tb = t.reshape(B, S // W, W, Hk, d)
    prev = jnp.concatenate([jnp.zeros_like(tb[:, :1]), tb[:, :-1]], axis=1)
    return jnp.concatenate([prev, tb], axis=2)


def swa_mixer(q, k, v, q_norm_w, k_norm_w, sinks):
    B, S, _ = q.shape
    W = SWA_WINDOW
    nb = S // W
    q = rms_norm(q.reshape(B, S, SWA_KV_HEADS, SWA_GROUP, HEAD_DIM), q_norm_w) * HEAD_DIM ** -0.5
    k = rms_norm(k.reshape(B, S, SWA_KV_HEADS, HEAD_DIM), k_norm_w)
    v = v.reshape(B, S, SWA_KV_HEADS, HEAD_DIM)
    qb = q.reshape(B, nb, W, SWA_KV_HEADS, SWA_GROUP, HEAD_DIM)
    kb = with_prev_block(k, W)
    vb = with_prev_block(v, W)
    s = jnp.einsum('bnqhgd,bnkhd->bnhgqk', qb, kb).astype(jnp.float32)
    q_pos = jnp.arange(W)[:, None] + W
    k_pos = jnp.arange(2 * W)[None, :]
    dist = q_pos - k_pos
    blk = jnp.arange(nb)[:, None, None]
    valid = (dist >= 0) & (dist < W) & (blk * W + k_pos - W >= 0)
    slopes = alibi_slopes(SWA_HEADS).reshape(SWA_KV_HEADS, SWA_GROUP)
    s = s - slopes[:, :, None, None] * dist.astype(jnp.float32)
    s = jnp.where(valid[None, :, None, None], s, -jnp.inf)
    sink = sinks.astype(jnp.float32).reshape(1, 1, SWA_KV_HEADS, SWA_GROUP, 1, 1)
    m = jnp.maximum(jnp.max(s, axis=-1, keepdims=True), sink)
    p = jnp.exp(s - m)
    p = p / (jnp.sum(p, axis=-1, keepdims=True) + jnp.exp(sink - m))
    o = jnp.einsum('bnhgqk,bnkhd->bnqhgd', p.astype(v.dtype), vb)
    return o.reshape(B, S, SWA_HEADS * HEAD_DIM)


def diff_mixer(q, k, v, q_norm_w, k_norm_w, lq1, lk1, lq2, lk2, out_norm_w, lambda_init):
    B, S, _ = q.shape
    H = DIFF_HEADS
    q = rms_norm(q.reshape(B, S, H, 2, DIFF_DQK), q_norm_w) * DIFF_DQK ** -0.5
    k = rms_norm(k.reshape(B, S, H, 2, DIFF_DQK), k_norm_w)
    v = v.reshape(B, S, H, DIFF_DV)
    f32 = jnp.float32
    lam = (jnp.exp(jnp.sum(lq1.astype(f32) * lk1.astype(f32)))
           - jnp.exp(jnp.sum(lq2.astype(f32) * lk2.astype(f32))) + lambda_init)
    slopes = alibi_slopes(H)
    nq = S // Q_BLOCK
    qb = jnp.moveaxis(q.reshape(B, nq, Q_BLOCK, H, 2, DIFF_DQK), 1, 0)
    k_pos = jnp.arange(S)

    def block(args):
        qi, idx = args
        s = jnp.einsum('bqhcd,bkhcd->bhcqk', qi, k).astype(f32)
        q_pos = idx * Q_BLOCK + jnp.arange(Q_BLOCK)
        dist = q_pos[:, None] - k_pos[None, :]
        s = s - slopes[None, :, None, None, None] * dist.astype(f32)
        s = jnp.where(dist >= 0, s, -jnp.inf)
        p = jax.nn.softmax(s, axis=-1)
        w = p[:, :, 0] - lam * p[:, :, 1]
        return jnp.einsum('bhqk,bkhv->bqhv', w.astype(v.dtype), v)

    o = lax.map(block, (qb, jnp.arange(nq)))
    o = jnp.moveaxis(o, 0, 1).reshape(B, S, H, DIFF_DV)
    o = rms_norm(o, out_norm_w) * (1.0 - lambda_init)
    return o.reshape(B, S, H * DIFF_DV)


def rglru_mixer(x, conv_w, conv_b, w_r, b_r, w_i, b_i, lam):
    B, S, Wd = x.shape
    f32 = jnp.float32
    xc = lax.conv_general_dilated(
        x, conv_w[:, None, :].astype(x.dtype), window_strides=(1,),
        padding=[(CONV_WIDTH - 1, 0)], dimension_numbers=('NWC', 'WIO', 'NWC'),
        feature_group_count=Wd) + conv_b
    xf = xc.astype(f32)
    xh = xf.reshape(B, S, LRU_BLOCKS, LRU_BLOCK_DIM)
    r = jax.nn.sigmoid(jnp.einsum('bsni,nij->bsnj', xh, w_r.astype(f32)).reshape(B, S, Wd) + b_r.astype(f32))
    i = jax.nn.sigmoid(jnp.einsum('bsni,nij->bsnj', xh, w_i.astype(f32)).reshape(B, S, Wd) + b_i.astype(f32))
    log_a = -LRU_C * r * jax.nn.softplus(-lam.astype(f32))
    a = jnp.exp(log_a)
    u = jnp.sqrt(-jnp.expm1(2.0 * log_a)) * (i * xf)

    def combine(c1, c2):
        a1, b1 = c1
        a2, b2 = c2
        return a1 * a2, a2 * b1 + b2

    _, h = lax.associative_scan(combine, (a, u), axis=1)
    return h.astype(x.dtype)


def setup_inputs(seed: int = 0) -> dict:
    key = jax.random.key(seed)
    ks = jax.random.split(key, 24)

    def nrm(k, shape, scale):
        return jax.random.normal(k, shape, jnp.float32) * scale

    u = jax.random.uniform(ks[23], (DEPTH, LRU_WIDTH), jnp.float32, 0.9, 0.999)
    p = u ** (1.0 / LRU_C)
    return {
        'x': nrm(ks[0], (BATCH, SEQ, D_MODEL), 1.0),
        'norm_w': 1.0 + nrm(ks[1], (DEPTH, D_MODEL), 0.02),
        'w_in': nrm(ks[2], (DEPTH, D_MODEL, D_IN), D_MODEL ** -0.5),
        'w_out': nrm(ks[3], (DEPTH, D_MIX, D_MODEL), D_MIX ** -0.5),
        'gla_w_up': nrm(ks[4], (DEPTH, GLA_RANK, GLA_HEADS * GLA_DK), GLA_RANK ** -0.5),
        'gla_b_up': nrm(ks[5], (DEPTH, GLA_HEADS * GLA_DK), 0.1),
        'gla_norm_w': 1.0 + nrm(ks[6], (DEPTH, GLA_DV), 0.02),
        'swa_q_norm': 1.0 + nrm(ks[7], (DEPTH, HEAD_DIM), 0.02),
        'swa_k_norm': 1.0 + nrm(ks[8], (DEPTH, HEAD_DIM), 0.02),
        'swa_sinks': nrm(ks[9], (DEPTH, SWA_HEADS), 0.5),
        'diff_q_norm': 1.0 + nrm(ks[10], (DEPTH, DIFF_DQK), 0.02),
        'diff_k_norm': 1.0 + nrm(ks[11], (DEPTH, DIFF_DQK), 0.02),
        'diff_lq1': nrm(ks[12], (DEPTH, DIFF_DQK), 0.1),
        'diff_lk1': nrm(ks[13], (DEPTH, DIFF_DQK), 0.1),
        'diff_lq2': nrm(ks[14], (DEPTH, DIFF_DQK), 0.1),
        'diff_lk2': nrm(ks[15], (DEPTH, DIFF_DQK), 0.1),
        'diff_out_norm': 1.0 + nrm(ks[16], (DEPTH, DIFF_DV), 0.02),
        'lru_conv_w': nrm(ks[17], (DEPTH, CONV_WIDTH, LRU_WIDTH), CONV_WIDTH ** -0.5),
        'lru_conv_b': nrm(ks[18], (DEPTH, LRU_WIDTH), 0.02),
        'lru_w_r': nrm(ks[19], (DEPTH, LRU_BLOCKS, LRU_BLOCK_DIM, LRU_BLOCK_DIM), LRU_BLOCK_DIM ** -0.5),
        'lru_b_r': nrm(ks[20], (DEPTH, LRU_WIDTH), 0.02),
        'lru_w_i': nrm(ks[21], (DEPTH, LRU_BLOCKS, LRU_BLOCK_DIM, LRU_BLOCK_DIM), LRU_BLOCK_DIM ** -0.5),
        'lru_b_i': nrm(ks[22], (DEPTH, LRU_WIDTH), 0.02),
        'lru_lambda': jnp.log(p) - jnp.log1p(-p),
    }


def reference(x, norm_w, w_in, w_out, gla_w_up, gla_b_up, gla_norm_w,
              swa_q_norm, swa_k_norm, swa_sinks,
              diff_q_norm, diff_k_norm, diff_lq1, diff_lk1, diff_lq2, diff_lk2, diff_out_norm,
              lru_conv_w, lru_conv_b, lru_w_r, lru_b_r, lru_w_i, lru_b_i, lru_lambda):
    for layer in range(DEPTH):
        h = rms_norm(x, norm_w[layer])
        proj = jnp.einsum('bsd,de->bse', h, w_in[layer])
        (g_q, g_k, g_v, g_a, s_q, s_k, s_v, d_q, d_k, d_v, r_x, gate) = split_cols(proj, IN_SPLITS)
        y_gla = gla_mixer(g_q, g_k, g_v, g_a, gla_w_up[layer], gla_b_up[layer], gla_norm_w[layer])
        y_swa = swa_mixer(s_q, s_k, s_v, swa_q_norm[layer], swa_k_norm[layer], swa_sinks[layer])
        lambda_init = 0.8 - 0.6 * math.exp(-0.3 * layer)
        y_diff = diff_mixer(d_q, d_k, d_v, diff_q_norm[layer], diff_k_norm[layer],
                            diff_lq1[layer], diff_lk1[layer], diff_lq2[layer], diff_lk2[layer],
                            diff_out_norm[layer], lambda_init)
        y_lru = rglru_mixer(r_x, lru_conv_w[layer], lru_conv_b[layer], lru_w_r[layer], lru_b_r[layer],
                            lru_w_i[layer], lru_b_i[layer], lru_lambda[layer])
        y = jnp.concatenate([y_gla, y_swa, y_diff, y_lru], axis=-1)
        y = y * jax.nn.silu(gate)
        x = x + jnp.einsum('bse,ed->bsd', y, w_out[layer])
    return x
```

```python
import functools
import math

import jax
import jax.numpy as jnp
from jax import lax
from jax.experimental import pallas as pl
from jax.experimental.pallas import tpu as pltpu

F32 = jnp.float32
BF16 = jnp.bfloat16

D_MODEL = 4096
HEAD_DIM = 128
BRANCH = D_MODEL // 4
NORM_EPS = 1e-6
GLA_DK = 64
GLA_RANK = 16
GLA_TAU = 16.0
GLA_CHUNK = 64
SWA_WINDOW = 128
SWA_GROUP = 4
SWA_KV_HEADS = 2
DIFF_DQK = 64
DIFF_HEADS = 8
LRU_C = 8.0
CONV_WIDTH = 4
LANES = 128
SUBLANES = 8

OFF_GQ, OFF_GK, OFF_GV = 0, 512, 1024
OFF_SQ = 2048
OFF_DQ, OFF_DK, OFF_DV = 3072, 4096, 5120
OFF_RX = 6144
OFF_GATE = 7168
OFF_SK, OFF_SV = 11264, 11520
N_MAIN = 11776
ORIG_A = 2048
ORIG_SQ = 2064
ORIG_SK = 3088
ORIG_DQ = 3600
ORIG_END = 11792

VMEM_LIMIT = 56 * 1024 * 1024


def _cparams(sem):
    return pltpu.CompilerParams(dimension_semantics=sem, vmem_limit_bytes=VMEM_LIMIT)


def _rms(x, w):
    ms = jnp.mean(x * x, axis=-1, keepdims=True)
    return x * lax.rsqrt(ms + NORM_EPS) * w


def _nt_dot(a, b):
    return lax.dot_general(a, b, (((1,), (1,)), ((), ())), preferred_element_type=F32)


def _norm_kernel(x_ref, w_ref, o_ref):
    o_ref[...] = _rms(x_ref[...], w_ref[...]).astype(o_ref.dtype)


def _norm(x2, w, tm=512):
    m, d = x2.shape
    return pl.pallas_call(
        _norm_kernel,
        grid=(m // tm,),
        in_specs=[pl.BlockSpec((tm, d), lambda i: (i, 0)),
                  pl.BlockSpec((1, d), lambda i: (0, 0))],
        out_specs=pl.BlockSpec((tm, d), lambda i: (i, 0)),
        out_shape=jax.ShapeDtypeStruct((m, d), BF16),
        compiler_params=_cparams(("parallel",)),
        name="rmsnorm",
    )(x2, w)


def _inproj_kernel(h_ref, w_ref, wa_ref, o_ref, a_ref):
    h = h_ref[...]
    o_ref[...] = jnp.dot(h, w_ref[...], preferred_element_type=F32).astype(o_ref.dtype)

    @pl.when(pl.program_id(1) == 0)
    def _():
        a_ref[...] = jnp.dot(h, wa_ref[...], preferred_element_type=F32)


def _inproj(h, w_main, w_a, tm=1024, tn=512):
    m, d = h.shape
    n = w_main.shape[1]
    return pl.pallas_call(
        _inproj_kernel,
        grid=(m // tm, n // tn),
        in_specs=[pl.BlockSpec((tm, d), lambda i, j: (i, 0)),
                  pl.BlockSpec((d, tn), lambda i, j: (0, j)),
                  pl.BlockSpec((d, LANES), lambda i, j: (0, 0))],
        out_specs=[pl.BlockSpec((tm, tn), lambda i, j: (i, j)),
                   pl.BlockSpec((tm, LANES), lambda i, j: (i, 0))],
        out_shape=[jax.ShapeDtypeStruct((m, n), BF16),
                   jax.ShapeDtypeStruct((m, LANES), F32)],
        compiler_params=_cparams(("parallel", "arbitrary")),
        name="inproj",
    )(h, w_main, w_a)


def _gla_kernel(q_ref, k_ref, v_ref, a_ref, wup_ref, bup_ref, nw_ref, o_ref, *, seq):
    c = GLA_CHUNK
    nc = seq // c
    dv2 = 2 * HEAD_DIM
    logit = jnp.dot(a_ref[...], wup_ref[...], preferred_element_type=F32) + bup_ref[...]
    g = (jnp.minimum(logit, 0.0) - jnp.log1p(jnp.exp(-jnp.abs(logit)))) * (1.0 / GLA_TAU)
    row = lax.broadcasted_iota(jnp.int32, (seq, LANES), 0) & (c - 1)
    b = g
    shift = 1
    while shift < c:
        b = b + jnp.where(row >= shift, pltpu.roll(b, shift, 0), 0.0)
        shift *= 2
    b3 = b.reshape(nc, c, LANES)
    b_last = b3[:, c - 1:c, :]
    q3 = q_ref[...].astype(F32).reshape(nc, c, LANES)
    k3 = k_ref[...].astype(F32).reshape(nc, c, LANES)
    v3 = v_ref[...].reshape(nc, c, dv2)
    q_dec = q3 * (GLA_DK ** -0.5) * jnp.exp(b3)
    k_dec = (k3 * jnp.exp(-b3)).astype(BF16)
    k_state = (k3 * jnp.exp(b_last - b3)).astype(BF16)
    decay = jnp.exp(b_last)
    lane = lax.broadcasted_iota(jnp.int32, (1, 1, LANES), 2)
    q_st = jnp.concatenate([jnp.where(lane < GLA_DK, q_dec, 0.0),
                            jnp.where(lane >= GLA_DK, q_dec, 0.0)], axis=1).astype(BF16)
    att = jnp.einsum('nid,njd->nij', q_st, k_dec, preferred_element_type=F32)
    ii = lax.broadcasted_iota(jnp.int32, (1, 2 * c, c), 1) & (c - 1)
    jj = lax.broadcasted_iota(jnp.int32, (1, 2 * c, c), 2)
    att = jnp.where(ii >= jj, att, 0.0).astype(BF16)
    o_all = jnp.einsum('nij,njv->niv', att, v3, preferred_element_type=F32)
    v3_t = jnp.swapaxes(v3.astype(F32), 1, 2).astype(BF16)
    u_t = jnp.einsum('nvj,njd->nvd', v3_t, k_state, preferred_element_type=F32)
    st = jnp.zeros((dv2, LANES), F32)
    starts = []
    for n in range(nc):
        starts.append(st.astype(BF16))
        st = st * decay[n] + u_t[n]
    s_start = jnp.stack(starts, axis=0)
    o_all = o_all + jnp.einsum('nid,nvd->niv', q_st, s_start, preferred_element_type=F32)
    nw = nw_ref[...]
    o0 = _rms(o_all[:, 0:c, 0:HEAD_DIM], nw).reshape(seq, HEAD_DIM)
    o1 = _rms(o_all[:, c:2 * c, HEAD_DIM:dv2], nw).reshape(seq, HEAD_DIM)
    o_ref[:, 0:HEAD_DIM] = o0.astype(o_ref.dtype)
    o_ref[:, HEAD_DIM:dv2] = o1.astype(o_ref.dtype)


def _gla(proj, a_lr, w_up, b_up, norm_w):
    bsz, seq, _ = proj.shape
    npair = BRANCH // (2 * HEAD_DIM)
    kern = functools.partial(_gla_kernel, seq=seq)
    return pl.pallas_call(
        kern,
        grid=(bsz, npair),
        in_specs=[
            pl.BlockSpec((None, seq, LANES), lambda b, p: (b, 0, OFF_GQ // LANES + p)),
            pl.BlockSpec((None, seq, LANES), lambda b, p: (b, 0, OFF_GK // LANES + p)),
            pl.BlockSpec((None, seq, 2 * HEAD_DIM), lambda b, p: (b, 0, OFF_GV // (2 * HEAD_DIM) + p)),
            pl.BlockSpec((None, seq, LANES), lambda b, p: (b, 0, 0)),
            pl.BlockSpec((LANES, LANES), lambda b, p: (0, p)),
            pl.BlockSpec((1, LANES), lambda b, p: (0, p)),
            pl.BlockSpec((1, HEAD_DIM), lambda b, p: (0, 0)),
        ],
        out_specs=pl.BlockSpec((None, seq, 2 * HEAD_DIM), lambda b, p: (b, 0, p)),
        out_shape=jax.ShapeDtypeStruct((bsz, seq, BRANCH), BF16),
        compiler_params=_cparams(("parallel", "parallel")),
        name="gla",
    )(proj, proj, proj, a_lr, w_up, b_up, norm_w)


def _swa_kernel(slope_ref, sink_ref, q_ref, k_ref, v_ref, qn_ref, kn_ref, o_ref,
                qs_ref, ks_ref, *, seq):
    w = SWA_WINDOW
    nb = seq // w
    g_n = SWA_GROUP
    kvh = pl.program_id(1)
    ks_ref[...] = _rms(k_ref[...].astype(F32), kn_ref[...]).astype(BF16)
    for g in range(g_n):
        qg = q_ref[:, g * HEAD_DIM:(g + 1) * HEAD_DIM].astype(F32)
        qs_ref[:, g * HEAD_DIM:(g + 1) * HEAD_DIM] = (
            _rms(qg, qn_ref[...]) * (HEAD_DIM ** -0.5)).astype(BF16)
    qi = lax.broadcasted_iota(jnp.int32, (w, 2 * w), 0)
    kj = lax.broadcasted_iota(jnp.int32, (w, 2 * w), 1)
    dist = qi + w - kj
    valid = (dist >= 0) & (dist < w)
    distf = dist.astype(F32)
    biases, sinks = [], []
    for g in range(g_n):
        slope = slope_ref[kvh * g_n + g]
        biases.append(jnp.where(valid, -slope * distf, -jnp.inf))
        sinks.append(sink_ref[kvh * g_n + g])

    def block(q_rows, k_win, v_win, bias_cols):
        q_st = jnp.concatenate(
            [qs_ref[q_rows, g * HEAD_DIM:(g + 1) * HEAD_DIM] for g in range(g_n)], axis=0)
        s = _nt_dot(q_st, k_win)
        ps, dens = [], []
        for g in range(g_n):
            sg = s[g * w:(g + 1) * w] + biases[g][:, bias_cols]
            m = jnp.maximum(jnp.max(sg, axis=-1, keepdims=True), sinks[g])
            p = jnp.exp(sg - m)
            dens.append(jnp.sum(p, axis=-1, keepdims=True) + jnp.exp(sinks[g] - m))
            ps.append(p.astype(BF16))
        o = jnp.dot(jnp.concatenate(ps, axis=0), v_win, preferred_element_type=F32)
        for g in range(g_n):
            og = o[g * w:(g + 1) * w] / dens[g]
            o_ref[q_rows, g * HEAD_DIM:(g + 1) * HEAD_DIM] = og.astype(o_ref.dtype)

    block(pl.ds(0, w), ks_ref[0:w, :], v_ref[0:w, :], slice(w, 2 * w))

    def body(n, carry):
        q_rows = pl.ds(pl.multiple_of(n * w, w), w)
        win = pl.ds(pl.multiple_of((n - 1) * w, w), 2 * w)
        block(q_rows, ks_ref[win, :], v_ref[win, :], slice(0, 2 * w))
        return carry

    lax.fori_loop(1, nb, body, 0)


def _swa(proj, q_norm, k_norm, slopes, sinks):
    bsz, seq, _ = proj.shape
    qw = SWA_GROUP * HEAD_DIM
    kern = functools.partial(_swa_kernel, seq=seq)
    smem = pl.BlockSpec(memory_space=pltpu.SMEM)
    return pl.pallas_call(
        kern,
        grid=(bsz, SWA_KV_HEADS),
        in_specs=[
            smem, smem,
            pl.BlockSpec((None, seq, qw), lambda b, h: (b, 0, OFF_SQ // qw + h)),
            pl.BlockSpec((None, seq, HEAD_DIM), lambda b, h: (b, 0, OFF_SK // HEAD_DIM + h)),
            pl.BlockSpec((None, seq, HEAD_DIM), lambda b, h: (b, 0, OFF_SV // HEAD_DIM + h)),
            pl.BlockSpec((1, HEAD_DIM), lambda b, h: (0, 0)),
            pl.BlockSpec((1, HEAD_DIM), lambda b, h: (0, 0)),
        ],
        out_specs=pl.BlockSpec((None, seq, qw), lambda b, h: (b, 0, h)),
        out_shape=jax.ShapeDtypeStruct((bsz, seq, BRANCH), BF16),
        scratch_shapes=[pltpu.VMEM((seq, qw), BF16), pltpu.VMEM((seq, HEAD_DIM), BF16)],
        compiler_params=_cparams(("parallel", "parallel")),
        name="swa",
    )(slopes, sinks, proj, proj, proj, q_norm, k_norm)


DIFF_TQ = 256


def _half_rms(x, w2):
    lane = lax.broadcasted_iota(jnp.int32, (1, LANES), 1)
    lo = lane < DIFF_DQK
    sq = x * x
    s_lo = jnp.sum(jnp.where(lo, sq, 0.0), axis=-1, keepdims=True)
    s_hi = jnp.sum(jnp.where(lo, 0.0, sq), axis=-1, keepdims=True)
    inv = jnp.where(lo, lax.rsqrt(s_lo / DIFF_DQK + NORM_EPS), lax.rsqrt(s_hi / DIFF_DQK + NORM_EPS))
    return x * inv * w2


def _diff_kernel(slope_ref, q_ref, k_ref, v_ref, qn_ref, kn_ref, lq1_ref, lk1_ref, lq2_ref,
                 lk2_ref, onw_ref, o_ref, qs_ref, ks_ref, m_ref, l_ref, acc_ref,
                 *, seq, lambda_init):
    t = DIFF_TQ
    nq = seq // t
    slope = slope_ref[pl.program_id(1)]
    lam = (jnp.exp(jnp.sum(lq1_ref[...] * lk1_ref[...], axis=-1, keepdims=True))
           - jnp.exp(jnp.sum(lq2_ref[...] * lk2_ref[...], axis=-1, keepdims=True)) + lambda_init)
    ks_ref[...] = _half_rms(k_ref[...].astype(F32), kn_ref[...]).astype(BF16)
    qn = _half_rms(q_ref[...].astype(F32), qn_ref[...]) * (DIFF_DQK ** -0.5)
    lane = lax.broadcasted_iota(jnp.int32, (1, LANES), 1)
    q_lo = jnp.where(lane < DIFF_DQK, qn, 0.0).astype(BF16)
    q_hi = jnp.where(lane < DIFF_DQK, 0.0, qn).astype(BF16)
    for i in range(nq):
        qs_ref[i, 0:t, :] = q_lo[i * t:(i + 1) * t]
        qs_ref[i, t:2 * t, :] = q_hi[i * t:(i + 1) * t]
    r = lax.broadcasted_iota(jnp.int32, (2 * t, t), 0) & (t - 1)
    c = lax.broadcasted_iota(jnp.int32, (2 * t, t), 1)
    rel = (r - c).astype(F32) * slope
    diag_bias = jnp.where(r >= c, -rel, -jnp.inf)

    for i in range(nq):
        q2 = qs_ref[i]
        rows = slice(i * t, (i + 1) * t)
        s = _nt_dot(q2, ks_ref[rows, :]) + diag_bias
        m0 = jnp.max(s, axis=-1, keepdims=True)
        p = jnp.exp(s - m0)
        m_ref[...] = m0
        l_ref[...] = jnp.sum(p, axis=-1, keepdims=True)
        acc_ref[...] = jnp.dot(p.astype(BF16), v_ref[rows, :], preferred_element_type=F32)

        def body(kb, carry, q2=q2, i=i):
            krows = pl.ds(pl.multiple_of(kb * t, t), t)
            off = ((i - kb) * t).astype(F32) * slope
            s = _nt_dot(q2, ks_ref[krows, :]) - (rel + off)
            m_old = m_ref[...]
            m_new = jnp.maximum(m_old, jnp.max(s, axis=-1, keepdims=True))
            alpha = jnp.exp(m_old - m_new)
            p = jnp.exp(s - m_new)
            l_ref[...] = alpha * l_ref[...] + jnp.sum(p, axis=-1, keepdims=True)
            acc_ref[...] = alpha * acc_ref[...] + jnp.dot(
                p.astype(BF16), v_ref[krows, :], preferred_element_type=F32)
            m_ref[...] = m_new
            return carry

        if i > 0:
            lax.fori_loop(0, i, body, 0)
        o12 = acc_ref[...] / l_ref[...]
        o = o12[0:t] - lam * o12[t:2 * t]
        o = _rms(o, onw_ref[...]) * (1.0 - lambda_init)
        o_ref[rows, :] = o.astype(o_ref.dtype)


def _diff(proj, slopes, q_norm2, k_norm2, lq1, lk1, lq2, lk2, out_norm, lambda_init):
    bsz, seq, _ = proj.shape
    t = DIFF_TQ
    kern = functools.partial(_diff_kernel, seq=seq, lambda_init=lambda_init)
    vec = lambda n: pl.BlockSpec((1, n), lambda b, h: (0, 0))
    return pl.pallas_call(
        kern,
        grid=(bsz, DIFF_HEADS),
        in_specs=[
            pl.BlockSpec(memory_space=pltpu.SMEM),
            pl.BlockSpec((None, seq, HEAD_DIM), lambda b, h: (b, 0, OFF_DQ // HEAD_DIM + h)),
            pl.BlockSpec((None, seq, HEAD_DIM), lambda b, h: (b, 0, OFF_DK // HEAD_DIM + h)),
            pl.BlockSpec((None, seq, HEAD_DIM), lambda b, h: (b, 0, OFF_DV // HEAD_DIM + h)),
            vec(HEAD_DIM), vec(HEAD_DIM),
            vec(DIFF_DQK), vec(DIFF_DQK), vec(DIFF_DQK), vec(DIFF_DQK),
            vec(HEAD_DIM),
        ],
        out_specs=pl.BlockSpec((None, seq, HEAD_DIM), lambda b, h: (b, 0, h)),
        out_shape=jax.ShapeDtypeStruct((bsz, seq, BRANCH), BF16),
        scratch_shapes=[
            pltpu.VMEM((seq // t, 2 * t, HEAD_DIM), BF16),
            pltpu.VMEM((seq, HEAD_DIM), BF16),
            pltpu.VMEM((2 * t, 1), F32),
            pltpu.VMEM((2 * t, 1), F32),
            pltpu.VMEM((2 * t, HEAD_DIM), F32),
        ],
        compiler_params=_cparams(("parallel", "parallel")),
        name="diffattn",
    )(slopes, proj, proj, proj, q_norm2, k_norm2, lq1, lk1, lq2, lk2, out_norm)


LRU_TILE = 512


def _lru_kernel(x_ref, cw_ref, cb_ref, wg_ref, br_ref, bi_ref, lam_ref, o_ref,
                a_scr, h_scr, *, seq):
    width = LRU_TILE
    x = x_ref[...].astype(F32)
    row = lax.broadcasted_iota(jnp.int32, (seq, width), 0)
    xc = x * cw_ref[CONV_WIDTH - 1:CONV_WIDTH, :] + cb_ref[...]
    for s in range(1, CONV_WIDTH):
        tap = cw_ref[CONV_WIDTH - 1 - s:CONV_WIDTH - s, :]
        xc = xc + jnp.where(row >= s, pltpu.roll(x, s, 0), 0.0) * tap
    lam = lam_ref[...]
    softplus_neg = jnp.maximum(-lam, 0.0) + jnp.log1p(jnp.exp(-jnp.abs(lam)))
    sub = lax.broadcasted_iota(jnp.int32, (seq, HEAD_DIM), 0) & (SUBLANES - 1)
    for n in range(width // HEAD_DIM):
        cols = slice(n * HEAD_DIM, (n + 1) * HEAD_DIM)
        xn = xc[:, cols]
        ri = jnp.dot(xn.astype(BF16), wg_ref[n], preferred_element_type=F32)
        r = jax.nn.sigmoid(ri[:, 0:HEAD_DIM] + br_ref[:, cols])
        gi = jax.nn.sigmoid(ri[:, HEAD_DIM:2 * HEAD_DIM] + bi_ref[:, cols])
        log_a = (-LRU_C) * r * softplus_neg[:, cols]
        a = jnp.exp(log_a)
        u = jnp.sqrt(1.0 - a * a) * (gi * xn)
        shift = 1
        while shift < SUBLANES:
            keep = sub >= shift
            u = jnp.where(keep, a * pltpu.roll(u, shift, 0) + u, u)
            a = jnp.where(keep, a * pltpu.roll(a, shift, 0), a)
            shift *= 2
        a_scr[:, cols] = a
        h_scr[:, cols] = u

    def body(tile, carry):
        rows = pl.ds(pl.multiple_of(tile * SUBLANES, SUBLANES), SUBLANES)
        h = a_scr[rows, :] * carry + h_scr[rows, :]
        h_scr[rows, :] = h
        return jnp.broadcast_to(h[SUBLANES - 1:SUBLANES, :], (SUBLANES, width))

    lax.fori_loop(0, seq // SUBLANES, body, jnp.zeros((SUBLANES, width), F32), unroll=8)
    o_ref[...] = h_scr[...].astype(o_ref.dtype)


def _lru(proj, conv_w, conv_b, w_gate, b_r, b_i, lam):
    bsz, seq, _ = proj.shape
    t = LRU_TILE
    nblk = t // HEAD_DIM
    kern = functools.partial(_lru_kernel, seq=seq)
    vec = pl.BlockSpec((1, t), lambda b, c: (0, c))
    return pl.pallas_call(
        kern,
        grid=(bsz, BRANCH // t),
        in_specs=[
            pl.BlockSpec((None, seq, t), lambda b, c: (b, 0, OFF_RX // t + c)),
            pl.BlockSpec((CONV_WIDTH, t), lambda b, c: (0, c)),
            vec,
            pl.BlockSpec((nblk, HEAD_DIM, 2 * HEAD_DIM), lambda b, c: (c, 0, 0)),
            vec, vec, vec,
        ],
        out_specs=pl.BlockSpec((None, seq, t), lambda b, c: (b, 0, c)),
        out_shape=jax.ShapeDtypeStruct((bsz, seq, BRANCH), BF16),
        scratch_shapes=[pltpu.VMEM((seq, t), F32), pltpu.VMEM((seq, t), F32)],
        compiler_params=_cparams(("parallel", "parallel")),
        name="rglru",
    )(proj, conv_w, conv_b, w_gate, b_r, b_i, lam)


def _outproj_kernel(y0, y1, y2, y3, g0, g1, g2, g3, w_ref, x_ref, o_ref, lhs_ref):
    @pl.when(pl.program_id(1) == 0)
    def _():
        for k, (y, g) in enumerate(((y0, g0), (y1, g1), (y2, g2), (y3, g3))):
            gate = g[...].astype(F32)
            lhs_ref[:, k * BRANCH:(k + 1) * BRANCH] = (
                y[...].astype(F32) * (gate * jax.nn.sigmoid(gate))).astype(BF16)

    o_ref[...] = x_ref[...] + jnp.dot(lhs_ref[...], w_ref[...], preferred_element_type=F32)


def _outproj(ys, proj2, w_out, x2, tm=512, tn=1024):
    m, d = x2.shape
    y_specs = [pl.BlockSpec((tm, BRANCH), lambda i, j: (i, 0)) for _ in range(4)]
    g_specs = [pl.BlockSpec((tm, BRANCH), functools.partial(lambda i, j, k: (i, OFF_GATE // BRANCH + k), k=k))
               for k in range(4)]
    return pl.pallas_call(
        _outproj_kernel,
        grid=(m // tm, d // tn),
        in_specs=y_specs + g_specs + [
            pl.BlockSpec((d, tn), lambda i, j: (0, j)),
            pl.BlockSpec((tm, tn), lambda i, j: (i, j)),
        ],
        out_specs=pl.BlockSpec((tm, tn), lambda i, j: (i, j)),
        out_shape=jax.ShapeDtypeStruct((m, d), F32),
        scratch_shapes=[pltpu.VMEM((tm, d), BF16)],
        compiler_params=_cparams(("parallel", "arbitrary")),
        name="outproj",
    )(*ys, proj2, proj2, proj2, proj2, w_out, x2)


def _alibi_slopes(n):
    return 2.0 ** (-8.0 * jnp.arange(1, n + 1, dtype=F32) / n)


def _layer(x2, bsz, seq, layer, p):
    w_in = p['w_in']
    w_main = jnp.concatenate(
        [w_in[:, :ORIG_A], w_in[:, ORIG_SQ:ORIG_SK], w_in[:, ORIG_DQ:ORIG_END],
         w_in[:, ORIG_SK:ORIG_DQ]], axis=1).astype(BF16)
    w_a = jnp.pad(w_in[:, ORIG_A:ORIG_SQ], ((0, 0), (0, LANES - GLA_RANK))).astype(BF16)
    w_out = p['w_out'].astype(BF16)

    h = _norm(x2, p['norm_w'].reshape(1, -1))
    proj2, a2 = _inproj(h, w_main, w_a)
    proj = proj2.reshape(bsz, seq, N_MAIN)
    a_lr = a2.reshape(bsz, seq, LANES)

    w_up = jnp.pad(p['gla_w_up'], ((0, LANES - GLA_RANK), (0, 0)))
    y_gla = _gla(proj, a_lr, w_up, p['gla_b_up'].reshape(1, -1), p['gla_norm_w'].reshape(1, -1))

    y_swa = _swa(proj, p['swa_q_norm'].reshape(1, -1), p['swa_k_norm'].reshape(1, -1),
                 _alibi_slopes(SWA_GROUP * SWA_KV_HEADS), p['swa_sinks'])

    lambda_init = 0.8 - 0.6 * math.exp(-0.3 * layer)
    tile2 = lambda v: jnp.concatenate([v, v]).reshape(1, -1)
    y_diff = _diff(proj, _alibi_slopes(DIFF_HEADS), tile2(p['diff_q_norm']), tile2(p['diff_k_norm']),
                   p['diff_lq1'].reshape(1, -1), p['diff_lk1'].reshape(1, -1),
                   p['diff_lq2'].reshape(1, -1), p['diff_lk2'].reshape(1, -1),
                   p['diff_out_norm'].reshape(1, -1), lambda_init)

    w_gate = jnp.concatenate([p['lru_w_r'], p['lru_w_i']], axis=-1).astype(BF16)
    y_lru = _lru(proj, p['lru_conv_w'], p['lru_conv_b'].reshape(1, -1), w_gate,
                 p['lru_b_r'].reshape(1, -1), p['lru_b_i'].reshape(1, -1),
                 p['lru_lambda'].reshape(1, -1))

    ys = [y.reshape(bsz * seq, BRANCH) for y in (y_gla, y_swa, y_diff, y_lru)]
    return _outproj(ys, proj2, w_out, x2)


def kernel(x, norm_w, w_in, w_out, gla_w_up, gla_b_up, gla_norm_w, swa_q_norm, swa_k_norm, swa_sinks, diff_q_norm, diff_k_norm, diff_lq1, diff_lk1, diff_lq2, diff_lk2, diff_out_norm, lru_conv_w, lru_conv_b, lru_w_r, lru_b_r, lru_w_i, lru_b_i, lru_lambda):
    params = dict(norm_w=norm_w, w_in=w_in, w_out=w_out, gla_w_up=gla_w_up, gla_b_up=gla_b_up,
                  gla_norm_w=gla_norm_w, swa_q_norm=swa_q_norm, swa_k_norm=swa_k_norm,
                  swa_sinks=swa_sinks, diff_q_norm=diff_q_norm, diff_k_norm=diff_k_norm,
                  diff_lq1=diff_lq1, diff_lk1=diff_lk1, diff_lq2=diff_lq2, diff_lk2=diff_lk2,
                  diff_out_norm=diff_out_norm, lru_conv_w=lru_conv_w, lru_conv_b=lru_conv_b,
                  lru_w_r=lru_w_r, lru_b_r=lru_b_r, lru_w_i=lru_w_i, lru_b_i=lru_b_i,
                  lru_lambda=lru_lambda)
    bsz, seq, d = x.shape
    x2 = x.reshape(bsz * seq, d)
    for layer in range(norm_w.shape[0]):
        x2 = _layer(x2, bsz, seq, layer, {k: v[layer] for k, v in params.items()})
    return x2.reshape(bsz, seq, d)
```

```python
import functools
import math

import jax
import jax.numpy as jnp
from jax import lax
from jax.experimental import pallas as pl
from jax.experimental.pallas import tpu as pltpu

F32 = jnp.float32
BF16 = jnp.bfloat16

D_MODEL = 4096
HEAD_DIM = 128
BRANCH = D_MODEL // 4
NORM_EPS = 1e-6
GLA_DK = 64
GLA_RANK = 16
GLA_TAU = 16.0
GLA_CHUNK = 64
SWA_WINDOW = 128
SWA_GROUP = 4
SWA_KV_HEADS = 2
DIFF_DQK = 64
DIFF_HEADS = 8
LRU_C = 8.0
CONV_WIDTH = 4
LANES = 128
SUBLANES = 8
LOG2E = math.log2(math.e)

OFF_GQ, OFF_GK, OFF_GV = 0, 512, 1024
OFF_SQ = 2048
OFF_DQ, OFF_DK, OFF_DV = 3072, 4096, 5120
OFF_RX = 6144
OFF_GATE = 7168
OFF_SK, OFF_SV = 11264, 11520
N_MAIN = 11776
OFF_GATE_GLA = OFF_GATE
OFF_GATE_SWA = OFF_GATE + BRANCH
OFF_GATE_DIFF = OFF_GATE + 2 * BRANCH
OFF_GATE_LRU = OFF_GATE + 3 * BRANCH
PREP_TN = 512
PREP_ROWS = 2048
ORIG_A_BLOCK = 2048 // LANES
N_PLAIN_TILES = 2048 // PREP_TN
N_SQ_END_TILE = 3072 // PREP_TN
LAST_TILE = N_MAIN // PREP_TN - 1

VMEM_LIMIT = 56 * 1024 * 1024


def _cparams(sem):
    return pltpu.CompilerParams(dimension_semantics=sem, vmem_limit_bytes=VMEM_LIMIT)


def _rms(x, w):
    ms = jnp.mean(x * x, axis=-1, keepdims=True)
    return x * lax.rsqrt(ms + NORM_EPS) * w


def _silu(g):
    return g * jax.nn.sigmoid(g)


def _prep_win_kernel(a_ref, b_ref, c_ref, o_ref, oa_ref):
    j = pl.program_id(2)

    @pl.when(j < N_PLAIN_TILES)
    def _():
        o_ref[...] = a_ref[...].astype(BF16)

    @pl.when(j >= N_PLAIN_TILES)
    def _():
        o_ref[...] = jnp.concatenate(
            [a_ref[:, GLA_RANK:], b_ref[:, :GLA_RANK]], axis=1).astype(BF16)

    @pl.when(j == 0)
    def _():
        lane = lax.broadcasted_iota(jnp.int32, (1, LANES), 1)
        oa_ref[...] = jnp.where(lane < GLA_RANK, c_ref[...], 0.0).astype(BF16)


def _src_tile(j):
    return jnp.where(j < N_SQ_END_TILE, j, jnp.where(j == LAST_TILE, N_SQ_END_TILE, j + 1))


def _prep_w_in(w_in):
    depth, d, _ = w_in.shape
    ratio = PREP_TN // LANES
    return pl.pallas_call(
        _prep_win_kernel,
        grid=(depth, d // PREP_ROWS, N_MAIN // PREP_TN),
        in_specs=[
            pl.BlockSpec((None, PREP_ROWS, PREP_TN), lambda l, r, j: (l, r, _src_tile(j))),
            pl.BlockSpec((None, PREP_ROWS, LANES), lambda l, r, j: (l, r, ratio * _src_tile(j) + ratio)),
            pl.BlockSpec((None, PREP_ROWS, LANES), lambda l, r, j: (l, r, ORIG_A_BLOCK)),
        ],
        out_specs=[
            pl.BlockSpec((None, PREP_ROWS, PREP_TN), lambda l, r, j: (l, r, j)),
            pl.BlockSpec((None, PREP_ROWS, LANES), lambda l, r, j: (l, r, 0)),
        ],
        out_shape=[jax.ShapeDtypeStruct((depth, d, N_MAIN), BF16),
                   jax.ShapeDtypeStruct((depth, d, LANES), BF16)],
        compiler_params=_cparams(("parallel", "parallel", "arbitrary")),
        name="prep_w_in",
    )(w_in, w_in, w_in)


def _cast_kernel(x_ref, o_ref):
    o_ref[...] = x_ref[...].astype(o_ref.dtype)


def _prep_w_out(w_out, rows=512):
    depth, k, n = w_out.shape
    return pl.pallas_call(
        _cast_kernel,
        grid=(depth, k // rows),
        in_specs=[pl.BlockSpec((None, rows, n), lambda l, r: (l, r, 0))],
        out_specs=pl.BlockSpec((None, rows, n), lambda l, r: (l, r, 0)),
        out_shape=jax.ShapeDtypeStruct((depth, k, n), BF16),
        compiler_params=_cparams(("parallel", "parallel")),
        name="prep_w_out",
    )(w_out)


def _norm_kernel(x_ref, w_ref, o_ref):
    o_ref[...] = _rms(x_ref[...], w_ref[...]).astype(o_ref.dtype)


def _norm(x2, w, tm=512):
    m, d = x2.shape
    return pl.pallas_call(
        _norm_kernel,
        grid=(m // tm,),
        in_specs=[pl.BlockSpec((tm, d), lambda i: (i, 0)),
                  pl.BlockSpec((1, d), lambda i: (0, 0))],
        out_specs=pl.BlockSpec((tm, d), lambda i: (i, 0)),
        out_shape=jax.ShapeDtypeStruct((m, d), BF16),
        compiler_params=_cparams(("parallel",)),
        name="rmsnorm",
    )(x2, w)


def _inproj_kernel(h_ref, w_ref, wa_ref, o_ref, a_ref):
    h = h_ref[...]
    o_ref[...] = jnp.dot(h, w_ref[...], preferred_element_type=F32).astype(o_ref.dtype)

    @pl.when(pl.program_id(1) == 0)
    def _():
        a_ref[...] = jnp.dot(h, wa_ref[...], preferred_element_type=F32)


def _inproj(h, w_main, w_a, layer, tm=1024, tn=512):
    m, d = h.shape
    n = w_main.shape[2]
    return pl.pallas_call(
        _inproj_kernel,
        grid=(m // tm, n // tn),
        in_specs=[pl.BlockSpec((tm, d), lambda i, j: (i, 0)),
                  pl.BlockSpec((None, d, tn), lambda i, j: (layer, 0, j)),
                  pl.BlockSpec((None, d, LANES), lambda i, j: (layer, 0, 0))],
        out_specs=[pl.BlockSpec((tm, tn), lambda i, j: (i, j)),
                   pl.BlockSpec((tm, LANES), lambda i, j: (i, 0))],
        out_shape=[jax.ShapeDtypeStruct((m, n), BF16),
                   jax.ShapeDtypeStruct((m, LANES), F32)],
        compiler_params=_cparams(("parallel", "arbitrary")),
        name="inproj",
    )(h, w_main, w_a)


def _gla_kernel(q_ref, k_ref, v_ref, gate_ref, a_ref, wup_ref, bup_ref, nw_ref, o_ref, *, seq):
    c = GLA_CHUNK
    nc = seq // c
    dv2 = 2 * HEAD_DIM
    logit = jnp.dot(a_ref[...], wup_ref[...], preferred_element_type=F32) + bup_ref[...]
    g = (jnp.minimum(logit, 0.0) - jnp.log1p(jnp.exp(-jnp.abs(logit)))) * (1.0 / GLA_TAU)
    row = lax.broadcasted_iota(jnp.int32, (seq, LANES), 0) & (c - 1)
    b = g
    shift = 1
    while shift < c:
        b = b + jnp.where(row >= shift, pltpu.roll(b, shift, 0), 0.0)
        shift *= 2
    b3 = b.reshape(nc, c, LANES)
    b_last = b3[:, c - 1:c, :]
    q3 = q_ref[...].astype(F32).reshape(nc, c, LANES)
    k3 = k_ref[...].astype(F32).reshape(nc, c, LANES)
    v3 = v_ref[...].reshape(nc, c, dv2)
    q_dec = q3 * (GLA_DK ** -0.5) * jnp.exp(b3)
    k_dec = (k3 * jnp.exp(-b3)).astype(BF16)
    k_state = (k3 * jnp.exp(b_last - b3)).astype(BF16)
    decay = jnp.exp(b_last)
    lane = lax.broadcasted_iota(jnp.int32, (1, 1, LANES), 2)
    q_st = jnp.concatenate([jnp.where(lane < GLA_DK, q_dec, 0.0),
                            jnp.where(lane >= GLA_DK, q_dec, 0.0)], axis=1).astype(BF16)
    att = jnp.einsum('nid,njd->nij', q_st, k_dec, preferred_element_type=F32)
    ii = lax.broadcasted_iota(jnp.int32, (1, 2 * c, c), 1) & (c - 1)
    jj = lax.broadcasted_iota(jnp.int32, (1, 2 * c, c), 2)
    att = jnp.where(ii >= jj, att, 0.0).astype(BF16)
    o_all = jnp.einsum('nij,njv->niv', att, v3, preferred_element_type=F32)
    v3_t = jnp.swapaxes(v3.astype(F32), 1, 2).astype(BF16)
    u_t = jnp.einsum('nvj,njd->nvd', v3_t, k_state, preferred_element_type=F32)
    st = jnp.zeros((dv2, LANES), F32)
    starts = []
    for n in range(nc):
        starts.append(st.astype(BF16))
        st = st * decay[n] + u_t[n]
    s_start = jnp.stack(starts, axis=0)
    o_all = o_all + jnp.einsum('nid,nvd->niv', q_st, s_start, preferred_element_type=F32)
    nw = nw_ref[...]
    o0 = _rms(o_all[:, 0:c, 0:HEAD_DIM], nw).reshape(seq, HEAD_DIM)
    o1 = _rms(o_all[:, c:2 * c, HEAD_DIM:dv2], nw).reshape(seq, HEAD_DIM)
    o_ref[:, 0:HEAD_DIM] = (o0 * _silu(gate_ref[:, 0:HEAD_DIM].astype(F32))).astype(o_ref.dtype)
    o_ref[:, HEAD_DIM:dv2] = (o1 * _silu(gate_ref[:, HEAD_DIM:dv2].astype(F32))).astype(o_ref.dtype)


def _gla(proj, a_lr, w_up, b_up, norm_w):
    bsz, seq, _ = proj.shape
    pair = 2 * HEAD_DIM
    kern = functools.partial(_gla_kernel, seq=seq)
    return pl.pallas_call(
        kern,
        grid=(bsz, BRANCH // pair),
        in_specs=[
            pl.BlockSpec((None, seq, LANES), lambda b, p: (b, 0, OFF_GQ // LANES + p)),
            pl.BlockSpec((None, seq, LANES), lambda b, p: (b, 0, OFF_GK // LANES + p)),
            pl.BlockSpec((None, seq, pair), lambda b, p: (b, 0, OFF_GV // pair + p)),
            pl.BlockSpec((None, seq, pair), lambda b, p: (b, 0, OFF_GATE_GLA // pair + p)),
            pl.BlockSpec((None, seq, LANES), lambda b, p: (b, 0, 0)),
            pl.BlockSpec((LANES, LANES), lambda b, p: (0, p)),
            pl.BlockSpec((1, LANES), lambda b, p: (0, p)),
            pl.BlockSpec((1, HEAD_DIM), lambda b, p: (0, 0)),
        ],
        out_specs=pl.BlockSpec((None, seq, pair), lambda b, p: (b, 0, p)),
        out_shape=jax.ShapeDtypeStruct((bsz, seq, BRANCH), BF16),
        compiler_params=_cparams(("parallel", "parallel")),
        name="gla",
    )(proj, proj, proj, proj, a_lr, w_up, b_up, norm_w)


def _nt_dot(a, b):
    return lax.dot_general(a, b, (((1,), (1,)), ((), ())), preferred_element_type=F32)


def _swa_kernel(slope_ref, sink_ref, q_ref, k_ref, v_ref, gate_ref, qn_ref, kn_ref, o_ref,
                qs_ref, ks_ref, *, seq):
    w = SWA_WINDOW
    nb = seq // w
    g_n = SWA_GROUP
    kvh = pl.program_id(1)
    ks_ref[...] = _rms(k_ref[...].astype(F32), kn_ref[...]).astype(BF16)
    for g in range(g_n):
        qg = q_ref[:, g * HEAD_DIM:(g + 1) * HEAD_DIM].astype(F32)
        qs_ref[:, g * HEAD_DIM:(g + 1) * HEAD_DIM] = (
            _rms(qg, qn_ref[...]) * (HEAD_DIM ** -0.5)).astype(BF16)
    qi = lax.broadcasted_iota(jnp.int32, (w, 2 * w), 0)
    kj = lax.broadcasted_iota(jnp.int32, (w, 2 * w), 1)
    dist = qi + w - kj
    valid = (dist >= 0) & (dist < w)
    distf = dist.astype(F32)
    biases, sinks = [], []
    for g in range(g_n):
        slope = slope_ref[kvh * g_n + g]
        biases.append(jnp.where(valid, -slope * distf, -jnp.inf))
        sinks.append(sink_ref[kvh * g_n + g])

    def block(q_rows, k_win, v_win, bias_cols):
        q_st = jnp.concatenate(
            [qs_ref[q_rows, g * HEAD_DIM:(g + 1) * HEAD_DIM] for g in range(g_n)], axis=0)
        s = _nt_dot(q_st, k_win)
        ps, dens = [], []
        for g in range(g_n):
            sg = s[g * w:(g + 1) * w] + biases[g][:, bias_cols]
            m = jnp.maximum(jnp.max(sg, axis=-1, keepdims=True), sinks[g])
            p = jnp.exp(sg - m)
            dens.append(jnp.sum(p, axis=-1, keepdims=True) + jnp.exp(sinks[g] - m))
            ps.append(p.astype(BF16))
        o = jnp.dot(jnp.concatenate(ps, axis=0), v_win, preferred_element_type=F32)
        for g in range(g_n):
            cols = slice(g * HEAD_DIM, (g + 1) * HEAD_DIM)
            og = o[g * w:(g + 1) * w] / dens[g]
            o_ref[q_rows, cols] = (og * _silu(gate_ref[q_rows, cols].astype(F32))).astype(o_ref.dtype)

    block(pl.ds(0, w), ks_ref[0:w, :], v_ref[0:w, :], slice(w, 2 * w))

    def body(n, carry):
        q_rows = pl.ds(pl.multiple_of(n * w, w), w)
        win = pl.ds(pl.multiple_of((n - 1) * w, w), 2 * w)
        block(q_rows, ks_ref[win, :], v_ref[win, :], slice(0, 2 * w))
        return carry

    lax.fori_loop(1, nb, body, 0)


def _swa(proj, q_norm, k_norm, slopes, sinks):
    bsz, seq, _ = proj.shape
    qw = SWA_GROUP * HEAD_DIM
    kern = functools.partial(_swa_kernel, seq=seq)
    smem = pl.BlockSpec(memory_space=pltpu.SMEM)
    return pl.pallas_call(
        kern,
        grid=(bsz, SWA_KV_HEADS),
        in_specs=[
            smem, smem,
            pl.BlockSpec((None, seq, qw), lambda b, h: (b, 0, OFF_SQ // qw + h)),
            pl.BlockSpec((None, seq, HEAD_DIM), lambda b, h: (b, 0, OFF_SK // HEAD_DIM + h)),
            pl.BlockSpec((None, seq, HEAD_DIM), lambda b, h: (b, 0, OFF_SV // HEAD_DIM + h)),
            pl.BlockSpec((None, seq, qw), lambda b, h: (b, 0, OFF_GATE_SWA // qw + h)),
            pl.BlockSpec((1, HEAD_DIM), lambda b, h: (0, 0)),
            pl.BlockSpec((1, HEAD_DIM), lambda b, h: (0, 0)),
        ],
        out_specs=pl.BlockSpec((None, seq, qw), lambda b, h: (b, 0, h)),
        out_shape=jax.ShapeDtypeStruct((bsz, seq, BRANCH), BF16),
        scratch_shapes=[pltpu.VMEM((seq, qw), BF16), pltpu.VMEM((seq, HEAD_DIM), BF16)],
        compiler_params=_cparams(("parallel", "parallel")),
        name="swa",
    )(slopes, sinks, proj, proj, proj, proj, q_norm, k_norm)


DIFF_TQ = 256


def _half_rms(x, w2):
    lane = lax.broadcasted_iota(jnp.int32, (1, LANES), 1)
    lo = lane < DIFF_DQK
    sq = x * x
    s_lo = jnp.sum(jnp.where(lo, sq, 0.0), axis=-1, keepdims=True)
    s_hi = jnp.sum(jnp.where(lo, 0.0, sq), axis=-1, keepdims=True)
    inv = jnp.where(lo, lax.rsqrt(s_lo / DIFF_DQK + NORM_EPS), lax.rsqrt(s_hi / DIFF_DQK + NORM_EPS))
    return x * inv * w2


def _diff_kernel(slope_ref, q_ref, k_ref, v_ref, gate_ref, qn_ref, kn_ref, lq1_ref, lk1_ref,
                 lq2_ref, lk2_ref, onw_ref, o_ref, qs_ref, kt_ref, *, seq, lambda_init):
    t = DIFF_TQ
    nq = seq // t
    slope = slope_ref[pl.program_id(1)]
    lam = (jnp.exp(jnp.sum(lq1_ref[...] * lk1_ref[...], axis=-1, keepdims=True))
           - jnp.exp(jnp.sum(lq2_ref[...] * lk2_ref[...], axis=-1, keepdims=True)) + lambda_init)
    kt_ref[...] = _half_rms(k_ref[...].astype(F32), kn_ref[...]).T.astype(BF16)
    qn = _half_rms(q_ref[...].astype(F32), qn_ref[...]) * (DIFF_DQK ** -0.5 * LOG2E)
    lane = lax.broadcasted_iota(jnp.int32, (1, LANES), 1)
    q_lo = jnp.where(lane < DIFF_DQK, qn, 0.0).astype(BF16)
    q_hi = jnp.where(lane < DIFF_DQK, 0.0, qn).astype(BF16)
    for i in range(nq):
        qs_ref[i, 0:t, :] = q_lo[i * t:(i + 1) * t]
        qs_ref[i, t:2 * t, :] = q_hi[i * t:(i + 1) * t]
    key_bias = lax.broadcasted_iota(jnp.int32, (1, seq), 1).astype(F32) * (slope * LOG2E)
    r = lax.broadcasted_iota(jnp.int32, (2 * t, t), 0) & (t - 1)
    c = lax.broadcasted_iota(jnp.int32, (2 * t, t), 1)
    causal = r >= c

    for i in range(nq):
        kk = (i + 1) * t
        rows = slice(i * t, kk)
        s = jnp.dot(qs_ref[i], kt_ref[:, 0:kk], preferred_element_type=F32) + key_bias[:, 0:kk]
        s_diag = jnp.where(causal, s[:, kk - t:kk], -jnp.inf)
        s = s_diag if i == 0 else jnp.concatenate([s[:, 0:kk - t], s_diag], axis=1)
        m = jnp.max(s, axis=-1, keepdims=True)
        p = jnp.exp2(s - m)
        inv_l = 1.0 / jnp.sum(p, axis=-1, keepdims=True)
        o12 = jnp.dot(p.astype(BF16), v_ref[0:kk, :], preferred_element_type=F32) * inv_l
        o = o12[0:t] - lam * o12[t:2 * t]
        o = _rms(o, onw_ref[...]) * (1.0 - lambda_init)
        o_ref[rows, :] = (o * _silu(gate_ref[rows, :].astype(F32))).astype(o_ref.dtype)


def _diff(proj, slopes, q_norm2, k_norm2, lq1, lk1, lq2, lk2, out_norm, lambda_init):
    bsz, seq, _ = proj.shape
    t = DIFF_TQ
    kern = functools.partial(_diff_kernel, seq=seq, lambda_init=lambda_init)
    vec = lambda n: pl.BlockSpec((1, n), lambda b, h: (0, 0))
    blk = lambda off: pl.BlockSpec((None, seq, HEAD_DIM), lambda b, h: (b, 0, off // HEAD_DIM + h))
    return pl.pallas_call(
        kern,
        grid=(bsz, DIFF_HEADS),
        in_specs=[
            pl.BlockSpec(memory_space=pltpu.SMEM),
            blk(OFF_DQ), blk(OFF_DK), blk(OFF_DV), blk(OFF_GATE_DIFF),
            vec(HEAD_DIM), vec(HEAD_DIM),
            vec(DIFF_DQK), vec(DIFF_DQK), vec(DIFF_DQK), vec(DIFF_DQK),
            vec(HEAD_DIM),
        ],
        out_specs=pl.BlockSpec((None, seq, HEAD_DIM), lambda b, h: (b, 0, h)),
        out_shape=jax.ShapeDtypeStruct((bsz, seq, BRANCH), BF16),
        scratch_shapes=[
            pltpu.VMEM((seq // t, 2 * t, HEAD_DIM), BF16),
            pltpu.VMEM((HEAD_DIM, seq), BF16),
        ],
        compiler_params=_cparams(("parallel", "parallel")),
        name="diffattn",
    )(slopes, proj, proj, proj, proj, q_norm2, k_norm2, lq1, lk1, lq2, lk2, out_norm)


LRU_TILE = 512


def _lru_kernel(x_ref, gate_ref, cw_ref, cb_ref, wg_ref, br_ref, bi_ref, lam_ref, o_ref,
                a_scr, h_scr, *, seq):
    width = LRU_TILE
    x = x_ref[...].astype(F32)
    row = lax.broadcasted_iota(jnp.int32, (seq, width), 0)
    xc = x * cw_ref[CONV_WIDTH - 1:CONV_WIDTH, :] + cb_ref[...]
    for s in range(1, CONV_WIDTH):
        tap = cw_ref[CONV_WIDTH - 1 - s:CONV_WIDTH - s, :]
        xc = xc + jnp.where(row >= s, pltpu.roll(x, s, 0), 0.0) * tap
    lam = lam_ref[...]
    softplus_neg = jnp.maximum(-lam, 0.0) + jnp.log1p(jnp.exp(-jnp.abs(lam)))
    sub = lax.broadcasted_iota(jnp.int32, (seq, HEAD_DIM), 0) & (SUBLANES - 1)
    for n in range(width // HEAD_DIM):
        cols = slice(n * HEAD_DIM, (n + 1) * HEAD_DIM)
        xn = xc[:, cols]
        ri = jnp.dot(xn.astype(BF16), wg_ref[n], preferred_element_type=F32)
        r = jax.nn.sigmoid(ri[:, 0:HEAD_DIM] + br_ref[:, cols])
        gi = jax.nn.sigmoid(ri[:, HEAD_DIM:2 * HEAD_DIM] + bi_ref[:, cols])
        log_a = (-LRU_C) * r * softplus_neg[:, cols]
        a = jnp.exp(log_a)
        u = jnp.sqrt(1.0 - a * a) * (gi * xn)
        shift = 1
        while shift < SUBLANES:
            keep = sub >= shift
            u = jnp.where(keep, a * pltpu.roll(u, shift, 0) + u, u)
            a = jnp.where(keep, a * pltpu.roll(a, shift, 0), a)
            shift *= 2
        a_scr[:, cols] = a
        h_scr[:, cols] = u

    def body(tile, carry):
        rows = pl.ds(pl.multiple_of(tile * SUBLANES, SUBLANES), SUBLANES)
        h = a_scr[rows, :] * carry + h_scr[rows, :]
        h_scr[rows, :] = h
        return jnp.broadcast_to(h[SUBLANES - 1:SUBLANES, :], (SUBLANES, width))

    lax.fori_loop(0, seq // SUBLANES, body, jnp.zeros((SUBLANES, width), F32), unroll=8)
    o_ref[...] = (h_scr[...] * _silu(gate_ref[...].astype(F32))).astype(o_ref.dtype)


def _lru(proj, conv_w, conv_b, w_gate, b_r, b_i, lam):
    bsz, seq, _ = proj.shape
    t = LRU_TILE
    nblk = t // HEAD_DIM
    kern = functools.partial(_lru_kernel, seq=seq)
    vec = pl.BlockSpec((1, t), lambda b, c: (0, c))
    return pl.pallas_call(
        kern,
        grid=(bsz, BRANCH // t),
        in_specs=[
            pl.BlockSpec((None, seq, t), lambda b, c: (b, 0, OFF_RX // t + c)),
            pl.BlockSpec((None, seq, t), lambda b, c: (b, 0, OFF_GATE_LRU // t + c)),
            pl.BlockSpec((CONV_WIDTH, t), lambda b, c: (0, c)),
            vec,
            pl.BlockSpec((nblk, HEAD_DIM, 2 * HEAD_DIM), lambda b, c: (c, 0, 0)),
            vec, vec, vec,
        ],
        out_specs=pl.BlockSpec((None, seq, t), lambda b, c: (b, 0, c)),
        out_shape=jax.ShapeDtypeStruct((bsz, seq, BRANCH), BF16),
        scratch_shapes=[pltpu.VMEM((seq, t), F32), pltpu.VMEM((seq, t), F32)],
        compiler_params=_cparams(("parallel", "parallel")),
        name="rglru",
    )(proj, proj, conv_w, conv_b, w_gate, b_r, b_i, lam)


def _outproj_kernel(y0, y1, y2, y3, w_ref, x_ref, o_ref):
    acc = x_ref[...]
    for k, y in enumerate((y0, y1, y2, y3)):
        acc = acc + jnp.dot(y[...], w_ref[k * BRANCH:(k + 1) * BRANCH, :],
                            preferred_element_type=F32)
    o_ref[...] = acc


def _outproj(ys, w_out, x2, layer, tm=1024, tn=1024):
    m, d = x2.shape
    return pl.pallas_call(
        _outproj_kernel,
        grid=(m // tm, d // tn),
        in_specs=[pl.BlockSpec((tm, BRANCH), lambda i, j: (i, 0)) for _ in range(4)] + [
            pl.BlockSpec((None, d, tn), lambda i, j: (layer, 0, j)),
            pl.BlockSpec((tm, tn), lambda i, j: (i, j)),
        ],
        out_specs=pl.BlockSpec((tm, tn), lambda i, j: (i, j)),
        out_shape=jax.ShapeDtypeStruct((m, d), F32),
        compiler_params=_cparams(("parallel", "arbitrary")),
        name="outproj",
    )(*ys, w_out, x2)


def _alibi_slopes(n):
    return 2.0 ** (-8.0 * jnp.arange(1, n + 1, dtype=F32) / n)


def _layer(x2, bsz, seq, layer, w_main, w_a, w_out, p):
    h = _norm(x2, p['norm_w'].reshape(1, -1))
    proj2, a2 = _inproj(h, w_main, w_a, layer)
    proj = proj2.reshape(bsz, seq, N_MAIN)
    a_lr = a2.reshape(bsz, seq, LANES)

    w_up = jnp.pad(p['gla_w_up'], ((0, LANES - GLA_RANK), (0, 0)))
    y_gla = _gla(proj, a_lr, w_up, p['gla_b_up'].reshape(1, -1), p['gla_norm_w'].reshape(1, -1))

    y_swa = _swa(proj, p['swa_q_norm'].reshape(1, -1), p['swa_k_norm'].reshape(1, -1),
                 _alibi_slopes(SWA_GROUP * SWA_KV_HEADS), p['swa_sinks'])

    lambda_init = 0.8 - 0.6 * math.exp(-0.3 * layer)
    tile2 = lambda v: jnp.concatenate([v, v]).reshape(1, -1)
    y_diff = _diff(proj, _alibi_slopes(DIFF_HEADS), tile2(p['diff_q_norm']), tile2(p['diff_k_norm']),
                   p['diff_lq1'].reshape(1, -1), p['diff_lk1'].reshape(1, -1),
                   p['diff_lq2'].reshape(1, -1), p['diff_lk2'].reshape(1, -1),
                   p['diff_out_norm'].reshape(1, -1), lambda_init)

    w_gate = jnp.concatenate([p['lru_w_r'], p['lru_w_i']], axis=-1).astype(BF16)
    y_lru = _lru(proj, p['lru_conv_w'], p['lru_conv_b'].reshape(1, -1), w_gate,
                 p['lru_b_r'].reshape(1, -1), p['lru_b_i'].reshape(1, -1),
                 p['lru_lambda'].reshape(1, -1))

    ys = [y.reshape(bsz * seq, BRANCH) for y in (y_gla, y_swa, y_diff, y_lru)]
    return _outproj(ys, w_out, x2, layer)


def kernel(x, norm_w, w_in, w_out, gla_w_up, gla_b_up, gla_norm_w, swa_q_norm, swa_k_norm, swa_sinks, diff_q_norm, diff_k_norm, diff_lq1, diff_lk1, diff_lq2, diff_lk2, diff_out_norm, lru_conv_w, lru_conv_b, lru_w_r, lru_b_r, lru_w_i, lru_b_i, lru_lambda):
    params = dict(norm_w=norm_w, gla_w_up=gla_w_up, gla_b_up=gla_b_up,
                  gla_norm_w=gla_norm_w, swa_q_norm=swa_q_norm, swa_k_norm=swa_k_norm,
                  swa_sinks=swa_sinks, diff_q_norm=diff_q_norm, diff_k_norm=diff_k_norm,
                  diff_lq1=diff_lq1, diff_lk1=diff_lk1, diff_lq2=diff_lq2, diff_lk2=diff_lk2,
                  diff_out_norm=diff_out_norm, lru_conv_w=lru_conv_w, lru_conv_b=lru_conv_b,
                  lru_w_r=lru_w_r, lru_b_r=lru_b_r, lru_w_i=lru_w_i, lru_b_i=lru_b_i,
                  lru_lambda=lru_lambda)
    bsz, seq, d = x.shape
    w_main, w_a = _prep_w_in(w_in)
    w_out_b = _prep_w_out(w_out)
    x2 = x.reshape(bsz * seq, d)
    for layer in range(norm_w.shape[0]):
        x2 = _layer(x2, bsz, seq, layer, w_main, w_a, w_out_b,
                    {k: v[layer] for k, v in params.items()})
    return x2.reshape(bsz, seq, d)
```

```python
import functools
import math

import jax
import jax.numpy as jnp
from jax import lax
from jax.experimental import pallas as pl
from jax.experimental.pallas import tpu as pltpu

F32 = jnp.float32
BF16 = jnp.bfloat16

D_MODEL = 4096
HEAD_DIM = 128
BRANCH = D_MODEL // 4
NORM_EPS = 1e-6
GLA_DK = 64
GLA_RANK = 16
GLA_TAU = 16.0
GLA_CHUNK = 64
SWA_WINDOW = 128
SWA_GROUP = 4
SWA_KV_HEADS = 2
DIFF_DQK = 64
DIFF_HEADS = 8
LRU_C = 8.0
CONV_WIDTH = 4
LANES = 128
SUBLANES = 8
LOG2E = math.log2(math.e)

OFF_GQ, OFF_GK, OFF_GV = 0, 512, 1024
OFF_SQ = 2048
OFF_DQ, OFF_DK, OFF_DV = 3072, 4096, 5120
OFF_RX = 6144
OFF_GATE = 7168
OFF_SK, OFF_SV = 11264, 11520
N_MAIN = 11776
OFF_GATE_GLA = OFF_GATE
OFF_GATE_SWA = OFF_GATE + BRANCH
OFF_GATE_DIFF = OFF_GATE + 2 * BRANCH
OFF_GATE_LRU = OFF_GATE + 3 * BRANCH
PREP_TN = 512
ORIG_A_ROW = 2048
N_PLAIN_TILES = 2048 // PREP_TN
N_SQ_END_TILE = 3072 // PREP_TN
LAST_TILE = N_MAIN // PREP_TN - 1

VMEM_LIMIT = 56 * 1024 * 1024


def _cparams(sem):
    return pltpu.CompilerParams(dimension_semantics=sem, vmem_limit_bytes=VMEM_LIMIT)


def _rms(x, w):
    ms = jnp.mean(x * x, axis=-1, keepdims=True)
    return x * lax.rsqrt(ms + NORM_EPS) * w


def _silu(g):
    return g * jax.nn.sigmoid(g)


def _prep_win_kernel(a_ref, b_ref, c_ref, o_ref, oa_ref):
    j = pl.program_id(1)
    keep = PREP_TN - GLA_RANK

    @pl.when(j < N_PLAIN_TILES)
    def _():
        o_ref[...] = a_ref[...].astype(BF16)

    @pl.when(j >= N_PLAIN_TILES)
    def _():
        o_ref[0:keep, :] = a_ref[GLA_RANK:, :].astype(BF16)
        o_ref[keep:, :] = b_ref[...].astype(BF16)

    @pl.when(j == 0)
    def _():
        oa_ref[0:GLA_RANK, :] = c_ref[...].astype(BF16)
        oa_ref[GLA_RANK:, :] = jnp.zeros((LANES - GLA_RANK, oa_ref.shape[1]), BF16)


def _src_tile(j):
    return jnp.where(j < N_SQ_END_TILE, j, jnp.where(j == LAST_TILE, N_SQ_END_TILE, j + 1))


def _prep_w_in(w_in_t):
    depth, _, d = w_in_t.shape
    ratio = PREP_TN // GLA_RANK
    return pl.pallas_call(
        _prep_win_kernel,
        grid=(depth, N_MAIN // PREP_TN),
        in_specs=[
            pl.BlockSpec((None, PREP_TN, d), lambda l, j: (l, _src_tile(j), 0)),
            pl.BlockSpec((None, GLA_RANK, d), lambda l, j: (l, ratio * _src_tile(j) + ratio, 0)),
            pl.BlockSpec((None, GLA_RANK, d), lambda l, j: (l, ORIG_A_ROW // GLA_RANK, 0)),
        ],
        out_specs=[
            pl.BlockSpec((None, PREP_TN, d), lambda l, j: (l, j, 0)),
            pl.BlockSpec((None, LANES, d), lambda l, j: (l, 0, 0)),
        ],
        out_shape=[jax.ShapeDtypeStruct((depth, N_MAIN, d), BF16),
                   jax.ShapeDtypeStruct((depth, LANES, d), BF16)],
        compiler_params=_cparams(("parallel", "arbitrary")),
        name="prep_w_in",
    )(w_in_t, w_in_t, w_in_t)


def _cast_kernel(x_ref, o_ref):
    o_ref[...] = x_ref[...].astype(o_ref.dtype)


def _prep_w_out(w_out, rows=512):
    depth, k, n = w_out.shape
    return pl.pallas_call(
        _cast_kernel,
        grid=(depth, k // rows),
        in_specs=[pl.BlockSpec((None, rows, n), lambda l, r: (l, r, 0))],
        out_specs=pl.BlockSpec((None, rows, n), lambda l, r: (l, r, 0)),
        out_shape=jax.ShapeDtypeStruct((depth, k, n), BF16),
        compiler_params=_cparams(("parallel", "parallel")),
        name="prep_w_out",
    )(w_out)


def _norm_kernel(x_ref, w_ref, o_ref):
    o_ref[...] = _rms(x_ref[...], w_ref[...]).astype(o_ref.dtype)


def _norm(x2, w, tm=512):
    m, d = x2.shape
    return pl.pallas_call(
        _norm_kernel,
        grid=(m // tm,),
        in_specs=[pl.BlockSpec((tm, d), lambda i: (i, 0)),
                  pl.BlockSpec((1, d), lambda i: (0, 0))],
        out_specs=pl.BlockSpec((tm, d), lambda i: (i, 0)),
        out_shape=jax.ShapeDtypeStruct((m, d), BF16),
        compiler_params=_cparams(("parallel",)),
        name="rmsnorm",
    )(x2, w)


def _nt_dot(a, b):
    return lax.dot_general(a, b, (((1,), (1,)), ((), ())), preferred_element_type=F32)


def _inproj_kernel(h_ref, w_ref, wa_ref, o_ref, a_ref):
    h = h_ref[...]
    o_ref[...] = _nt_dot(h, w_ref[...]).astype(o_ref.dtype)

    @pl.when(pl.program_id(1) == 0)
    def _():
        a_ref[...] = _nt_dot(h, wa_ref[...])


def _inproj(h, w_main, w_a, layer, tm=1024, tn=512):
    m, d = h.shape
    n = w_main.shape[1]
    return pl.pallas_call(
        _inproj_kernel,
        grid=(m // tm, n // tn),
        in_specs=[pl.BlockSpec((tm, d), lambda i, j: (i, 0)),
                  pl.BlockSpec((None, tn, d), lambda i, j: (layer, j, 0)),
                  pl.BlockSpec((None, LANES, d), lambda i, j: (layer, 0, 0))],
        out_specs=[pl.BlockSpec((tm, tn), lambda i, j: (i, j)),
                   pl.BlockSpec((tm, LANES), lambda i, j: (i, 0))],
        out_shape=[jax.ShapeDtypeStruct((m, n), BF16),
                   jax.ShapeDtypeStruct((m, LANES), F32)],
        compiler_params=_cparams(("parallel", "arbitrary")),
        name="inproj",
    )(h, w_main, w_a)


def _gla_kernel(q_ref, k_ref, v_ref, gate_ref, a_ref, wup_ref, bup_ref, nw_ref, o_ref, *, seq):
    c = GLA_CHUNK
    nc = seq // c
    dv2 = 2 * HEAD_DIM
    logit = jnp.dot(a_ref[...], wup_ref[...], preferred_element_type=F32) + bup_ref[...]
    g = (jnp.minimum(logit, 0.0) - jnp.log1p(jnp.exp(-jnp.abs(logit)))) * (1.0 / GLA_TAU)
    row = lax.broadcasted_iota(jnp.int32, (seq, LANES), 0) & (c - 1)
    b = g
    shift = 1
    while shift < c:
        b = b + jnp.where(row >= shift, pltpu.roll(b, shift, 0), 0.0)
        shift *= 2
    b3 = b.reshape(nc, c, LANES)
    b_last = b3[:, c - 1:c, :]
    q3 = q_ref[...].astype(F32).reshape(nc, c, LANES)
    k3 = k_ref[...].astype(F32).reshape(nc, c, LANES)
    v3 = v_ref[...].reshape(nc, c, dv2)
    q_dec = q3 * (GLA_DK ** -0.5) * jnp.exp(b3)
    k_dec = (k3 * jnp.exp(-b3)).astype(BF16)
    k_state = (k3 * jnp.exp(b_last - b3)).astype(BF16)
    decay = jnp.exp(b_last)
    lane = lax.broadcasted_iota(jnp.int32, (1, 1, LANES), 2)
    q_st = jnp.concatenate([jnp.where(lane < GLA_DK, q_dec, 0.0),
                            jnp.where(lane >= GLA_DK, q_dec, 0.0)], axis=1).astype(BF16)
    att = jnp.einsum('nid,njd->nij', q_st, k_dec, preferred_element_type=F32)
    ii = lax.broadcasted_iota(jnp.int32, (1, 2 * c, c), 1) & (c - 1)
    jj = lax.broadcasted_iota(jnp.int32, (1, 2 * c, c), 2)
    att = jnp.where(ii >= jj, att, 0.0).astype(BF16)
    o_all = jnp.einsum('nij,njv->niv', att, v3, preferred_element_type=F32)
    v3_t = jnp.swapaxes(v3.astype(F32), 1, 2).astype(BF16)
    u_t = jnp.einsum('nvj,njd->nvd', v3_t, k_state, preferred_element_type=F32)
    st = jnp.zeros((dv2, LANES), F32)
    starts = []
    for n in range(nc):
        starts.append(st.astype(BF16))
        st = st * decay[n] + u_t[n]
    s_start = jnp.stack(starts, axis=0)
    o_all = o_all + jnp.einsum('nid,nvd->niv', q_st, s_start, preferred_element_type=F32)
    nw = nw_ref[...]
    o0 = _rms(o_all[:, 0:c, 0:HEAD_DIM], nw).reshape(seq, HEAD_DIM)
    o1 = _rms(o_all[:, c:2 * c, HEAD_DIM:dv2], nw).reshape(seq, HEAD_DIM)
    o_ref[:, 0:HEAD_DIM] = (o0 * _silu(gate_ref[:, 0:HEAD_DIM].astype(F32))).astype(o_ref.dtype)
    o_ref[:, HEAD_DIM:dv2] = (o1 * _silu(gate_ref[:, HEAD_DIM:dv2].astype(F32))).astype(o_ref.dtype)


def _gla(proj, a_lr, w_up, b_up, norm_w):
    bsz, seq, _ = proj.shape
    pair = 2 * HEAD_DIM
    kern = functools.partial(_gla_kernel, seq=seq)
    return pl.pallas_call(
        kern,
        grid=(bsz, BRANCH // pair),
        in_specs=[
            pl.BlockSpec((None, seq, LANES), lambda b, p: (b, 0, OFF_GQ // LANES + p)),
            pl.BlockSpec((None, seq, LANES), lambda b, p: (b, 0, OFF_GK // LANES + p)),
            pl.BlockSpec((None, seq, pair), lambda b, p: (b, 0, OFF_GV // pair + p)),
            pl.BlockSpec((None, seq, pair), lambda b, p: (b, 0, OFF_GATE_GLA // pair + p)),
            pl.BlockSpec((None, seq, LANES), lambda b, p: (b, 0, 0)),
            pl.BlockSpec((LANES, LANES), lambda b, p: (0, p)),
            pl.BlockSpec((1, LANES), lambda b, p: (0, p)),
            pl.BlockSpec((1, HEAD_DIM), lambda b, p: (0, 0)),
        ],
        out_specs=pl.BlockSpec((None, seq, pair), lambda b, p: (b, 0, p)),
        out_shape=jax.ShapeDtypeStruct((bsz, seq, BRANCH), BF16),
        compiler_params=_cparams(("parallel", "parallel")),
        name="gla",
    )(proj, proj, proj, proj, a_lr, w_up, b_up, norm_w)


def _swa_kernel(slope_ref, sink_ref, q_ref, k_ref, v_ref, gate_ref, qn_ref, kn_ref, o_ref,
                qs_ref, ks_ref, *, seq):
    w = SWA_WINDOW
    nb = seq // w
    g_n = SWA_GROUP
    kvh = pl.program_id(1)
    ks_ref[...] = _rms(k_ref[...].astype(F32), kn_ref[...]).astype(BF16)
    for g in range(g_n):
        qg = q_ref[:, g * HEAD_DIM:(g + 1) * HEAD_DIM].astype(F32)
        qs_ref[:, g * HEAD_DIM:(g + 1) * HEAD_DIM] = (
            _rms(qg, qn_ref[...]) * (HEAD_DIM ** -0.5)).astype(BF16)
    qi = lax.broadcasted_iota(jnp.int32, (w, 2 * w), 0)
    kj = lax.broadcasted_iota(jnp.int32, (w, 2 * w), 1)
    dist = qi + w - kj
    valid = (dist >= 0) & (dist < w)
    distf = dist.astype(F32)
    biases, sinks = [], []
    for g in range(g_n):
        slope = slope_ref[kvh * g_n + g]
        biases.append(jnp.where(valid, -slope * distf, -jnp.inf))
        sinks.append(sink_ref[kvh * g_n + g])

    def block(q_rows, k_win, v_win, bias_cols):
        q_st = jnp.concatenate(
            [qs_ref[q_rows, g * HEAD_DIM:(g + 1) * HEAD_DIM] for g in range(g_n)], axis=0)
        s = _nt_dot(q_st, k_win)
        ps, dens = [], []
        for g in range(g_n):
            sg = s[g * w:(g + 1) * w] + biases[g][:, bias_cols]
            m = jnp.maximum(jnp.max(sg, axis=-1, keepdims=True), sinks[g])
            p = jnp.exp(sg - m)
            dens.append(jnp.sum(p, axis=-1, keepdims=True) + jnp.exp(sinks[g] - m))
            ps.append(p.astype(BF16))
        o = jnp.dot(jnp.concatenate(ps, axis=0), v_win, preferred_element_type=F32)
        for g in range(g_n):
            cols = slice(g * HEAD_DIM, (g + 1) * HEAD_DIM)
            og = o[g * w:(g + 1) * w] / dens[g]
            o_ref[q_rows, cols] = (og * _silu(gate_ref[q_rows, cols].astype(F32))).astype(o_ref.dtype)

    block(pl.ds(0, w), ks_ref[0:w, :], v_ref[0:w, :], slice(w, 2 * w))

    def body(n, carry):
        q_rows = pl.ds(pl.multiple_of(n * w, w), w)
        win = pl.ds(pl.multiple_of((n - 1) * w, w), 2 * w)
        block(q_rows, ks_ref[win, :], v_ref[win, :], slice(0, 2 * w))
        return carry

    lax.fori_loop(1, nb, body, 0)


def _swa(proj, q_norm, k_norm, slopes, sinks):
    bsz, seq, _ = proj.shape
    qw = SWA_GROUP * HEAD_DIM
    kern = functools.partial(_swa_kernel, seq=seq)
    smem = pl.BlockSpec(memory_space=pltpu.SMEM)
    return pl.pallas_call(
        kern,
        grid=(bsz, SWA_KV_HEADS),
        in_specs=[
            smem, smem,
            pl.BlockSpec((None, seq, qw), lambda b, h: (b, 0, OFF_SQ // qw + h)),
            pl.BlockSpec((None, seq, HEAD_DIM), lambda b, h: (b, 0, OFF_SK // HEAD_DIM + h)),
            pl.BlockSpec((None, seq, HEAD_DIM), lambda b, h: (b, 0, OFF_SV // HEAD_DIM + h)),
            pl.BlockSpec((None, seq, qw), lambda b, h: (b, 0, OFF_GATE_SWA // qw + h)),
            pl.BlockSpec((1, HEAD_DIM), lambda b, h: (0, 0)),
            pl.BlockSpec((1, HEAD_DIM), lambda b, h: (0, 0)),
        ],
        out_specs=pl.BlockSpec((None, seq, qw), lambda b, h: (b, 0, h)),
        out_shape=jax.ShapeDtypeStruct((bsz, seq, BRANCH), BF16),
        scratch_shapes=[pltpu.VMEM((seq, qw), BF16), pltpu.VMEM((seq, HEAD_DIM), BF16)],
        compiler_params=_cparams(("parallel", "parallel")),
        name="swa",
    )(slopes, sinks, proj, proj, proj, proj, q_norm, k_norm)


DIFF_TQ = 256


def _half_rms(x, w2):
    lane = lax.broadcasted_iota(jnp.int32, (1, LANES), 1)
    lo = lane < DIFF_DQK
    sq = x * x
    s_lo = jnp.sum(jnp.where(lo, sq, 0.0), axis=-1, keepdims=True)
    s_hi = jnp.sum(jnp.where(lo, 0.0, sq), axis=-1, keepdims=True)
    inv = jnp.where(lo, lax.rsqrt(s_lo / DIFF_DQK + NORM_EPS), lax.rsqrt(s_hi / DIFF_DQK + NORM_EPS))
    return x * inv * w2


def _diff_kernel(slope_ref, q_ref, k_ref, v_ref, gate_ref, qn_ref, kn_ref, lq1_ref, lk1_ref,
                 lq2_ref, lk2_ref, onw_ref, o_ref, qs_ref, kt_ref, *, seq, lambda_init):
    t = DIFF_TQ
    nq = seq // t
    slope = slope_ref[pl.program_id(1)]
    lam = (jnp.exp(jnp.sum(lq1_ref[...] * lk1_ref[...], axis=-1, keepdims=True))
           - jnp.exp(jnp.sum(lq2_ref[...] * lk2_ref[...], axis=-1, keepdims=True)) + lambda_init)
    kt_ref[...] = _half_rms(k_ref[...].astype(F32), kn_ref[...]).T.astype(BF16)
    qn = _half_rms(q_ref[...].astype(F32), qn_ref[...]) * (DIFF_DQK ** -0.5 * LOG2E)
    lane = lax.broadcasted_iota(jnp.int32, (1, LANES), 1)
    q_lo = jnp.where(lane < DIFF_DQK, qn, 0.0).astype(BF16)
    q_hi = jnp.where(lane < DIFF_DQK, 0.0, qn).astype(BF16)
    for i in range(nq):
        qs_ref[i, 0:t, :] = q_lo[i * t:(i + 1) * t]
        qs_ref[i, t:2 * t, :] = q_hi[i * t:(i + 1) * t]
    key_bias = lax.broadcasted_iota(jnp.int32, (1, seq), 1).astype(F32) * (slope * LOG2E)
    r = lax.broadcasted_iota(jnp.int32, (2 * t, t), 0) & (t - 1)
    c = lax.broadcasted_iota(jnp.int32, (2 * t, t), 1)
    causal = r >= c

    for i in range(nq):
        kk = (i + 1) * t
        rows = slice(i * t, kk)
        s = jnp.dot(qs_ref[i], kt_ref[:, 0:kk], preferred_element_type=F32) + key_bias[:, 0:kk]
        s_diag = jnp.where(causal, s[:, kk - t:kk], -jnp.inf)
        s = s_diag if i == 0 else jnp.concatenate([s[:, 0:kk - t], s_diag], axis=1)
        m = jnp.max(s, axis=-1, keepdims=True)
        p = jnp.exp2(s - m)
        inv_l = 1.0 / jnp.sum(p, axis=-1, keepdims=True)
        o12 = jnp.dot(p.astype(BF16), v_ref[0:kk, :], preferred_element_type=F32) * inv_l
        o = o12[0:t] - lam * o12[t:2 * t]
        o = _rms(o, onw_ref[...]) * (1.0 - lambda_init)
        o_ref[rows, :] = (o * _silu(gate_ref[rows, :].astype(F32))).astype(o_ref.dtype)


def _diff(proj, slopes, q_norm2, k_norm2, lq1, lk1, lq2, lk2, out_norm, lambda_init):
    bsz, seq, _ = proj.shape
    t = DIFF_TQ
    kern = functools.partial(_diff_kernel, seq=seq, lambda_init=lambda_init)
    vec = lambda n: pl.BlockSpec((1, n), lambda b, h: (0, 0))
    blk = lambda off: pl.BlockSpec((None, seq, HEAD_DIM), lambda b, h: (b, 0, off // HEAD_DIM + h))
    return pl.pallas_call(
        kern,
        grid=(bsz, DIFF_HEADS),
        in_specs=[
            pl.BlockSpec(memory_space=pltpu.SMEM),
            blk(OFF_DQ), blk(OFF_DK), blk(OFF_DV), blk(OFF_GATE_DIFF),
            vec(HEAD_DIM), vec(HEAD_DIM),
            vec(DIFF_DQK), vec(DIFF_DQK), vec(DIFF_DQK), vec(DIFF_DQK),
            vec(HEAD_DIM),
        ],
        out_specs=pl.BlockSpec((None, seq, HEAD_DIM), lambda b, h: (b, 0, h)),
        out_shape=jax.ShapeDtypeStruct((bsz, seq, BRANCH), BF16),
        scratch_shapes=[
            pltpu.VMEM((seq // t, 2 * t, HEAD_DIM), BF16),
            pltpu.VMEM((HEAD_DIM, seq), BF16),
        ],
        compiler_params=_cparams(("parallel", "parallel")),
        name="diffattn",
    )(slopes, proj, proj, proj, proj, q_norm2, k_norm2, lq1, lk1, lq2, lk2, out_norm)


LRU_TILE = 512


def _lru_kernel(x_ref, gate_ref, cw_ref, cb_ref, wg_ref, br_ref, bi_ref, lam_ref, o_ref,
                a_scr, h_scr, *, seq):
    width = LRU_TILE
    x = x_ref[...].astype(F32)
    row = lax.broadcasted_iota(jnp.int32, (seq, width), 0)
    xc = x * cw_ref[CONV_WIDTH - 1:CONV_WIDTH, :] + cb_ref[...]
    for s in range(1, CONV_WIDTH):
        tap = cw_ref[CONV_WIDTH - 1 - s:CONV_WIDTH - s, :]
        xc = xc + jnp.where(row >= s, pltpu.roll(x, s, 0), 0.0) * tap
    lam = lam_ref[...]
    softplus_neg = jnp.maximum(-lam, 0.0) + jnp.log1p(jnp.exp(-jnp.abs(lam)))
    sub = lax.broadcasted_iota(jnp.int32, (seq, HEAD_DIM), 0) & (SUBLANES - 1)
    for n in range(width // HEAD_DIM):
        cols = slice(n * HEAD_DIM, (n + 1) * HEAD_DIM)
        xn = xc[:, cols]
        ri = jnp.dot(xn.astype(BF16), wg_ref[n], preferred_element_type=F32)
        r = jax.nn.sigmoid(ri[:, 0:HEAD_DIM] + br_ref[:, cols])
        gi = jax.nn.sigmoid(ri[:, HEAD_DIM:2 * HEAD_DIM] + bi_ref[:, cols])
        log_a = (-LRU_C) * r * softplus_neg[:, cols]
        a = jnp.exp(log_a)
        u = jnp.sqrt(1.0 - a * a) * (gi * xn)
        shift = 1
        while shift < SUBLANES:
            keep = sub >= shift
            u = jnp.where(keep, a * pltpu.roll(u, shift, 0) + u, u)
            a = jnp.where(keep, a * pltpu.roll(a, shift, 0), a)
            shift *= 2
        a_scr[:, cols] = a
        h_scr[:, cols] = u

    def body(tile, carry):
        rows = pl.ds(pl.multiple_of(tile * SUBLANES, SUBLANES), SUBLANES)
        h = a_scr[rows, :] * carry + h_scr[rows, :]
        h_scr[rows, :] = h
        return jnp.broadcast_to(h[SUBLANES - 1:SUBLANES, :], (SUBLANES, width))

    lax.fori_loop(0, seq // SUBLANES, body, jnp.zeros((SUBLANES, width), F32), unroll=8)
    o_ref[...] = (h_scr[...] * _silu(gate_ref[...].astype(F32))).astype(o_ref.dtype)


def _lru(proj, conv_w, conv_b, w_gate, b_r, b_i, lam):
    bsz, seq, _ = proj.shape
    t = LRU_TILE
    nblk = t // HEAD_DIM
    kern = functools.partial(_lru_kernel, seq=seq)
    vec = pl.BlockSpec((1, t), lambda b, c: (0, c))
    return pl.pallas_call(
        kern,
        grid=(bsz, BRANCH // t),
        in_specs=[
            pl.BlockSpec((None, seq, t), lambda b, c: (b, 0, OFF_RX // t + c)),
            pl.BlockSpec((None, seq, t), lambda b, c: (b, 0, OFF_GATE_LRU // t + c)),
            pl.BlockSpec((CONV_WIDTH, t), lambda b, c: (0, c)),
            vec,
            pl.BlockSpec((nblk, HEAD_DIM, 2 * HEAD_DIM), lambda b, c: (c, 0, 0)),
            vec, vec, vec,
        ],
        out_specs=pl.BlockSpec((None, seq, t), lambda b, c: (b, 0, c)),
        out_shape=jax.ShapeDtypeStruct((bsz, seq, BRANCH), BF16),
        scratch_shapes=[pltpu.VMEM((seq, t), F32), pltpu.VMEM((seq, t), F32)],
        compiler_params=_cparams(("parallel", "parallel")),
        name="rglru",
    )(proj, proj, conv_w, conv_b, w_gate, b_r, b_i, lam)


def _outproj_kernel(y0, y1, y2, y3, w_ref, x_ref, o_ref):
    acc = x_ref[...]
    for k, y in enumerate((y0, y1, y2, y3)):
        acc = acc + jnp.dot(y[...], w_ref[k * BRANCH:(k + 1) * BRANCH, :],
                            preferred_element_type=F32)
    o_ref[...] = acc


def _outproj(ys, w_out, x2, layer, tm=1024, tn=1024):
    m, d = x2.shape
    return pl.pallas_call(
        _outproj_kernel,
        grid=(m // tm, d // tn),
        in_specs=[pl.BlockSpec((tm, BRANCH), lambda i, j: (i, 0)) for _ in range(4)] + [
            pl.BlockSpec((None, d, tn), lambda i, j: (layer, 0, j)),
            pl.BlockSpec((tm, tn), lambda i, j: (i, j)),
        ],
        out_specs=pl.BlockSpec((tm, tn), lambda i, j: (i, j)),
        out_shape=jax.ShapeDtypeStruct((m, d), F32),
        compiler_params=_cparams(("parallel", "arbitrary")),
        name="outproj",
    )(*ys, w_out, x2)


def _alibi_slopes(n):
    return 2.0 ** (-8.0 * jnp.arange(1, n + 1, dtype=F32) / n)


def _layer(x2, bsz, seq, layer, w_main, w_a, w_out, p):
    h = _norm(x2, p['norm_w'].reshape(1, -1))
    proj2, a2 = _inproj(h, w_main, w_a, layer)
    proj = proj2.reshape(bsz, seq, N_MAIN)
    a_lr = a2.reshape(bsz, seq, LANES)

    w_up = jnp.pad(p['gla_w_up'], ((0, LANES - GLA_RANK), (0, 0)))
    y_gla = _gla(proj, a_lr, w_up, p['gla_b_up'].reshape(1, -1), p['gla_norm_w'].reshape(1, -1))

    y_swa = _swa(proj, p['swa_q_norm'].reshape(1, -1), p['swa_k_norm'].reshape(1, -1),
                 _alibi_slopes(SWA_GROUP * SWA_KV_HEADS), p['swa_sinks'])

    lambda_init = 0.8 - 0.6 * math.exp(-0.3 * layer)
    tile2 = lambda v: jnp.concatenate([v, v]).reshape(1, -1)
    y_diff = _diff(proj, _alibi_slopes(DIFF_HEADS), tile2(p['diff_q_norm']), tile2(p['diff_k_norm']),
                   p['diff_lq1'].reshape(1, -1), p['diff_lk1'].reshape(1, -1),
                   p['diff_lq2'].reshape(1, -1), p['diff_lk2'].reshape(1, -1),
                   p['diff_out_norm'].reshape(1, -1), lambda_init)

    w_gate = jnp.concatenate([p['lru_w_r'], p['lru_w_i']], axis=-1).astype(BF16)
    y_lru = _lru(proj, p['lru_conv_w'], p['lru_conv_b'].reshape(1, -1), w_gate,
                 p['lru_b_r'].reshape(1, -1), p['lru_b_i'].reshape(1, -1),
                 p['lru_lambda'].reshape(1, -1))

    ys = [y.reshape(bsz * seq, BRANCH) for y in (y_gla, y_swa, y_diff, y_lru)]
    return _outproj(ys, w_out, x2, layer)


def kernel(x, norm_w, w_in, w_out, gla_w_up, gla_b_up, gla_norm_w, swa_q_norm, swa_k_norm, swa_sinks, diff_q_norm, diff_k_norm, diff_lq1, diff_lk1, diff_lq2, diff_lk2, diff_out_norm, lru_conv_w, lru_conv_b, lru_w_r, lru_b_r, lru_w_i, lru_b_i, lru_lambda):
    params = dict(norm_w=norm_w, gla_w_up=gla_w_up, gla_b_up=gla_b_up,
                  gla_norm_w=gla_norm_w, swa_q_norm=swa_q_norm, swa_k_norm=swa_k_norm,
                  swa_sinks=swa_sinks, diff_q_norm=diff_q_norm, diff_k_norm=diff_k_norm,
                  diff_lq1=diff_lq1, diff_lk1=diff_lk1, diff_lq2=diff_lq2, diff_lk2=diff_lk2,
                  diff_out_norm=diff_out_norm, lru_conv_w=lru_conv_w, lru_conv_b=lru_conv_b,
                  lru_w_r=lru_w_r, lru_b_r=lru_b_r, lru_w_i=lru_w_i, lru_b_i=lru_b_i,
                  lru_lambda=lru_lambda)
    bsz, seq, d = x.shape
    w_main, w_a = _prep_w_in(jnp.swapaxes(w_in, 1, 2))
    w_out_b = _prep_w_out(w_out)
    x2 = x.reshape(bsz * seq, d)
    for layer in range(norm_w.shape[0]):
        x2 = _layer(x2, bsz, seq, layer, w_main, w_a, w_out_b,
                    {k: v[layer] for k, v in params.items()})
    return x2.reshape(bsz, seq, d)
```

```python
import functools
import math

import jax
import jax.numpy as jnp
from jax import lax
from jax.experimental import pallas as pl
from jax.experimental.pallas import tpu as pltpu

F32 = jnp.float32
BF16 = jnp.bfloat16

D_MODEL = 4096
HEAD_DIM = 128
BRANCH = D_MODEL // 4
NORM_EPS = 1e-6
GLA_DK = 64
GLA_RANK = 16
GLA_TAU = 16.0
GLA_CHUNK = 64
SWA_WINDOW = 128
SWA_GROUP = 4
SWA_KV_HEADS = 2
DIFF_DQK = 64
DIFF_HEADS = 8
LRU_C = 8.0
CONV_WIDTH = 4
LANES = 128
SUBLANES = 8
LOG2E = math.log2(math.e)

OFF_GQ, OFF_GK, OFF_GV = 0, 512, 1024
OFF_SQ = 2048
OFF_DQ, OFF_DK, OFF_DV = 3072, 4096, 5120
OFF_RX = 6144
OFF_GATE = 7168
OFF_SK, OFF_SV = 11264, 11520
N_MAIN = 11776
OFF_GATE_GLA = OFF_GATE
OFF_GATE_SWA = OFF_GATE + BRANCH
OFF_GATE_DIFF = OFF_GATE + 2 * BRANCH
OFF_GATE_LRU = OFF_GATE + 3 * BRANCH
PREP_TN = 512
ORIG_A_ROW = 2048
N_PLAIN_TILES = 2048 // PREP_TN
N_SQ_END_TILE = 3072 // PREP_TN
LAST_TILE = N_MAIN // PREP_TN - 1

VMEM_LIMIT = 56 * 1024 * 1024


def _cparams(sem):
    return pltpu.CompilerParams(dimension_semantics=sem, vmem_limit_bytes=VMEM_LIMIT)


def _rms(x, w):
    ms = jnp.mean(x * x, axis=-1, keepdims=True)
    return x * lax.rsqrt(ms + NORM_EPS) * w


def _silu(g):
    return g * jax.nn.sigmoid(g)


def _norm_kernel(x_ref, w_ref, o_ref):
    o_ref[...] = _rms(x_ref[...], w_ref[...]).astype(o_ref.dtype)


def _norm(x2, w, tm=512):
    m, d = x2.shape
    return pl.pallas_call(
        _norm_kernel,
        grid=(m // tm,),
        in_specs=[pl.BlockSpec((tm, d), lambda i: (i, 0)),
                  pl.BlockSpec((1, d), lambda i: (0, 0))],
        out_specs=pl.BlockSpec((tm, d), lambda i: (i, 0)),
        out_shape=jax.ShapeDtypeStruct((m, d), BF16),
        compiler_params=_cparams(("parallel",)),
        name="rmsnorm",
    )(x2, w)


def _nt_dot(a, b):
    return lax.dot_general(a, b, (((1,), (1,)), ((), ())), preferred_element_type=F32)


def _inproj_kernel(h_ref, w_ref, wa_ref, o_ref, a_ref):
    h = h_ref[...]
    o_ref[...] = _nt_dot(h, w_ref[0].astype(BF16)).astype(o_ref.dtype)

    @pl.when(pl.program_id(1) == 0)
    def _():
        pad = jnp.zeros((LANES - GLA_RANK, wa_ref.shape[1]), BF16)
        a_ref[...] = _nt_dot(h, jnp.concatenate([wa_ref[...].astype(BF16), pad], axis=0))


def _src_row(j):
    tile = jnp.where(j < N_SQ_END_TILE, j, jnp.where(j == LAST_TILE, N_SQ_END_TILE, j + 1))
    units = tile * (PREP_TN // GLA_RANK) + jnp.where(j < N_PLAIN_TILES, 0, 1)
    return pl.multiple_of(units * GLA_RANK, GLA_RANK)


def _inproj(h, w_in_t, layer, tm=1024):
    m, d = h.shape
    tn = PREP_TN
    return pl.pallas_call(
        _inproj_kernel,
        grid=(m // tm, N_MAIN // tn),
        in_specs=[pl.BlockSpec((tm, d), lambda i, j: (i, 0)),
                  pl.BlockSpec((pl.Element(1), pl.Element(tn), pl.Element(d)),
                               lambda i, j: (layer, _src_row(j), 0)),
                  pl.BlockSpec((None, GLA_RANK, d), lambda i, j: (layer, ORIG_A_ROW // GLA_RANK, 0))],
        out_specs=[pl.BlockSpec((tm, tn), lambda i, j: (i, j)),
                   pl.BlockSpec((tm, LANES), lambda i, j: (i, 0))],
        out_shape=[jax.ShapeDtypeStruct((m, N_MAIN), BF16),
                   jax.ShapeDtypeStruct((m, LANES), F32)],
        compiler_params=_cparams(("parallel", "arbitrary")),
        name="inproj",
    )(h, w_in_t, w_in_t)


def _gla_kernel(q_ref, k_ref, v_ref, gate_ref, a_ref, wup_ref, bup_ref, nw_ref, o_ref, *, seq):
    c = GLA_CHUNK
    nc = seq // c
    dv2 = 2 * HEAD_DIM
    logit = jnp.dot(a_ref[...], wup_ref[...], preferred_element_type=F32) + bup_ref[...]
    g = (jnp.minimum(logit, 0.0) - jnp.log1p(jnp.exp(-jnp.abs(logit)))) * (1.0 / GLA_TAU)
    row = lax.broadcasted_iota(jnp.int32, (seq, LANES), 0) & (c - 1)
    b = g
    shift = 1
    while shift < c:
        b = b + jnp.where(row >= shift, pltpu.roll(b, shift, 0), 0.0)
        shift *= 2
    b3 = b.reshape(nc, c, LANES)
    b_last = b3[:, c - 1:c, :]
    q3 = q_ref[...].astype(F32).reshape(nc, c, LANES)
    k3 = k_ref[...].astype(F32).reshape(nc, c, LANES)
    v3 = v_ref[...].reshape(nc, c, dv2)
    q_dec = q3 * (GLA_DK ** -0.5) * jnp.exp(b3)
    k_dec = (k3 * jnp.exp(-b3)).astype(BF16)
    k_state = (k3 * jnp.exp(b_last - b3)).astype(BF16)
    decay = jnp.exp(b_last)
    lane = lax.broadcasted_iota(jnp.int32, (1, 1, LANES), 2)
    q_st = jnp.concatenate([jnp.where(lane < GLA_DK, q_dec, 0.0),
                            jnp.where(lane >= GLA_DK, q_dec, 0.0)], axis=1).astype(BF16)
    att = jnp.einsum('nid,njd->nij', q_st, k_dec, preferred_element_type=F32)
    ii = lax.broadcasted_iota(jnp.int32, (1, 2 * c, c), 1) & (c - 1)
    jj = lax.broadcasted_iota(jnp.int32, (1, 2 * c, c), 2)
    att = jnp.where(ii >= jj, att, 0.0).astype(BF16)
    o_all = jnp.einsum('nij,njv->niv', att, v3, preferred_element_type=F32)
    v3_t = jnp.swapaxes(v3.astype(F32), 1, 2).astype(BF16)
    u_t = jnp.einsum('nvj,njd->nvd', v3_t, k_state, preferred_element_type=F32)
    st = jnp.zeros((dv2, LANES), F32)
    starts = []
    for n in range(nc):
        starts.append(st.astype(BF16))
        st = st * decay[n] + u_t[n]
    s_start = jnp.stack(starts, axis=0)
    o_all = o_all + jnp.einsum('nid,nvd->niv', q_st, s_start, preferred_element_type=F32)
    nw = nw_ref[...]
    o0 = _rms(o_all[:, 0:c, 0:HEAD_DIM], nw).reshape(seq, HEAD_DIM)
    o1 = _rms(o_all[:, c:2 * c, HEAD_DIM:dv2], nw).reshape(seq, HEAD_DIM)
    o_ref[:, 0:HEAD_DIM] = (o0 * _silu(gate_ref[:, 0:HEAD_DIM].astype(F32))).astype(o_ref.dtype)
    o_ref[:, HEAD_DIM:dv2] = (o1 * _silu(gate_ref[:, HEAD_DIM:dv2].astype(F32))).astype(o_ref.dtype)


def _gla(proj, a_lr, w_up, b_up, norm_w):
    bsz, seq, _ = proj.shape
    pair = 2 * HEAD_DIM
    kern = functools.partial(_gla_kernel, seq=seq)
    return pl.pallas_call(
        kern,
        grid=(bsz, BRANCH // pair),
        in_specs=[
            pl.BlockSpec((None, seq, LANES), lambda b, p: (b, 0, OFF_GQ // LANES + p)),
            pl.BlockSpec((None, seq, LANES), lambda b, p: (b, 0, OFF_GK // LANES + p)),
            pl.BlockSpec((None, seq, pair), lambda b, p: (b, 0, OFF_GV // pair + p)),
            pl.BlockSpec((None, seq, pair), lambda b, p: (b, 0, OFF_GATE_GLA // pair + p)),
            pl.BlockSpec((None, seq, LANES), lambda b, p: (b, 0, 0)),
            pl.BlockSpec((LANES, LANES), lambda b, p: (0, p)),
            pl.BlockSpec((1, LANES), lambda b, p: (0, p)),
            pl.BlockSpec((1, HEAD_DIM), lambda b, p: (0, 0)),
        ],
        out_specs=pl.BlockSpec((None, seq, pair), lambda b, p: (b, 0, p)),
        out_shape=jax.ShapeDtypeStruct((bsz, seq, BRANCH), BF16),
        compiler_params=_cparams(("parallel", "parallel")),
        name="gla",
    )(proj, proj, proj, proj, a_lr, w_up, b_up, norm_w)


def _swa_kernel(slope_ref, sink_ref, q_ref, k_ref, v_ref, gate_ref, qn_ref, kn_ref, o_ref,
                qs_ref, ks_ref, *, seq):
    w = SWA_WINDOW
    nb = seq // w
    g_n = SWA_GROUP
    kvh = pl.program_id(1)
    ks_ref[...] = _rms(k_ref[...].astype(F32), kn_ref[...]).astype(BF16)
    for g in range(g_n):
        qg = q_ref[:, g * HEAD_DIM:(g + 1) * HEAD_DIM].astype(F32)
        qs_ref[:, g * HEAD_DIM:(g + 1) * HEAD_DIM] = (
            _rms(qg, qn_ref[...]) * (HEAD_DIM ** -0.5)).astype(BF16)
    qi = lax.broadcasted_iota(jnp.int32, (w, 2 * w), 0)
    kj = lax.broadcasted_iota(jnp.int32, (w, 2 * w), 1)
    dist = qi + w - kj
    valid = (dist >= 0) & (dist < w)
    distf = dist.astype(F32)
    biases, sinks = [], []
    for g in range(g_n):
        slope = slope_ref[kvh * g_n + g]
        biases.append(jnp.where(valid, -slope * distf, -jnp.inf))
        sinks.append(sink_ref[kvh * g_n + g])

    def block(q_rows, k_win, v_win, bias_cols):
        q_st = jnp.concatenate(
            [qs_ref[q_rows, g * HEAD_DIM:(g + 1) * HEAD_DIM] for g in range(g_n)], axis=0)
        s = _nt_dot(q_st, k_win)
        ps, dens = [], []
        for g in range(g_n):
            sg = s[g * w:(g + 1) * w] + biases[g][:, bias_cols]
            m = jnp.maximum(jnp.max(sg, axis=-1, keepdims=True), sinks[g])
            p = jnp.exp(sg - m)
            dens.append(jnp.sum(p, axis=-1, keepdims=True) + jnp.exp(sinks[g] - m))
            ps.append(p.astype(BF16))
        o = jnp.dot(jnp.concatenate(ps, axis=0), v_win, preferred_element_type=F32)
        for g in range(g_n):
            cols = slice(g * HEAD_DIM, (g + 1) * HEAD_DIM)
            og = o[g * w:(g + 1) * w] / dens[g]
            o_ref[q_rows, cols] = (og * _silu(gate_ref[q_rows, cols].astype(F32))).astype(o_ref.dtype)

    block(pl.ds(0, w), ks_ref[0:w, :], v_ref[0:w, :], slice(w, 2 * w))

    def body(n, carry):
        q_rows = pl.ds(pl.multiple_of(n * w, w), w)
        win = pl.ds(pl.multiple_of((n - 1) * w, w), 2 * w)
        block(q_rows, ks_ref[win, :], v_ref[win, :], slice(0, 2 * w))
        return carry

    lax.fori_loop(1, nb, body, 0)


def _swa(proj, q_norm, k_norm, slopes, sinks):
    bsz, seq, _ = proj.shape
    qw = SWA_GROUP * HEAD_DIM
    kern = functools.partial(_swa_kernel, seq=seq)
    smem = pl.BlockSpec(memory_space=pltpu.SMEM)
    return pl.pallas_call(
        kern,
        grid=(bsz, SWA_KV_HEADS),
        in_specs=[
            smem, smem,
            pl.BlockSpec((None, seq, qw), lambda b, h: (b, 0, OFF_SQ // qw + h)),
            pl.BlockSpec((None, seq, HEAD_DIM), lambda b, h: (b, 0, OFF_SK // HEAD_DIM + h)),
            pl.BlockSpec((None, seq, HEAD_DIM), lambda b, h: (b, 0, OFF_SV // HEAD_DIM + h)),
            pl.BlockSpec((None, seq, qw), lambda b, h: (b, 0, OFF_GATE_SWA // qw + h)),
            pl.BlockSpec((1, HEAD_DIM), lambda b, h: (0, 0)),
            pl.BlockSpec((1, HEAD_DIM), lambda b, h: (0, 0)),
        ],
        out_specs=pl.BlockSpec((None, seq, qw), lambda b, h: (b, 0, h)),
        out_shape=jax.ShapeDtypeStruct((bsz, seq, BRANCH), BF16),
        scratch_shapes=[pltpu.VMEM((seq, qw), BF16), pltpu.VMEM((seq, HEAD_DIM), BF16)],
        compiler_params=_cparams(("parallel", "parallel")),
        name="swa",
    )(slopes, sinks, proj, proj, proj, proj, q_norm, k_norm)


DIFF_TQ = 256


def _half_rms(x, w2):
    lane = lax.broadcasted_iota(jnp.int32, (1, LANES), 1)
    lo = lane < DIFF_DQK
    sq = x * x
    s_lo = jnp.sum(jnp.where(lo, sq, 0.0), axis=-1, keepdims=True)
    s_hi = jnp.sum(jnp.where(lo, 0.0, sq), axis=-1, keepdims=True)
    inv = jnp.where(lo, lax.rsqrt(s_lo / DIFF_DQK + NORM_EPS), lax.rsqrt(s_hi / DIFF_DQK + NORM_EPS))
    return x * inv * w2


def _diff_kernel(slope_ref, q_ref, k_ref, v_ref, gate_ref, qn_ref, kn_ref, lq1_ref, lk1_ref,
                 lq2_ref, lk2_ref, onw_ref, o_ref, qs_ref, kt_ref, *, seq, lambda_init):
    t = DIFF_TQ
    nq = seq // t
    slope = slope_ref[pl.program_id(1)]
    lam = (jnp.exp(jnp.sum(lq1_ref[...] * lk1_ref[...], axis=-1, keepdims=True))
           - jnp.exp(jnp.sum(lq2_ref[...] * lk2_ref[...], axis=-1, keepdims=True)) + lambda_init)
    kt_ref[...] = _half_rms(k_ref[...].astype(F32), kn_ref[...]).T.astype(BF16)
    qn = _half_rms(q_ref[...].astype(F32), qn_ref[...]) * (DIFF_DQK ** -0.5 * LOG2E)
    lane = lax.broadcasted_iota(jnp.int32, (1, LANES), 1)
    q_lo = jnp.where(lane < DIFF_DQK, qn, 0.0).astype(BF16)
    q_hi = jnp.where(lane < DIFF_DQK, 0.0, qn).astype(BF16)
    for i in range(nq):
        qs_ref[i, 0:t, :] = q_lo[i * t:(i + 1) * t]
        qs_ref[i, t:2 * t, :] = q_hi[i * t:(i + 1) * t]
    key_bias = lax.broadcasted_iota(jnp.int32, (1, seq), 1).astype(F32) * (slope * LOG2E)
    r = lax.broadcasted_iota(jnp.int32, (2 * t, t), 0) & (t - 1)
    c = lax.broadcasted_iota(jnp.int32, (2 * t, t), 1)
    causal = r >= c

    for i in range(nq):
        kk = (i + 1) * t
        rows = slice(i * t, kk)
        s = jnp.dot(qs_ref[i], kt_ref[:, 0:kk], preferred_element_type=F32) + key_bias[:, 0:kk]
        s_diag = jnp.where(causal, s[:, kk - t:kk], -jnp.inf)
        s = s_diag if i == 0 else jnp.concatenate([s[:, 0:kk - t], s_diag], axis=1)
        m = jnp.max(s, axis=-1, keepdims=True)
        p = jnp.exp2(s - m)
        inv_l = 1.0 / jnp.sum(p, axis=-1, keepdims=True)
        o12 = jnp.dot(p.astype(BF16), v_ref[0:kk, :], preferred_element_type=F32) * inv_l
        o = o12[0:t] - lam * o12[t:2 * t]
        o = _rms(o, onw_ref[...]) * (1.0 - lambda_init)
        o_ref[rows, :] = (o * _silu(gate_ref[rows, :].astype(F32))).astype(o_ref.dtype)


def _diff(proj, slopes, q_norm2, k_norm2, lq1, lk1, lq2, lk2, out_norm, lambda_init):
    bsz, seq, _ = proj.shape
    t = DIFF_TQ
    kern = functools.partial(_diff_kernel, seq=seq, lambda_init=lambda_init)
    vec = lambda n: pl.BlockSpec((1, n), lambda b, h: (0, 0))
    blk = lambda off: pl.BlockSpec((None, seq, HEAD_DIM), lambda b, h: (b, 0, off // HEAD_DIM + h))
    return pl.pallas_call(
        kern,
        grid=(bsz, DIFF_HEADS),
        in_specs=[
            pl.BlockSpec(memory_space=pltpu.SMEM),
            blk(OFF_DQ), blk(OFF_DK), blk(OFF_DV), blk(OFF_GATE_DIFF),
            vec(HEAD_DIM), vec(HEAD_DIM),
            vec(DIFF_DQK), vec(DIFF_DQK), vec(DIFF_DQK), vec(DIFF_DQK),
            vec(HEAD_DIM),
        ],
        out_specs=pl.BlockSpec((None, seq, HEAD_DIM), lambda b, h: (b, 0, h)),
        out_shape=jax.ShapeDtypeStruct((bsz, seq, BRANCH), BF16),
        scratch_shapes=[
            pltpu.VMEM((seq // t, 2 * t, HEAD_DIM), BF16),
            pltpu.VMEM((HEAD_DIM, seq), BF16),
        ],
        compiler_params=_cparams(("parallel", "parallel")),
        name="diffattn",
    )(slopes, proj, proj, proj, proj, q_norm2, k_norm2, lq1, lk1, lq2, lk2, out_norm)


LRU_TILE = 512


def _lru_kernel(x_ref, gate_ref, cw_ref, cb_ref, wg_ref, br_ref, bi_ref, lam_ref, o_ref,
                a_scr, h_scr, *, seq):
    width = LRU_TILE
    x = x_ref[...].astype(F32)
    row = lax.broadcasted_iota(jnp.int32, (seq, width), 0)
    xc = x * cw_ref[CONV_WIDTH - 1:CONV_WIDTH, :] + cb_ref[...]
    for s in range(1, CONV_WIDTH):
        tap = cw_ref[CONV_WIDTH - 1 - s:CONV_WIDTH - s, :]
        xc = xc + jnp.where(row >= s, pltpu.roll(x, s, 0), 0.0) * tap
    lam = lam_ref[...]
    softplus_neg = jnp.maximum(-lam, 0.0) + jnp.log1p(jnp.exp(-jnp.abs(lam)))
    sub = lax.broadcasted_iota(jnp.int32, (seq, HEAD_DIM), 0) & (SUBLANES - 1)
    for n in range(width // HEAD_DIM):
        cols = slice(n * HEAD_DIM, (n + 1) * HEAD_DIM)
        xn = xc[:, cols]
        ri = jnp.dot(xn.astype(BF16), wg_ref[n], preferred_element_type=F32)
        r = jax.nn.sigmoid(ri[:, 0:HEAD_DIM] + br_ref[:, cols])
        gi = jax.nn.sigmoid(ri[:, HEAD_DIM:2 * HEAD_DIM] + bi_ref[:, cols])
        log_a = (-LRU_C) * r * softplus_neg[:, cols]
        a = jnp.exp(log_a)
        u = jnp.sqrt(1.0 - a * a) * (gi * xn)
        shift = 1
        while shift < SUBLANES:
            keep = sub >= shift
            u = jnp.where(keep, a * pltpu.roll(u, shift, 0) + u, u)
            a = jnp.where(keep, a * pltpu.roll(a, shift, 0), a)
            shift *= 2
        a_scr[:, cols] = a
        h_scr[:, cols] = u

    def body(tile, carry):
        rows = pl.ds(pl.multiple_of(tile * SUBLANES, SUBLANES), SUBLANES)
        h = a_scr[rows, :] * carry + h_scr[rows, :]
        h_scr[rows, :] = h
        return jnp.broadcast_to(h[SUBLANES - 1:SUBLANES, :], (SUBLANES, width))

    lax.fori_loop(0, seq // SUBLANES, body, jnp.zeros((SUBLANES, width), F32), unroll=8)
    o_ref[...] = (h_scr[...] * _silu(gate_ref[...].astype(F32))).astype(o_ref.dtype)


def _lru(proj, conv_w, conv_b, w_gate, b_r, b_i, lam):
    bsz, seq, _ = proj.shape
    t = LRU_TILE
    nblk = t // HEAD_DIM
    kern = functools.partial(_lru_kernel, seq=seq)
    vec = pl.BlockSpec((1, t), lambda b, c: (0, c))
    return pl.pallas_call(
        kern,
        grid=(bsz, BRANCH // t),
        in_specs=[
            pl.BlockSpec((None, seq, t), lambda b, c: (b, 0, OFF_RX // t + c)),
            pl.BlockSpec((None, seq, t), lambda b, c: (b, 0, OFF_GATE_LRU // t + c)),
            pl.BlockSpec((CONV_WIDTH, t), lambda b, c: (0, c)),
            vec,
            pl.BlockSpec((nblk, HEAD_DIM, 2 * HEAD_DIM), lambda b, c: (c, 0, 0)),
            vec, vec, vec,
        ],
        out_specs=pl.BlockSpec((None, seq, t), lambda b, c: (b, 0, c)),
        out_shape=jax.ShapeDtypeStruct((bsz, seq, BRANCH), BF16),
        scratch_shapes=[pltpu.VMEM((seq, t), F32), pltpu.VMEM((seq, t), F32)],
        compiler_params=_cparams(("parallel", "parallel")),
        name="rglru",
    )(proj, proj, conv_w, conv_b, w_gate, b_r, b_i, lam)


def _outproj_kernel(y0, y1, y2, y3, w_ref, x_ref, o_ref):
    acc = x_ref[...]
    for k, y in enumerate((y0, y1, y2, y3)):
        acc = acc + jnp.dot(y[...], w_ref[k * BRANCH:(k + 1) * BRANCH, :].astype(BF16),
                            preferred_element_type=F32)
    o_ref[...] = acc


def _outproj(ys, w_out, x2, layer, tm=1024, tn=512):
    m, d = x2.shape
    return pl.pallas_call(
        _outproj_kernel,
        grid=(m // tm, d // tn),
        in_specs=[pl.BlockSpec((tm, BRANCH), lambda i, j: (i, 0)) for _ in range(4)] + [
            pl.BlockSpec((None, d, tn), lambda i, j: (layer, 0, j)),
            pl.BlockSpec((tm, tn), lambda i, j: (i, j)),
        ],
        out_specs=pl.BlockSpec((tm, tn), lambda i, j: (i, j)),
        out_shape=jax.ShapeDtypeStruct((m, d), F32),
        compiler_params=_cparams(("parallel", "arbitrary")),
        name="outproj",
    )(*ys, w_out, x2)


def _alibi_slopes(n):
    return 2.0 ** (-8.0 * jnp.arange(1, n + 1, dtype=F32) / n)


def _layer(x2, bsz, seq, layer, w_in_t, w_out, p):
    h = _norm(x2, p['norm_w'].reshape(1, -1))
    proj2, a2 = _inproj(h, w_in_t, layer)
    proj = proj2.reshape(bsz, seq, N_MAIN)
    a_lr = a2.reshape(bsz, seq, LANES)

    w_up = jnp.pad(p['gla_w_up'], ((0, LANES - GLA_RANK), (0, 0)))
    y_gla = _gla(proj, a_lr, w_up, p['gla_b_up'].reshape(1, -1), p['gla_norm_w'].reshape(1, -1))

    y_swa = _swa(proj, p['swa_q_norm'].reshape(1, -1), p['swa_k_norm'].reshape(1, -1),
                 _alibi_slopes(SWA_GROUP * SWA_KV_HEADS), p['swa_sinks'])

    lambda_init = 0.8 - 0.6 * math.exp(-0.3 * layer)
    tile2 = lambda v: jnp.concatenate([v, v]).reshape(1, -1)
    y_diff = _diff(proj, _alibi_slopes(DIFF_HEADS), tile2(p['diff_q_norm']), tile2(p['diff_k_norm']),
                   p['diff_lq1'].reshape(1, -1), p['diff_lk1'].reshape(1, -1),
                   p['diff_lq2'].reshape(1, -1), p['diff_lk2'].reshape(1, -1),
                   p['diff_out_norm'].reshape(1, -1), lambda_init)

    w_gate = jnp.concatenate([p['lru_w_r'], p['lru_w_i']], axis=-1).astype(BF16)
    y_lru = _lru(proj, p['lru_conv_w'], p['lru_conv_b'].reshape(1, -1), w_gate,
                 p['lru_b_r'].reshape(1, -1), p['lru_b_i'].reshape(1, -1),
                 p['lru_lambda'].reshape(1, -1))

    ys = [y.reshape(bsz * seq, BRANCH) for y in (y_gla, y_swa, y_diff, y_lru)]
    return _outproj(ys, w_out, x2, layer)


def kernel(x, norm_w, w_in, w_out, gla_w_up, gla_b_up, gla_norm_w, swa_q_norm, swa_k_norm, swa_sinks, diff_q_norm, diff_k_norm, diff_lq1, diff_lk1, diff_lq2, diff_lk2, diff_out_norm, lru_conv_w, lru_conv_b, lru_w_r, lru_b_r, lru_w_i, lru_b_i, lru_lambda):
    params = dict(norm_w=norm_w, gla_w_up=gla_w_up, gla_b_up=gla_b_up,
                  gla_norm_w=gla_norm_w, swa_q_norm=swa_q_norm, swa_k_norm=swa_k_norm,
                  swa_sinks=swa_sinks, diff_q_norm=diff_q_norm, diff_k_norm=diff_k_norm,
                  diff_lq1=diff_lq1, diff_lk1=diff_lk1, diff_lq2=diff_lq2, diff_lk2=diff_lk2,
                  diff_out_norm=diff_out_norm, lru_conv_w=lru_conv_w, lru_conv_b=lru_conv_b,
                  lru_w_r=lru_w_r, lru_b_r=lru_b_r, lru_w_i=lru_w_i, lru_b_i=lru_b_i,
                  lru_lambda=lru_lambda)
    bsz, seq, d = x.shape
    w_in_t = jnp.swapaxes(w_in, 1, 2)
    x2 = x.reshape(bsz * seq, d)
    for layer in range(norm_w.shape[0]):
        x2 = _layer(x2, bsz, seq, layer, w_in_t, w_out,
                    {k: v[layer] for k, v in params.items()})
    return x2.reshape(bsz, seq, d)
```

```python
import functools
import math

import jax
import jax.numpy as jnp
from jax import lax
from jax.experimental import pallas as pl
from jax.experimental.pallas import tpu as pltpu

F32 = jnp.float32
BF16 = jnp.bfloat16

D_MODEL = 4096
HEAD_DIM = 128
BRANCH = D_MODEL // 4
NORM_EPS = 1e-6
GLA_DK = 64
GLA_RANK = 16
GLA_TAU = 16.0
GLA_CHUNK = 64
SWA_WINDOW = 128
SWA_GROUP = 4
SWA_KV_HEADS = 2
DIFF_DQK = 64
DIFF_HEADS = 8
LRU_C = 8.0
CONV_WIDTH = 4
LANES = 128
SUBLANES = 8
LOG2E = math.log2(math.e)

OFF_GQ, OFF_GK, OFF_GV = 0, 512, 1024
OFF_SQ = 2048
OFF_DQ, OFF_DK, OFF_DV = 3072, 4096, 5120
OFF_RX = 6144
OFF_GATE = 7168
OFF_SK, OFF_SV = 11264, 11520
N_MAIN = 11776
OFF_GATE_GLA = OFF_GATE
OFF_GATE_SWA = OFF_GATE + BRANCH
OFF_GATE_DIFF = OFF_GATE + 2 * BRANCH
OFF_GATE_LRU = OFF_GATE + 3 * BRANCH
PREP_TN = 512
ORIG_A_ROW = 2048
N_PLAIN_TILES = 2048 // PREP_TN
N_SQ_END_TILE = 3072 // PREP_TN
LAST_TILE = N_MAIN // PREP_TN - 1

VMEM_LIMIT = 56 * 1024 * 1024


def _cparams(sem):
    return pltpu.CompilerParams(dimension_semantics=sem, vmem_limit_bytes=VMEM_LIMIT)


def _rms(x, w):
    ms = jnp.mean(x * x, axis=-1, keepdims=True)
    return x * lax.rsqrt(ms + NORM_EPS) * w


def _silu(g):
    return g * jax.nn.sigmoid(g)


def _norm_kernel(x_ref, w_ref, o_ref):
    o_ref[...] = _rms(x_ref[...], w_ref[...]).astype(o_ref.dtype)


def _norm(x2, w, tm=512):
    m, d = x2.shape
    return pl.pallas_call(
        _norm_kernel,
        grid=(m // tm,),
        in_specs=[pl.BlockSpec((tm, d), lambda i: (i, 0)),
                  pl.BlockSpec((1, d), lambda i: (0, 0))],
        out_specs=pl.BlockSpec((tm, d), lambda i: (i, 0)),
        out_shape=jax.ShapeDtypeStruct((m, d), BF16),
        compiler_params=_cparams(("parallel",)),
        name="rmsnorm",
    )(x2, w)


def _nt_dot(a, b):
    return lax.dot_general(a, b, (((1,), (1,)), ((), ())), preferred_element_type=F32)


def _inproj_kernel(h_ref, w_ref, wa_ref, o_ref, a_ref):
    h = h_ref[...]
    o_ref[...] = _nt_dot(h, w_ref[0].astype(BF16)).astype(o_ref.dtype)

    @pl.when(pl.program_id(1) == 0)
    def _():
        pad = jnp.zeros((LANES - GLA_RANK, wa_ref.shape[1]), BF16)
        a_ref[...] = _nt_dot(h, jnp.concatenate([wa_ref[...].astype(BF16), pad], axis=0))


def _src_row(j):
    tile = jnp.where(j < N_SQ_END_TILE, j, jnp.where(j == LAST_TILE, N_SQ_END_TILE, j + 1))
    units = tile * (PREP_TN // GLA_RANK) + jnp.where(j < N_PLAIN_TILES, 0, 1)
    return pl.multiple_of(units * GLA_RANK, GLA_RANK)


def _inproj(h, w_in_t, layer, tm=1024):
    m, d = h.shape
    tn = PREP_TN
    return pl.pallas_call(
        _inproj_kernel,
        grid=(m // tm, N_MAIN // tn),
        in_specs=[pl.BlockSpec((tm, d), lambda i, j: (i, 0)),
                  pl.BlockSpec((pl.Element(1), pl.Element(tn), pl.Element(d)),
                               lambda i, j: (layer, _src_row(j), 0)),
                  pl.BlockSpec((None, GLA_RANK, d), lambda i, j: (layer, ORIG_A_ROW // GLA_RANK, 0))],
        out_specs=[pl.BlockSpec((tm, tn), lambda i, j: (i, j)),
                   pl.BlockSpec((tm, LANES), lambda i, j: (i, 0))],
        out_shape=[jax.ShapeDtypeStruct((m, N_MAIN), BF16),
                   jax.ShapeDtypeStruct((m, LANES), F32)],
        compiler_params=_cparams(("parallel", "arbitrary")),
        name="inproj",
    )(h, w_in_t, w_in_t)


def _gla_kernel(q_ref, k_ref, v_ref, gate_ref, a_ref, wup_ref, bup_ref, nw_ref, o_ref, *, seq):
    c = GLA_CHUNK
    nc = seq // c
    dv2 = 2 * HEAD_DIM
    logit = jnp.dot(a_ref[...], wup_ref[...], preferred_element_type=F32) + bup_ref[...]
    g = (jnp.minimum(logit, 0.0) - jnp.log1p(jnp.exp(-jnp.abs(logit)))) * (1.0 / GLA_TAU)
    row = lax.broadcasted_iota(jnp.int32, (seq, LANES), 0) & (c - 1)
    b = g
    shift = 1
    while shift < c:
        b = b + jnp.where(row >= shift, pltpu.roll(b, shift, 0), 0.0)
        shift *= 2
    b3 = b.reshape(nc, c, LANES)
    b_last = b3[:, c - 1:c, :]
    q3 = q_ref[...].astype(F32).reshape(nc, c, LANES)
    k3 = k_ref[...].astype(F32).reshape(nc, c, LANES)
    v3 = v_ref[...].reshape(nc, c, dv2)
    q_dec = q3 * (GLA_DK ** -0.5) * jnp.exp(b3)
    k_dec = (k3 * jnp.exp(-b3)).astype(BF16)
    k_state = (k3 * jnp.exp(b_last - b3)).astype(BF16)
    decay = jnp.exp(b_last)
    lane = lax.broadcasted_iota(jnp.int32, (1, 1, LANES), 2)
    q_st = jnp.concatenate([jnp.where(lane < GLA_DK, q_dec, 0.0),
                            jnp.where(lane >= GLA_DK, q_dec, 0.0)], axis=1).astype(BF16)
    att = jnp.einsum('nid,njd->nij', q_st, k_dec, preferred_element_type=F32)
    ii = lax.broadcasted_iota(jnp.int32, (1, 2 * c, c), 1) & (c - 1)
    jj = lax.broadcasted_iota(jnp.int32, (1, 2 * c, c), 2)
    att = jnp.where(ii >= jj, att, 0.0).astype(BF16)
    heads = ((slice(0, c), slice(0, HEAD_DIM)), (slice(c, 2 * c), slice(HEAD_DIM, dv2)))
    o_heads = [jnp.einsum('nij,njv->niv', att[:, rs], v3[:, :, vs], preferred_element_type=F32)
               for rs, vs in heads]
    v3_t = jnp.swapaxes(v3.astype(F32), 1, 2).astype(BF16)
    u_t = jnp.einsum('nvj,njd->nvd', v3_t, k_state, preferred_element_type=F32)
    st = jnp.zeros((dv2, LANES), F32)
    starts = []
    for n in range(nc):
        starts.append(st.astype(BF16))
        st = st * decay[n] + u_t[n]
    s_start = jnp.stack(starts, axis=0)
    nw = nw_ref[...]
    for (rs, vs), o_intra in zip(heads, o_heads):
        o = o_intra + jnp.einsum('nid,nvd->niv', q_st[:, rs], s_start[:, vs],
                                 preferred_element_type=F32)
        o = _rms(o, nw).reshape(seq, HEAD_DIM)
        o_ref[:, vs] = (o * _silu(gate_ref[:, vs].astype(F32))).astype(o_ref.dtype)


def _gla(proj, a_lr, w_up, b_up, norm_w):
    bsz, seq, _ = proj.shape
    pair = 2 * HEAD_DIM
    kern = functools.partial(_gla_kernel, seq=seq)
    return pl.pallas_call(
        kern,
        grid=(bsz, BRANCH // pair),
        in_specs=[
            pl.BlockSpec((None, seq, LANES), lambda b, p: (b, 0, OFF_GQ // LANES + p)),
            pl.BlockSpec((None, seq, LANES), lambda b, p: (b, 0, OFF_GK // LANES + p)),
            pl.BlockSpec((None, seq, pair), lambda b, p: (b, 0, OFF_GV // pair + p)),
            pl.BlockSpec((None, seq, pair), lambda b, p: (b, 0, OFF_GATE_GLA // pair + p)),
            pl.BlockSpec((None, seq, LANES), lambda b, p: (b, 0, 0)),
            pl.BlockSpec((LANES, LANES), lambda b, p: (0, p)),
            pl.BlockSpec((1, LANES), lambda b, p: (0, p)),
            pl.BlockSpec((1, HEAD_DIM), lambda b, p: (0, 0)),
        ],
        out_specs=pl.BlockSpec((None, seq, pair), lambda b, p: (b, 0, p)),
        out_shape=jax.ShapeDtypeStruct((bsz, seq, BRANCH), BF16),
        compiler_params=_cparams(("parallel", "parallel")),
        name="gla",
    )(proj, proj, proj, proj, a_lr, w_up, b_up, norm_w)


def _swa_kernel(slope_ref, sink_ref, q_ref, k_ref, v_ref, gate_ref, qn_ref, kn_ref, o_ref,
                qs_ref, ks_ref, *, seq):
    w = SWA_WINDOW
    nb = seq // w
    g_n = SWA_GROUP
    kvh = pl.program_id(1)
    ks_ref[...] = _rms(k_ref[...].astype(F32), kn_ref[...]).astype(BF16)
    for g in range(g_n):
        qg = q_ref[:, g * HEAD_DIM:(g + 1) * HEAD_DIM].astype(F32)
        qs_ref[:, g * HEAD_DIM:(g + 1) * HEAD_DIM] = (
            _rms(qg, qn_ref[...]) * (HEAD_DIM ** -0.5)).astype(BF16)
    qi = lax.broadcasted_iota(jnp.int32, (w, 2 * w), 0)
    kj = lax.broadcasted_iota(jnp.int32, (w, 2 * w), 1)
    dist = qi + w - kj
    valid = (dist >= 0) & (dist < w)
    distf = dist.astype(F32)
    biases, sinks = [], []
    for g in range(g_n):
        slope = slope_ref[kvh * g_n + g]
        biases.append(jnp.where(valid, -slope * distf, -jnp.inf))
        sinks.append(sink_ref[kvh * g_n + g])

    def block(q_rows, k_win, v_win, bias_cols):
        q_st = jnp.concatenate(
            [qs_ref[q_rows, g * HEAD_DIM:(g + 1) * HEAD_DIM] for g in range(g_n)], axis=0)
        s = _nt_dot(q_st, k_win)
        ps, dens = [], []
        for g in range(g_n):
            sg = s[g * w:(g + 1) * w] + biases[g][:, bias_cols]
            m = jnp.maximum(jnp.max(sg, axis=-1, keepdims=True), sinks[g])
            p = jnp.exp(sg - m)
            dens.append(jnp.sum(p, axis=-1, keepdims=True) + jnp.exp(sinks[g] - m))
            ps.append(p.astype(BF16))
        o = jnp.dot(jnp.concatenate(ps, axis=0), v_win, preferred_element_type=F32)
        for g in range(g_n):
            cols = slice(g * HEAD_DIM, (g + 1) * HEAD_DIM)
            og = o[g * w:(g + 1) * w] / dens[g]
            o_ref[q_rows, cols] = (og * _silu(gate_ref[q_rows, cols].astype(F32))).astype(o_ref.dtype)

    block(pl.ds(0, w), ks_ref[0:w, :], v_ref[0:w, :], slice(w, 2 * w))

    def body(n, carry):
        q_rows = pl.ds(pl.multiple_of(n * w, w), w)
        win = pl.ds(pl.multiple_of((n - 1) * w, w), 2 * w)
        block(q_rows, ks_ref[win, :], v_ref[win, :], slice(0, 2 * w))
        return carry

    lax.fori_loop(1, nb, body, 0, unroll=3)


def _swa(proj, q_norm, k_norm, slopes, sinks):
    bsz, seq, _ = proj.shape
    qw = SWA_GROUP * HEAD_DIM
    kern = functools.partial(_swa_kernel, seq=seq)
    smem = pl.BlockSpec(memory_space=pltpu.SMEM)
    return pl.pallas_call(
        kern,
        grid=(bsz, SWA_KV_HEADS),
        in_specs=[
            smem, smem,
            pl.BlockSpec((None, seq, qw), lambda b, h: (b, 0, OFF_SQ // qw + h)),
            pl.BlockSpec((None, seq, HEAD_DIM), lambda b, h: (b, 0, OFF_SK // HEAD_DIM + h)),
            pl.BlockSpec((None, seq, HEAD_DIM), lambda b, h: (b, 0, OFF_SV // HEAD_DIM + h)),
            pl.BlockSpec((None, seq, qw), lambda b, h: (b, 0, OFF_GATE_SWA // qw + h)),
            pl.BlockSpec((1, HEAD_DIM), lambda b, h: (0, 0)),
            pl.BlockSpec((1, HEAD_DIM), lambda b, h: (0, 0)),
        ],
        out_specs=pl.BlockSpec((None, seq, qw), lambda b, h: (b, 0, h)),
        out_shape=jax.ShapeDtypeStruct((bsz, seq, BRANCH), BF16),
        scratch_shapes=[pltpu.VMEM((seq, qw), BF16), pltpu.VMEM((seq, HEAD_DIM), BF16)],
        compiler_params=_cparams(("parallel", "parallel")),
        name="swa",
    )(slopes, sinks, proj, proj, proj, proj, q_norm, k_norm)


DIFF_TQ = 256


def _half_rms(x, w2):
    lane = lax.broadcasted_iota(jnp.int32, (1, LANES), 1)
    lo = lane < DIFF_DQK
    sq = x * x
    s_lo = jnp.sum(jnp.where(lo, sq, 0.0), axis=-1, keepdims=True)
    s_hi = jnp.sum(jnp.where(lo, 0.0, sq), axis=-1, keepdims=True)
    inv = jnp.where(lo, lax.rsqrt(s_lo / DIFF_DQK + NORM_EPS), lax.rsqrt(s_hi / DIFF_DQK + NORM_EPS))
    return x * inv * w2


def _diff_kernel(slope_ref, q_ref, k_ref, v_ref, gate_ref, qn_ref, kn_ref, lq1_ref, lk1_ref,
                 lq2_ref, lk2_ref, onw_ref, o_ref, qs_ref, kt_ref, va_ref, *, seq, lambda_init):
    t = DIFF_TQ
    nq = seq // t
    n_terms = 3
    slope = slope_ref[pl.program_id(1)]
    lam = (jnp.exp(jnp.sum(lq1_ref[...] * lk1_ref[...], axis=-1, keepdims=True))
           - jnp.exp(jnp.sum(lq2_ref[...] * lk2_ref[...], axis=-1, keepdims=True)) + lambda_init)
    lane = lax.broadcasted_iota(jnp.int32, (1, LANES), 1)
    kt_ref[0:HEAD_DIM, :] = _half_rms(k_ref[...].astype(F32), kn_ref[...]).T.astype(BF16)
    rest = lax.broadcasted_iota(jnp.int32, (1, seq), 1).astype(F32) * (slope * LOG2E)
    sub = lax.broadcasted_iota(jnp.int32, (HEAD_DIM, 1), 0)
    bias_rows = jnp.zeros((HEAD_DIM, seq), F32)
    for term in range(n_terms):
        piece = rest.astype(BF16).astype(F32)
        bias_rows = jnp.where(sub == term, piece, bias_rows)
        rest = rest - piece
    kt_ref[HEAD_DIM:2 * HEAD_DIM, :] = bias_rows.astype(BF16)
    va_ref[:, 0:HEAD_DIM] = v_ref[...]
    va_ref[:, HEAD_DIM:2 * HEAD_DIM] = jnp.broadcast_to(
        jnp.where(lane == 0, 1.0, 0.0).astype(BF16), (seq, HEAD_DIM))
    qn = _half_rms(q_ref[...].astype(F32), qn_ref[...]) * (DIFF_DQK ** -0.5 * LOG2E)
    q_lo = jnp.where(lane < DIFF_DQK, qn, 0.0).astype(BF16)
    q_hi = jnp.where(lane < DIFF_DQK, 0.0, qn).astype(BF16)
    ones = jnp.broadcast_to(jnp.where(lane < n_terms, 1.0, 0.0).astype(BF16), (2 * t, HEAD_DIM))
    for i in range(nq):
        qs_ref[i, 0:t, 0:HEAD_DIM] = q_lo[i * t:(i + 1) * t]
        qs_ref[i, t:2 * t, 0:HEAD_DIM] = q_hi[i * t:(i + 1) * t]
        qs_ref[i, :, HEAD_DIM:2 * HEAD_DIM] = ones
    r = lax.broadcasted_iota(jnp.int32, (2 * t, t), 0) & (t - 1)
    c = lax.broadcasted_iota(jnp.int32, (2 * t, t), 1)
    causal = r >= c

    def scores(i):
        kk = (i + 1) * t
        s = jnp.dot(qs_ref[i], kt_ref[:, 0:kk], preferred_element_type=F32)
        s_diag = jnp.where(causal, s[:, kk - t:kk], -jnp.inf)
        return s_diag if i == 0 else jnp.concatenate([s[:, 0:kk - t], s_diag], axis=1)

    order = list(range(nq - 1, -1, -1))
    s_next = scores(order[0])
    for pos, i in enumerate(order):
        kk = (i + 1) * t
        rows = slice(i * t, kk)
        s = s_next
        if pos + 1 < nq:
            s_next = scores(order[pos + 1])
        m = jnp.max(s, axis=-1, keepdims=True)
        p = jnp.exp2(s - m).astype(BF16)
        o_aug = jnp.dot(p, va_ref[0:kk, :], preferred_element_type=F32)
        o12 = o_aug[:, 0:HEAD_DIM] * (1.0 / o_aug[:, HEAD_DIM:HEAD_DIM + 1])
        o = o12[0:t] - lam * o12[t:2 * t]
        o = _rms(o, onw_ref[...]) * (1.0 - lambda_init)
        o_ref[rows, :] = (o * _silu(gate_ref[rows, :].astype(F32))).astype(o_ref.dtype)


def _diff(proj, slopes, q_norm2, k_norm2, lq1, lk1, lq2, lk2, out_norm, lambda_init):
    bsz, seq, _ = proj.shape
    t = DIFF_TQ
    kern = functools.partial(_diff_kernel, seq=seq, lambda_init=lambda_init)
    vec = lambda n: pl.BlockSpec((1, n), lambda b, h: (0, 0))
    blk = lambda off: pl.BlockSpec((None, seq, HEAD_DIM), lambda b, h: (b, 0, off // HEAD_DIM + h))
    return pl.pallas_call(
        kern,
        grid=(bsz, DIFF_HEADS),
        in_specs=[
            pl.BlockSpec(memory_space=pltpu.SMEM),
            blk(OFF_DQ), blk(OFF_DK), blk(OFF_DV), blk(OFF_GATE_DIFF),
            vec(HEAD_DIM), vec(HEAD_DIM),
            vec(DIFF_DQK), vec(DIFF_DQK), vec(DIFF_DQK), vec(DIFF_DQK),
            vec(HEAD_DIM),
        ],
        out_specs=pl.BlockSpec((None, seq, HEAD_DIM), lambda b, h: (b, 0, h)),
        out_shape=jax.ShapeDtypeStruct((bsz, seq, BRANCH), BF16),
        scratch_shapes=[
            pltpu.VMEM((seq // t, 2 * t, 2 * HEAD_DIM), BF16),
            pltpu.VMEM((2 * HEAD_DIM, seq), BF16),
            pltpu.VMEM((seq, 2 * HEAD_DIM), BF16),
        ],
        compiler_params=_cparams(("parallel", "parallel")),
        name="diffattn",
    )(slopes, proj, proj, proj, proj, q_norm2, k_norm2, lq1, lk1, lq2, lk2, out_norm)


LRU_TILE = 512


def _lru_kernel(x_ref, gate_ref, cw_ref, cb_ref, wg_ref, br_ref, bi_ref, lam_ref, o_ref,
                a_scr, h_scr, *, seq):
    width = LRU_TILE
    x = x_ref[...].astype(F32)
    row = lax.broadcasted_iota(jnp.int32, (seq, width), 0)
    xc = x * cw_ref[CONV_WIDTH - 1:CONV_WIDTH, :] + cb_ref[...]
    for s in range(1, CONV_WIDTH):
        tap = cw_ref[CONV_WIDTH - 1 - s:CONV_WIDTH - s, :]
        xc = xc + jnp.where(row >= s, pltpu.roll(x, s, 0), 0.0) * tap
    lam = lam_ref[...]
    softplus_neg = jnp.maximum(-lam, 0.0) + jnp.log1p(jnp.exp(-jnp.abs(lam)))
    sub = lax.broadcasted_iota(jnp.int32, (seq, HEAD_DIM), 0) & (SUBLANES - 1)
    for n in range(width // HEAD_DIM):
        cols = slice(n * HEAD_DIM, (n + 1) * HEAD_DIM)
        xn = xc[:, cols]
        ri = jnp.dot(xn.astype(BF16), wg_ref[n], preferred_element_type=F32)
        r = jax.nn.sigmoid(ri[:, 0:HEAD_DIM] + br_ref[:, cols])
        gi = jax.nn.sigmoid(ri[:, HEAD_DIM:2 * HEAD_DIM] + bi_ref[:, cols])
        log_a = (-LRU_C) * r * softplus_neg[:, cols]
        a = jnp.exp(log_a)
        u = jnp.sqrt(1.0 - a * a) * (gi * xn)
        shift = 1
        while shift < SUBLANES:
            keep = sub >= shift
            u = jnp.where(keep, a * pltpu.roll(u, shift, 0) + u, u)
            a = jnp.where(keep, a * pltpu.roll(a, shift, 0), a)
            shift *= 2
        a_scr[:, cols] = a
        h_scr[:, cols] = u

    def body(tile, carry):
        rows = pl.ds(pl.multiple_of(tile * SUBLANES, SUBLANES), SUBLANES)
        h = a_scr[rows, :] * carry + h_scr[rows, :]
        h_scr[rows, :] = h
        return jnp.broadcast_to(h[SUBLANES - 1:SUBLANES, :], (SUBLANES, width))

    lax.fori_loop(0, seq // SUBLANES, body, jnp.zeros((SUBLANES, width), F32), unroll=8)
    o_ref[...] = (h_scr[...] * _silu(gate_ref[...].astype(F32))).astype(o_ref.dtype)


def _lru(proj, conv_w, conv_b, w_gate, b_r, b_i, lam):
    bsz, seq, _ = proj.shape
    t = LRU_TILE
    nblk = t // HEAD_DIM
    kern = functools.partial(_lru_kernel, seq=seq)
    vec = pl.BlockSpec((1, t), lambda b, c: (0, c))
    return pl.pallas_call(
        kern,
        grid=(bsz, BRANCH // t),
        in_specs=[
            pl.BlockSpec((None, seq, t), lambda b, c: (b, 0, OFF_RX // t + c)),
            pl.BlockSpec((None, seq, t), lambda b, c: (b, 0, OFF_GATE_LRU // t + c)),
            pl.BlockSpec((CONV_WIDTH, t), lambda b, c: (0, c)),
            vec,
            pl.BlockSpec((nblk, HEAD_DIM, 2 * HEAD_DIM), lambda b, c: (c, 0, 0)),
            vec, vec, vec,
        ],
        out_specs=pl.BlockSpec((None, seq, t), lambda b, c: (b, 0, c)),
        out_shape=jax.ShapeDtypeStruct((bsz, seq, BRANCH), BF16),
        scratch_shapes=[pltpu.VMEM((seq, t), F32), pltpu.VMEM((seq, t), F32)],
        compiler_params=_cparams(("parallel", "parallel")),
        name="rglru",
    )(proj, proj, conv_w, conv_b, w_gate, b_r, b_i, lam)


def _outproj_kernel(y0, y1, y2, y3, w_ref, x_ref, o_ref):
    acc = x_ref[...]
    for k, y in enumerate((y0, y1, y2, y3)):
        acc = acc + jnp.dot(y[...], w_ref[k * BRANCH:(k + 1) * BRANCH, :].astype(BF16),
                            preferred_element_type=F32)
    o_ref[...] = acc


def _outproj(ys, w_out, x2, layer, tm=1024, tn=512):
    m, d = x2.shape
    return pl.pallas_call(
        _outproj_kernel,
        grid=(m // tm, d // tn),
        in_specs=[pl.BlockSpec((tm, BRANCH), lambda i, j: (i, 0)) for _ in range(4)] + [
            pl.BlockSpec((None, d, tn), lambda i, j: (layer, 0, j)),
            pl.BlockSpec((tm, tn), lambda i, j: (i, j)),
        ],
        out_specs=pl.BlockSpec((tm, tn), lambda i, j: (i, j)),
        out_shape=jax.ShapeDtypeStruct((m, d), F32),
        compiler_params=_cparams(("parallel", "arbitrary")),
        name="outproj",
    )(*ys, w_out, x2)


def _alibi_slopes(n):
    return 2.0 ** (-8.0 * jnp.arange(1, n + 1, dtype=F32) / n)


def _layer(x2, bsz, seq, layer, w_in_t, w_out, p):
    h = _norm(x2, p['norm_w'].reshape(1, -1))
    proj2, a2 = _inproj(h, w_in_t, layer)
    proj = proj2.reshape(bsz, seq, N_MAIN)
    a_lr = a2.reshape(bsz, seq, LANES)

    w_up = jnp.pad(p['gla_w_up'], ((0, LANES - GLA_RANK), (0, 0)))
    y_gla = _gla(proj, a_lr, w_up, p['gla_b_up'].reshape(1, -1), p['gla_norm_w'].reshape(1, -1))

    y_swa = _swa(proj, p['swa_q_norm'].reshape(1, -1), p['swa_k_norm'].reshape(1, -1),
                 _alibi_slopes(SWA_GROUP * SWA_KV_HEADS), p['swa_sinks'])

    lambda_init = 0.8 - 0.6 * math.exp(-0.3 * layer)
    tile2 = lambda v: jnp.concatenate([v, v]).reshape(1, -1)
    y_diff = _diff(proj, _alibi_slopes(DIFF_HEADS), tile2(p['diff_q_norm']), tile2(p['diff_k_norm']),
                   p['diff_lq1'].reshape(1, -1), p['diff_lk1'].reshape(1, -1),
                   p['diff_lq2'].reshape(1, -1), p['diff_lk2'].reshape(1, -1),
                   p['diff_out_norm'].reshape(1, -1), lambda_init)

    w_gate = jnp.concatenate([p['lru_w_r'], p['lru_w_i']], axis=-1).astype(BF16)
    y_lru = _lru(proj, p['lru_conv_w'], p['lru_conv_b'].reshape(1, -1), w_gate,
                 p['lru_b_r'].reshape(1, -1), p['lru_b_i'].reshape(1, -1),
                 p['lru_lambda'].reshape(1, -1))

    ys = [y.reshape(bsz * seq, BRANCH) for y in (y_gla, y_swa, y_diff, y_lru)]
    return _outproj(ys, w_out, x2, layer)


def kernel(x, norm_w, w_in, w_out, gla_w_up, gla_b_up, gla_norm_w, swa_q_norm, swa_k_norm, swa_sinks, diff_q_norm, diff_k_norm, diff_lq1, diff_lk1, diff_lq2, diff_lk2, diff_out_norm, lru_conv_w, lru_conv_b, lru_w_r, lru_b_r, lru_w_i, lru_b_i, lru_lambda):
    params = dict(norm_w=norm_w, gla_w_up=gla_w_up, gla_b_up=gla_b_up,
                  gla_norm_w=gla_norm_w, swa_q_norm=swa_q_norm, swa_k_norm=swa_k_norm,
                  swa_sinks=swa_sinks, diff_q_norm=diff_q_norm, diff_k_norm=diff_k_norm,
                  diff_lq1=diff_lq1, diff_lk1=diff_lk1, diff_lq2=diff_lq2, diff_lk2=diff_lk2,
                  diff_out_norm=diff_out_norm, lru_conv_w=lru_conv_w, lru_conv_b=lru_conv_b,
                  lru_w_r=lru_w_r, lru_b_r=lru_b_r, lru_w_i=lru_w_i, lru_b_i=lru_b_i,
                  lru_lambda=lru_lambda)
    bsz, seq, d = x.shape
    w_in_t = jnp.swapaxes(w_in, 1, 2)
    x2 = x.reshape(bsz * seq, d)
    for layer in range(norm_w.shape[0]):
        x2 = _layer(x2, bsz, seq, layer, w_in_t, w_out,
                    {k: v[layer] for k, v in params.items()})
    return x2.reshape(bsz, seq, d)
```

```python
import functools
import math

import jax
import jax.numpy as jnp
from jax import lax
from jax.experimental import pallas as pl
from jax.experimental.pallas import tpu as pltpu

F32 = jnp.float32
BF16 = jnp.bfloat16

D_MODEL = 4096
HEAD_DIM = 128
BRANCH = D_MODEL // 4
NORM_EPS = 1e-6
GLA_DK = 64
GLA_RANK = 16
GLA_TAU = 16.0
GLA_CHUNK = 64
SWA_WINDOW = 128
SWA_GROUP = 4
SWA_KV_HEADS = 2
DIFF_DQK = 64
DIFF_HEADS = 8
LRU_C = 8.0
CONV_WIDTH = 4
LANES = 128
SUBLANES = 8
LOG2E = math.log2(math.e)

OFF_GQ, OFF_GK, OFF_GV = 0, 512, 1024
OFF_SQ = 2048
OFF_DQ, OFF_DK, OFF_DV = 3072, 4096, 5120
OFF_RX = 6144
OFF_GATE = 7168
OFF_SK, OFF_SV = 11264, 11520
N_MAIN = 11776
OFF_GATE_GLA = OFF_GATE
OFF_GATE_SWA = OFF_GATE + BRANCH
OFF_GATE_DIFF = OFF_GATE + 2 * BRANCH
OFF_GATE_LRU = OFF_GATE + 3 * BRANCH
PREP_TN = 512
ORIG_A_ROW = 2048
N_PLAIN_TILES = 2048 // PREP_TN
N_SQ_END_TILE = 3072 // PREP_TN
LAST_TILE = N_MAIN // PREP_TN - 1

VMEM_LIMIT = 56 * 1024 * 1024


def _cparams(sem):
    return pltpu.CompilerParams(dimension_semantics=sem, vmem_limit_bytes=VMEM_LIMIT)


def _rms(x, w):
    ms = jnp.mean(x * x, axis=-1, keepdims=True)
    return x * lax.rsqrt(ms + NORM_EPS) * w


def _silu(g):
    return g * jax.nn.sigmoid(g)


def _norm_kernel(x_ref, w_ref, o_ref):
    o_ref[...] = _rms(x_ref[...], w_ref[...]).astype(o_ref.dtype)


def _norm(x2, w, tm=512):
    m, d = x2.shape
    return pl.pallas_call(
        _norm_kernel,
        grid=(m // tm,),
        in_specs=[pl.BlockSpec((tm, d), lambda i: (i, 0)),
                  pl.BlockSpec((1, d), lambda i: (0, 0))],
        out_specs=pl.BlockSpec((tm, d), lambda i: (i, 0)),
        out_shape=jax.ShapeDtypeStruct((m, d), BF16),
        compiler_params=_cparams(("parallel",)),
        name="rmsnorm",
    )(x2, w)


def _nt_dot(a, b):
    return lax.dot_general(a, b, (((1,), (1,)), ((), ())), preferred_element_type=F32)


def _inproj_kernel(h_ref, w_ref, wa_ref, o_ref, a_ref):
    h = h_ref[...]
    o_ref[...] = _nt_dot(h, w_ref[0].astype(BF16)).astype(o_ref.dtype)

    @pl.when(pl.program_id(1) == 0)
    def _():
        pad = jnp.zeros((LANES - GLA_RANK, wa_ref.shape[1]), BF16)
        a_ref[...] = _nt_dot(h, jnp.concatenate([wa_ref[...].astype(BF16), pad], axis=0))


def _src_row(j):
    tile = jnp.where(j < N_SQ_END_TILE, j, jnp.where(j == LAST_TILE, N_SQ_END_TILE, j + 1))
    units = tile * (PREP_TN // GLA_RANK) + jnp.where(j < N_PLAIN_TILES, 0, 1)
    return pl.multiple_of(units * GLA_RANK, GLA_RANK)


def _inproj(h, w_in_t, layer, tm=1024):
    m, d = h.shape
    tn = PREP_TN
    return pl.pallas_call(
        _inproj_kernel,
        grid=(m // tm, N_MAIN // tn),
        in_specs=[pl.BlockSpec((tm, d), lambda i, j: (i, 0)),
                  pl.BlockSpec((pl.Element(1), pl.Element(tn), pl.Element(d)),
                               lambda i, j: (layer, _src_row(j), 0)),
                  pl.BlockSpec((None, GLA_RANK, d), lambda i, j: (layer, ORIG_A_ROW // GLA_RANK, 0))],
        out_specs=[pl.BlockSpec((tm, tn), lambda i, j: (i, j)),
                   pl.BlockSpec((tm, LANES), lambda i, j: (i, 0))],
        out_shape=[jax.ShapeDtypeStruct((m, N_MAIN), BF16),
                   jax.ShapeDtypeStruct((m, LANES), F32)],
        compiler_params=_cparams(("parallel", "arbitrary")),
        name="inproj",
    )(h, w_in_t, w_in_t)


def _gla_kernel(q_ref, k_ref, v_ref, gate_ref, a_ref, wup_ref, bup_ref, nw_ref, o_ref, *, seq):
    c = GLA_CHUNK
    nc = seq // c
    dv2 = 2 * HEAD_DIM
    logit = jnp.dot(a_ref[...], wup_ref[...], preferred_element_type=F32) + bup_ref[...]
    g = (jnp.minimum(logit, 0.0) - jnp.log(1.0 + jnp.exp(-jnp.abs(logit)))) * (LOG2E / GLA_TAU)
    row = lax.broadcasted_iota(jnp.int32, (seq, LANES), 0) & (c - 1)
    b = g
    shift = 1
    while shift < SUBLANES:
        b = b + jnp.where(row >= shift, pltpu.roll(b, shift, 0), 0.0)
        shift *= 2
    b4 = b.reshape(nc, c // SUBLANES, SUBLANES, LANES)
    skip = 1
    while skip < c // SUBLANES:
        b4 = jnp.concatenate([b4[:, :skip], b4[:, skip:] + b4[:, :-skip]], axis=1)
        skip *= 2
    b3 = b4.reshape(nc, c, LANES)
    b_last = b3[:, c - 1:c, :]
    q3 = q_ref[...].astype(F32).reshape(nc, c, LANES)
    k3 = k_ref[...].astype(F32).reshape(nc, c, LANES)
    v3 = v_ref[...].reshape(nc, c, dv2)
    q_dec = q3 * (GLA_DK ** -0.5) * jnp.exp2(b3)
    k_dec = (k3 * jnp.exp2(-b3)).astype(BF16)
    k_state = (k3 * jnp.exp2(b_last - b3)).astype(BF16)
    decay = jnp.exp2(b_last)
    lane = lax.broadcasted_iota(jnp.int32, (1, 1, LANES), 2)
    q_st = jnp.concatenate([jnp.where(lane < GLA_DK, q_dec, 0.0),
                            jnp.where(lane >= GLA_DK, q_dec, 0.0)], axis=1).astype(BF16)
    att = jnp.einsum('nid,njd->nij', q_st, k_dec, preferred_element_type=F32)
    ii = lax.broadcasted_iota(jnp.int32, (1, 2 * c, c), 1) & (c - 1)
    jj = lax.broadcasted_iota(jnp.int32, (1, 2 * c, c), 2)
    att = jnp.where(ii >= jj, att, 0.0).astype(BF16)
    heads = ((slice(0, c), slice(0, HEAD_DIM)), (slice(c, 2 * c), slice(HEAD_DIM, dv2)))
    o_heads = [jnp.einsum('nij,njv->niv', att[:, rs], v3[:, :, vs], preferred_element_type=F32)
               for rs, vs in heads]
    v3_t = jnp.swapaxes(v3.astype(F32), 1, 2).astype(BF16)
    u_t = jnp.einsum('nvj,njd->nvd', v3_t, k_state, preferred_element_type=F32)
    st = jnp.zeros((dv2, LANES), F32)
    starts = []
    for n in range(nc):
        starts.append(st.astype(BF16))
        st = st * decay[n] + u_t[n]
    s_start = jnp.stack(starts, axis=0)
    nw = nw_ref[...]
    for (rs, vs), o_intra in zip(heads, o_heads):
        o = o_intra + jnp.einsum('nid,nvd->niv', q_st[:, rs], s_start[:, vs],
                                 preferred_element_type=F32)
        o = _rms(o, nw).reshape(seq, HEAD_DIM)
        o_ref[:, vs] = (o * _silu(gate_ref[:, vs].astype(F32))).astype(o_ref.dtype)


def _gla(proj, a_lr, w_up, b_up, norm_w):
    bsz, seq, _ = proj.shape
    pair = 2 * HEAD_DIM
    kern = functools.partial(_gla_kernel, seq=seq)
    return pl.pallas_call(
        kern,
        grid=(bsz, BRANCH // pair),
        in_specs=[
            pl.BlockSpec((None, seq, LANES), lambda b, p: (b, 0, OFF_GQ // LANES + p)),
            pl.BlockSpec((None, seq, LANES), lambda b, p: (b, 0, OFF_GK // LANES + p)),
            pl.BlockSpec((None, seq, pair), lambda b, p: (b, 0, OFF_GV // pair + p)),
            pl.BlockSpec((None, seq, pair), lambda b, p: (b, 0, OFF_GATE_GLA // pair + p)),
            pl.BlockSpec((None, seq, LANES), lambda b, p: (b, 0, 0)),
            pl.BlockSpec((LANES, LANES), lambda b, p: (0, p)),
            pl.BlockSpec((1, LANES), lambda b, p: (0, p)),
            pl.BlockSpec((1, HEAD_DIM), lambda b, p: (0, 0)),
        ],
        out_specs=pl.BlockSpec((None, seq, pair), lambda b, p: (b, 0, p)),
        out_shape=jax.ShapeDtypeStruct((bsz, seq, BRANCH), BF16),
        compiler_params=_cparams(("parallel", "parallel")),
        name="gla",
    )(proj, proj, proj, proj, a_lr, w_up, b_up, norm_w)


def _swa_kernel(slope_ref, sink_ref, q_ref, k_ref, v_ref, gate_ref, qn_ref, kn_ref, o_ref,
                qs_ref, ks_ref, *, seq):
    w = SWA_WINDOW
    nb = seq // w
    g_n = SWA_GROUP
    kvh = pl.program_id(1)
    ks_ref[...] = _rms(k_ref[...].astype(F32), kn_ref[...]).astype(BF16)
    for g in range(g_n):
        qg = q_ref[:, g * HEAD_DIM:(g + 1) * HEAD_DIM].astype(F32)
        qs_ref[:, g * HEAD_DIM:(g + 1) * HEAD_DIM] = (
            _rms(qg, qn_ref[...]) * (HEAD_DIM ** -0.5)).astype(BF16)
    qi = lax.broadcasted_iota(jnp.int32, (w, 2 * w), 0)
    kj = lax.broadcasted_iota(jnp.int32, (w, 2 * w), 1)
    dist = qi + w - kj
    valid = (dist >= 0) & (dist < w)
    distf = dist.astype(F32)
    biases, sinks = [], []
    for g in range(g_n):
        slope = slope_ref[kvh * g_n + g]
        biases.append(jnp.where(valid, -slope * distf, -jnp.inf))
        sinks.append(sink_ref[kvh * g_n + g])

    def block(q_rows, k_win, v_win, bias_cols):
        q_st = jnp.concatenate(
            [qs_ref[q_rows, g * HEAD_DIM:(g + 1) * HEAD_DIM] for g in range(g_n)], axis=0)
        s = _nt_dot(q_st, k_win)
        ps, dens = [], []
        for g in range(g_n):
            sg = s[g * w:(g + 1) * w] + biases[g][:, bias_cols]
            m = jnp.maximum(jnp.max(sg, axis=-1, keepdims=True), sinks[g])
            p = jnp.exp(sg - m)
            dens.append(jnp.sum(p, axis=-1, keepdims=True) + jnp.exp(sinks[g] - m))
            ps.append(p.astype(BF16))
        o = jnp.dot(jnp.concatenate(ps, axis=0), v_win, preferred_element_type=F32)
        for g in range(g_n):
            cols = slice(g * HEAD_DIM, (g + 1) * HEAD_DIM)
            og = o[g * w:(g + 1) * w] / dens[g]
            o_ref[q_rows, cols] = (og * _silu(gate_ref[q_rows, cols].astype(F32))).astype(o_ref.dtype)

    block(pl.ds(0, w), ks_ref[0:w, :], v_ref[0:w, :], slice(w, 2 * w))

    def body(n, carry):
        q_rows = pl.ds(pl.multiple_of(n * w, w), w)
        win = pl.ds(pl.multiple_of((n - 1) * w, w), 2 * w)
        block(q_rows, ks_ref[win, :], v_ref[win, :], slice(0, 2 * w))
        return carry

    lax.fori_loop(1, nb, body, 0, unroll=3)


def _swa(proj, q_norm, k_norm, slopes, sinks):
    bsz, seq, _ = proj.shape
    qw = SWA_GROUP * HEAD_DIM
    kern = functools.partial(_swa_kernel, seq=seq)
    smem = pl.BlockSpec(memory_space=pltpu.SMEM)
    return pl.pallas_call(
        kern,
        grid=(bsz, SWA_KV_HEADS),
        in_specs=[
            smem, smem,
            pl.BlockSpec((None, seq, qw), lambda b, h: (b, 0, OFF_SQ // qw + h)),
            pl.BlockSpec((None, seq, HEAD_DIM), lambda b, h: (b, 0, OFF_SK // HEAD_DIM + h)),
            pl.BlockSpec((None, seq, HEAD_DIM), lambda b, h: (b, 0, OFF_SV // HEAD_DIM + h)),
            pl.BlockSpec((None, seq, qw), lambda b, h: (b, 0, OFF_GATE_SWA // qw + h)),
            pl.BlockSpec((1, HEAD_DIM), lambda b, h: (0, 0)),
            pl.BlockSpec((1, HEAD_DIM), lambda b, h: (0, 0)),
        ],
        out_specs=pl.BlockSpec((None, seq, qw), lambda b, h: (b, 0, h)),
        out_shape=jax.ShapeDtypeStruct((bsz, seq, BRANCH), BF16),
        scratch_shapes=[pltpu.VMEM((seq, qw), BF16), pltpu.VMEM((seq, HEAD_DIM), BF16)],
        compiler_params=_cparams(("parallel", "parallel")),
        name="swa",
    )(slopes, sinks, proj, proj, proj, proj, q_norm, k_norm)


DIFF_TQ = 256
DIFF_HEADS_PER_STEP = 2


def _group_sum(x, group):
    gi = lax.broadcasted_iota(jnp.int32, (LANES, LANES), 0) // group
    gj = lax.broadcasted_iota(jnp.int32, (LANES, LANES), 1) // group
    ones = jnp.where(gi == gj, 1.0, 0.0).astype(BF16)
    hi = x.astype(BF16)
    lo = (x - hi.astype(F32)).astype(BF16)
    return (jnp.dot(hi, ones, preferred_element_type=F32)
            + jnp.dot(lo, ones, preferred_element_type=F32))


def _half_rms(x, w2):
    ms = _group_sum(x * x, DIFF_DQK) * (1.0 / DIFF_DQK)
    return x * lax.rsqrt(ms + NORM_EPS) * w2


def _diff_kernel(slope_ref, q_ref, k_ref, v_ref, gate_ref, qn_ref, kn_ref, lq1_ref, lk1_ref,
                 lq2_ref, lk2_ref, onw_ref, o_ref, qs_ref, kt_ref, va_ref, *, seq, lambda_init):
    t = DIFF_TQ
    nq = seq // t
    n_terms = 3
    lam = (jnp.exp(jnp.sum(lq1_ref[...] * lk1_ref[...], axis=-1, keepdims=True))
           - jnp.exp(jnp.sum(lq2_ref[...] * lk2_ref[...], axis=-1, keepdims=True)) + lambda_init)
    lane = lax.broadcasted_iota(jnp.int32, (1, LANES), 1)
    sub = lax.broadcasted_iota(jnp.int32, (HEAD_DIM, 1), 0)
    key_pos = lax.broadcasted_iota(jnp.int32, (1, seq), 1).astype(F32)
    ones = jnp.broadcast_to(jnp.where(lane < n_terms, 1.0, 0.0).astype(BF16), (2 * t, HEAD_DIM))
    one_cols = jnp.ones((seq, HEAD_DIM), BF16)
    r = lax.broadcasted_iota(jnp.int32, (2 * t, t), 0) & (t - 1)
    c = lax.broadcasted_iota(jnp.int32, (2 * t, t), 1)
    causal = r >= c

    def prepare(hh):
        cols = slice(hh * HEAD_DIM, (hh + 1) * HEAD_DIM)
        slope = slope_ref[pl.program_id(1) * DIFF_HEADS_PER_STEP + hh]
        kt_ref[hh, 0:HEAD_DIM, :] = _half_rms(k_ref[:, cols].astype(F32), kn_ref[...]).T.astype(BF16)
        rest = key_pos * (slope * LOG2E)
        bias_rows = jnp.zeros((HEAD_DIM, seq), F32)
        for term in range(n_terms):
            piece = rest.astype(BF16).astype(F32)
            bias_rows = jnp.where(sub == term, piece, bias_rows)
            rest = rest - piece
        kt_ref[hh, HEAD_DIM:2 * HEAD_DIM, :] = bias_rows.astype(BF16)
        va_ref[hh, :, 0:HEAD_DIM] = v_ref[:, cols]
        va_ref[hh, :, HEAD_DIM:2 * HEAD_DIM] = one_cols
        qn = _half_rms(q_ref[:, cols].astype(F32), qn_ref[...]) * (DIFF_DQK ** -0.5 * LOG2E)
        q_lo = jnp.where(lane < DIFF_DQK, qn, 0.0).astype(BF16)
        q_hi = jnp.where(lane < DIFF_DQK, 0.0, qn).astype(BF16)
        for i in range(nq):
            qs_ref[hh, i, 0:t, 0:HEAD_DIM] = q_lo[i * t:(i + 1) * t]
            qs_ref[hh, i, t:2 * t, 0:HEAD_DIM] = q_hi[i * t:(i + 1) * t]
            qs_ref[hh, i, :, HEAD_DIM:2 * HEAD_DIM] = ones

    def scores(hh, i):
        kk = (i + 1) * t
        s = jnp.dot(qs_ref[hh, i], kt_ref[hh, :, 0:kk], preferred_element_type=F32)
        s_diag = jnp.where(causal, s[:, kk - t:kk], -jnp.inf)
        return s_diag if i == 0 else jnp.concatenate([s[:, 0:kk - t], s_diag], axis=1)

    def finish(hh, i, s):
        kk = (i + 1) * t
        rows = slice(i * t, kk)
        cols = slice(hh * HEAD_DIM, (hh + 1) * HEAD_DIM)
        m = jnp.max(s, axis=-1, keepdims=True)
        p = jnp.exp2(s - m).astype(BF16)
        o_aug = jnp.dot(p, va_ref[hh, 0:kk, :], preferred_element_type=F32)
        o12 = o_aug[:, 0:HEAD_DIM] * (1.0 / o_aug[:, HEAD_DIM:2 * HEAD_DIM])
        o = o12[0:t] - lam * o12[t:2 * t]
        o = _rms(o, onw_ref[...]) * (1.0 - lambda_init)
        o_ref[rows, cols] = (o * _silu(gate_ref[rows, cols].astype(F32))).astype(o_ref.dtype)

    heads = range(DIFF_HEADS_PER_STEP)
    order = list(range(nq - 1, -1, -1))
    for hh in heads:
        prepare(hh)
    s_next = [scores(hh, order[0]) for hh in heads]
    for pos, i in enumerate(order):
        for hh in heads:
            s = s_next[hh]
            if pos + 1 < nq:
                s_next[hh] = scores(hh, order[pos + 1])
            finish(hh, i, s)


def _diff(proj, slopes, q_norm2, k_norm2, lq1, lk1, lq2, lk2, out_norm, lambda_init):
    bsz, seq, _ = proj.shape
    t = DIFF_TQ
    hps = DIFF_HEADS_PER_STEP
    width = hps * HEAD_DIM
    kern = functools.partial(_diff_kernel, seq=seq, lambda_init=lambda_init)
    vec = lambda n: pl.BlockSpec((1, n), lambda b, h: (0, 0))
    blk = lambda off: pl.BlockSpec((None, seq, width), lambda b, h: (b, 0, off // width + h))
    return pl.pallas_call(
        kern,
        grid=(bsz, DIFF_HEADS // hps),
        in_specs=[
            pl.BlockSpec(memory_space=pltpu.SMEM),
            blk(OFF_DQ), blk(OFF_DK), blk(OFF_DV), blk(OFF_GATE_DIFF),
            vec(HEAD_DIM), vec(HEAD_DIM),
            vec(DIFF_DQK), vec(DIFF_DQK), vec(DIFF_DQK), vec(DIFF_DQK),
            vec(HEAD_DIM),
        ],
        out_specs=pl.BlockSpec((None, seq, width), lambda b, h: (b, 0, h)),
        out_shape=jax.ShapeDtypeStruct((bsz, seq, BRANCH), BF16),
        scratch_shapes=[
            pltpu.VMEM((hps, seq // t, 2 * t, 2 * HEAD_DIM), BF16),
            pltpu.VMEM((hps, 2 * HEAD_DIM, seq), BF16),
            pltpu.VMEM((hps, seq, 2 * HEAD_DIM), BF16),
        ],
        compiler_params=_cparams(("parallel", "parallel")),
        name="diffattn",
    )(slopes, proj, proj, proj, proj, q_norm2, k_norm2, lq1, lk1, lq2, lk2, out_norm)


LRU_TILE = 512


def _lru_kernel(x_ref, gate_ref, cw_ref, cb_ref, wg_ref, br_ref, bi_ref, lam_ref, o_ref,
                a_scr, h_scr, *, seq):
    width = LRU_TILE
    x = x_ref[...].astype(F32)
    row = lax.broadcasted_iota(jnp.int32, (seq, width), 0)
    xc = x * cw_ref[CONV_WIDTH - 1:CONV_WIDTH, :] + cb_ref[...]
    for s in range(1, CONV_WIDTH):
        tap = cw_ref[CONV_WIDTH - 1 - s:CONV_WIDTH - s, :]
        xc = xc + jnp.where(row >= s, pltpu.roll(x, s, 0), 0.0) * tap
    lam = lam_ref[...]
    softplus_neg = jnp.maximum(-lam, 0.0) + jnp.log1p(jnp.exp(-jnp.abs(lam)))
    sub = lax.broadcasted_iota(jnp.int32, (seq, HEAD_DIM), 0) & (SUBLANES - 1)
    for n in range(width // HEAD_DIM):
        cols = slice(n * HEAD_DIM, (n + 1) * HEAD_DIM)
        xn = xc[:, cols]
        ri = jnp.dot(xn.astype(BF16), wg_ref[n], preferred_element_type=F32)
        r = jax.nn.sigmoid(ri[:, 0:HEAD_DIM] + br_ref[:, cols])
        gi = jax.nn.sigmoid(ri[:, HEAD_DIM:2 * HEAD_DIM] + bi_ref[:, cols])
        log_a = (-LRU_C) * r * softplus_neg[:, cols]
        a = jnp.exp(log_a)
        u = jnp.sqrt(1.0 - a * a) * (gi * xn)
        shift = 1
        while shift < SUBLANES:
            keep = sub >= shift
            u = jnp.where(keep, a * pltpu.roll(u, shift, 0) + u, u)
            a = jnp.where(keep, a * pltpu.roll(a, shift, 0), a)
            shift *= 2
        a_scr[:, cols] = a
        h_scr[:, cols] = u

    def body(tile, carry):
        rows = pl.ds(pl.multiple_of(tile * SUBLANES, SUBLANES), SUBLANES)
        h = a_scr[rows, :] * carry + h_scr[rows, :]
        h_scr[rows, :] = h
        return jnp.broadcast_to(h[SUBLANES - 1:SUBLANES, :], (SUBLANES, width))

    lax.fori_loop(0, seq // SUBLANES, body, jnp.zeros((SUBLANES, width), F32), unroll=8)
    o_ref[...] = (h_scr[...] * _silu(gate_ref[...].astype(F32))).astype(o_ref.dtype)


def _lru(proj, conv_w, conv_b, w_gate, b_r, b_i, lam):
    bsz, seq, _ = proj.shape
    t = LRU_TILE
    nblk = t // HEAD_DIM
    kern = functools.partial(_lru_kernel, seq=seq)
    vec = pl.BlockSpec((1, t), lambda b, c: (0, c))
    return pl.pallas_call(
        kern,
        grid=(bsz, BRANCH // t),
        in_specs=[
            pl.BlockSpec((None, seq, t), lambda b, c: (b, 0, OFF_RX // t + c)),
            pl.BlockSpec((None, seq, t), lambda b, c: (b, 0, OFF_GATE_LRU // t + c)),
            pl.BlockSpec((CONV_WIDTH, t), lambda b, c: (0, c)),
            vec,
            pl.BlockSpec((nblk, HEAD_DIM, 2 * HEAD_DIM), lambda b, c: (c, 0, 0)),
            vec, vec, vec,
        ],
        out_specs=pl.BlockSpec((None, seq, t), lambda b, c: (b, 0, c)),
        out_shape=jax.ShapeDtypeStruct((bsz, seq, BRANCH), BF16),
        scratch_shapes=[pltpu.VMEM((seq, t), F32), pltpu.VMEM((seq, t), F32)],
        compiler_params=_cparams(("parallel", "parallel")),
        name="rglru",
    )(proj, proj, conv_w, conv_b, w_gate, b_r, b_i, lam)


def _outproj_kernel(y0, y1, y2, y3, w_ref, x_ref, o_ref, wb_ref):
    @pl.when(pl.program_id(1) == 0)
    def _():
        wb_ref[...] = w_ref[...].astype(BF16)

    acc = x_ref[...]
    for k, y in enumerate((y0, y1, y2, y3)):
        acc = acc + jnp.dot(y[...], wb_ref[k * BRANCH:(k + 1) * BRANCH, :],
                            preferred_element_type=F32)
    o_ref[...] = acc


def _outproj(ys, w_out, x2, layer, tm=1024, tn=512):
    m, d = x2.shape
    return pl.pallas_call(
        _outproj_kernel,
        grid=(d // tn, m // tm),
        in_specs=[pl.BlockSpec((tm, BRANCH), lambda j, i: (i, 0)) for _ in range(4)] + [
            pl.BlockSpec((None, d, tn), lambda j, i: (layer, 0, j)),
            pl.BlockSpec((tm, tn), lambda j, i: (i, j)),
        ],
        out_specs=pl.BlockSpec((tm, tn), lambda j, i: (i, j)),
        out_shape=jax.ShapeDtypeStruct((m, d), F32),
        scratch_shapes=[pltpu.VMEM((d, tn), BF16)],
        compiler_params=_cparams(("parallel", "arbitrary")),
        name="outproj",
    )(*ys, w_out, x2)


def _alibi_slopes(n):
    return 2.0 ** (-8.0 * jnp.arange(1, n + 1, dtype=F32) / n)


def _layer(x2, bsz, seq, layer, w_in_t, w_out, p):
    h = _norm(x2, p['norm_w'].reshape(1, -1))
    proj2, a2 = _inproj(h, w_in_t, layer)
    proj = proj2.reshape(bsz, seq, N_MAIN)
    a_lr = a2.reshape(bsz, seq, LANES)

    w_up = jnp.pad(p['gla_w_up'], ((0, LANES - GLA_RANK), (0, 0)))
    y_gla = _gla(proj, a_lr, w_up, p['gla_b_up'].reshape(1, -1), p['gla_norm_w'].reshape(1, -1))

    y_swa = _swa(proj, p['swa_q_norm'].reshape(1, -1), p['swa_k_norm'].reshape(1, -1),
                 _alibi_slopes(SWA_GROUP * SWA_KV_HEADS), p['swa_sinks'])

    lambda_init = 0.8 - 0.6 * math.exp(-0.3 * layer)
    tile2 = lambda v: jnp.concatenate([v, v]).reshape(1, -1)
    y_diff = _diff(proj, _alibi_slopes(DIFF_HEADS), tile2(p['diff_q_norm']), tile2(p['diff_k_norm']),
                   p['diff_lq1'].reshape(1, -1), p['diff_lk1'].reshape(1, -1),
                   p['diff_lq2'].reshape(1, -1), p['diff_lk2'].reshape(1, -1),
                   p['diff_out_norm'].reshape(1, -1), lambda_init)

    w_gate = jnp.concatenate([p['lru_w_r'], p['lru_w_i']], axis=-1).astype(BF16)
    y_lru = _lru(proj, p['lru_conv_w'], p['lru_conv_b'].reshape(1, -1), w_gate,
                 p['lru_b_r'].reshape(1, -1), p['lru_b_i'].reshape(1, -1),
                 p['lru_lambda'].reshape(1, -1))

    ys = [y.reshape(bsz * seq, BRANCH) for y in (y_gla, y_swa, y_diff, y_lru)]
    return _outproj(ys, w_out, x2, layer)


def kernel(x, norm_w, w_in, w_out, gla_w_up, gla_b_up, gla_norm_w, swa_q_norm, swa_k_norm, swa_sinks, diff_q_norm, diff_k_norm, diff_lq1, diff_lk1, diff_lq2, diff_lk2, diff_out_norm, lru_conv_w, lru_conv_b, lru_w_r, lru_b_r, lru_w_i, lru_b_i, lru_lambda):
    params = dict(norm_w=norm_w, gla_w_up=gla_w_up, gla_b_up=gla_b_up,
                  gla_norm_w=gla_norm_w, swa_q_norm=swa_q_norm, swa_k_norm=swa_k_norm,
                  swa_sinks=swa_sinks, diff_q_norm=diff_q_norm, diff_k_norm=diff_k_norm,
                  diff_lq1=diff_lq1, diff_lk1=diff_lk1, diff_lq2=diff_lq2, diff_lk2=diff_lk2,
                  diff_out_norm=diff_out_norm, lru_conv_w=lru_conv_w, lru_conv_b=lru_conv_b,
                  lru_w_r=lru_w_r, lru_b_r=lru_b_r, lru_w_i=lru_w_i, lru_b_i=lru_b_i,
                  lru_lambda=lru_lambda)
    bsz, seq, d = x.shape
    w_in_t = jnp.swapaxes(w_in, 1, 2)
    x2 = x.reshape(bsz * seq, d)
    for layer in range(norm_w.shape[0]):
        x2 = _layer(x2, bsz, seq, layer, w_in_t, w_out,
                    {k: v[layer] for k, v in params.items()})
    return x2.reshape(bsz, seq, d)
```

```python
import functools
import math

import jax
import jax.numpy as jnp
from jax import lax
from jax.experimental import pallas as pl
from jax.experimental.pallas import tpu as pltpu

F32 = jnp.float32
BF16 = jnp.bfloat16

D_MODEL = 4096
HEAD_DIM = 128
BRANCH = D_MODEL // 4
NORM_EPS = 1e-6
GLA_DK = 64
GLA_RANK = 16
GLA_TAU = 16.0
GLA_CHUNK = 64
SWA_WINDOW = 128
SWA_GROUP = 4
SWA_KV_HEADS = 2
DIFF_DQK = 64
DIFF_HEADS = 8
LRU_C = 8.0
CONV_WIDTH = 4
LANES = 128
SUBLANES = 8
LOG2E = math.log2(math.e)

OFF_GQ, OFF_GK, OFF_GV = 0, 512, 1024
OFF_SQ = 2048
OFF_DQ, OFF_DK, OFF_DV = 3072, 4096, 5120
OFF_RX = 6144
OFF_GATE = 7168
OFF_SK, OFF_SV = 11264, 11520
N_MAIN = 11776
OFF_GATE_GLA = OFF_GATE
OFF_GATE_SWA = OFF_GATE + BRANCH
OFF_GATE_DIFF = OFF_GATE + 2 * BRANCH
OFF_GATE_LRU = OFF_GATE + 3 * BRANCH
PREP_TN = 512
ORIG_A_ROW = 2048
N_PLAIN_TILES = 2048 // PREP_TN
N_SQ_END_TILE = 3072 // PREP_TN
LAST_TILE = N_MAIN // PREP_TN - 1

VMEM_LIMIT = 56 * 1024 * 1024


def _cparams(sem):
    return pltpu.CompilerParams(dimension_semantics=sem, vmem_limit_bytes=VMEM_LIMIT)


def _rms(x, w):
    ms = jnp.mean(x * x, axis=-1, keepdims=True)
    return x * lax.rsqrt(ms + NORM_EPS) * w


def _silu(g):
    return g * jax.nn.sigmoid(g)


def _norm_kernel(x_ref, w_ref, o_ref):
    o_ref[...] = _rms(x_ref[...], w_ref[...]).astype(o_ref.dtype)


def _norm(x2, w, tm=512):
    m, d = x2.shape
    return pl.pallas_call(
        _norm_kernel,
        grid=(m // tm,),
        in_specs=[pl.BlockSpec((tm, d), lambda i: (i, 0)),
                  pl.BlockSpec((1, d), lambda i: (0, 0))],
        out_specs=pl.BlockSpec((tm, d), lambda i: (i, 0)),
        out_shape=jax.ShapeDtypeStruct((m, d), BF16),
        compiler_params=_cparams(("parallel",)),
        name="rmsnorm",
    )(x2, w)


def _nt_dot(a, b):
    return lax.dot_general(a, b, (((1,), (1,)), ((), ())), preferred_element_type=F32)


WOUT_CAST_ROWS = 32
WOUT_CAST_STEPS = 16


def _inproj_kernel(h_ref, w_ref, wa_ref, wo_ref, o_ref, a_ref, wob_ref):
    j = pl.program_id(1)
    h = h_ref[...]
    o_ref[...] = _nt_dot(h, w_ref[0].astype(BF16)).astype(o_ref.dtype)

    @pl.when(j == 0)
    def _():
        pad = jnp.zeros((LANES - GLA_RANK, wa_ref.shape[1]), BF16)
        a_ref[...] = _nt_dot(h, jnp.concatenate([wa_ref[...].astype(BF16), pad], axis=0))

    @pl.when(j < WOUT_CAST_STEPS)
    def _():
        wob_ref[...] = wo_ref[...].astype(BF16)


def _src_row(j):
    tile = jnp.where(j < N_SQ_END_TILE, j, jnp.where(j == LAST_TILE, N_SQ_END_TILE, j + 1))
    units = tile * (PREP_TN // GLA_RANK) + jnp.where(j < N_PLAIN_TILES, 0, 1)
    return pl.multiple_of(units * GLA_RANK, GLA_RANK)


def _inproj(h, w_in_t, w_out, layer, tm=1024):
    m, d = h.shape
    tn = PREP_TN
    k_out, n_out = w_out.shape[1:]
    assert (m // tm) * WOUT_CAST_STEPS * WOUT_CAST_ROWS == k_out
    assert N_MAIN // tn >= WOUT_CAST_STEPS
    slab = lambda i, j: i * WOUT_CAST_STEPS + jnp.minimum(j, WOUT_CAST_STEPS - 1)
    return pl.pallas_call(
        _inproj_kernel,
        grid=(m // tm, N_MAIN // tn),
        in_specs=[pl.BlockSpec((tm, d), lambda i, j: (i, 0)),
                  pl.BlockSpec((pl.Element(1), pl.Element(tn), pl.Element(d)),
                               lambda i, j: (layer, _src_row(j), 0)),
                  pl.BlockSpec((None, GLA_RANK, d), lambda i, j: (layer, ORIG_A_ROW // GLA_RANK, 0)),
                  pl.BlockSpec((None, WOUT_CAST_ROWS, n_out), lambda i, j: (layer, slab(i, j), 0))],
        out_specs=[pl.BlockSpec((tm, tn), lambda i, j: (i, j)),
                   pl.BlockSpec((tm, LANES), lambda i, j: (i, 0)),
                   pl.BlockSpec((WOUT_CAST_ROWS, n_out), lambda i, j: (slab(i, j), 0))],
        out_shape=[jax.ShapeDtypeStruct((m, N_MAIN), BF16),
                   jax.ShapeDtypeStruct((m, LANES), F32),
                   jax.ShapeDtypeStruct((k_out, n_out), BF16)],
        compiler_params=_cparams(("parallel", "arbitrary")),
        name="inproj",
    )(h, w_in_t, w_in_t, w_out)


def _gla_kernel(q_ref, k_ref, v_ref, gate_ref, a_ref, wup_ref, bup_ref, nw_ref, o_ref, *, seq):
    c = GLA_CHUNK
    nc = seq // c
    dv2 = 2 * HEAD_DIM
    logit = jnp.dot(a_ref[...], wup_ref[...], preferred_element_type=F32) + bup_ref[...]
    g = (jnp.minimum(logit, 0.0) - jnp.log(1.0 + jnp.exp(-jnp.abs(logit)))) * (LOG2E / GLA_TAU)
    row = lax.broadcasted_iota(jnp.int32, (seq, LANES), 0) & (c - 1)
    b = g
    shift = 1
    while shift < SUBLANES:
        b = b + jnp.where(row >= shift, pltpu.roll(b, shift, 0), 0.0)
        shift *= 2
    b4 = b.reshape(nc, c // SUBLANES, SUBLANES, LANES)
    skip = 1
    while skip < c // SUBLANES:
        b4 = jnp.concatenate([b4[:, :skip], b4[:, skip:] + b4[:, :-skip]], axis=1)
        skip *= 2
    b3 = b4.reshape(nc, c, LANES)
    b_last = b3[:, c - 1:c, :]
    q3 = q_ref[...].astype(F32).reshape(nc, c, LANES)
    k3 = k_ref[...].astype(F32).reshape(nc, c, LANES)
    v3 = v_ref[...].reshape(nc, c, dv2)
    q_dec = q3 * (GLA_DK ** -0.5) * jnp.exp2(b3)
    k_dec = (k3 * jnp.exp2(-b3)).astype(BF16)
    k_state = (k3 * jnp.exp2(b_last - b3)).astype(BF16)
    decay = jnp.exp2(b_last)
    lane = lax.broadcasted_iota(jnp.int32, (1, 1, LANES), 2)
    q_st = jnp.concatenate([jnp.where(lane < GLA_DK, q_dec, 0.0),
                            jnp.where(lane >= GLA_DK, q_dec, 0.0)], axis=1).astype(BF16)
    att = jnp.einsum('nid,njd->nij', q_st, k_dec, preferred_element_type=F32)
    ii = lax.broadcasted_iota(jnp.int32, (1, 2 * c, c), 1) & (c - 1)
    jj = lax.broadcasted_iota(jnp.int32, (1, 2 * c, c), 2)
    att = jnp.where(ii >= jj, att, 0.0).astype(BF16)
    heads = ((slice(0, c), slice(0, HEAD_DIM)), (slice(c, 2 * c), slice(HEAD_DIM, dv2)))
    o_heads = [jnp.einsum('nij,njv->niv', att[:, rs], v3[:, :, vs], preferred_element_type=F32)
               for rs, vs in heads]
    v3_t = jnp.swapaxes(v3.astype(F32), 1, 2).astype(BF16)
    u_t = jnp.einsum('nvj,njd->nvd', v3_t, k_state, preferred_element_type=F32)
    st = jnp.zeros((dv2, LANES), F32)
    starts = []
    for n in range(nc):
        starts.append(st.astype(BF16))
        st = st * decay[n] + u_t[n]
    s_start = jnp.stack(starts, axis=0)
    nw = nw_ref[...]
    for (rs, vs), o_intra in zip(heads, o_heads):
        o = o_intra + jnp.einsum('nid,nvd->niv', q_st[:, rs], s_start[:, vs],
                                 preferred_element_type=F32)
        o = _rms(o, nw).reshape(seq, HEAD_DIM)
        o_ref[:, vs] = (o * _silu(gate_ref[:, vs].astype(F32))).astype(o_ref.dtype)


def _gla(proj, a_lr, w_up, b_up, norm_w):
    bsz, seq, _ = proj.shape
    pair = 2 * HEAD_DIM
    kern = functools.partial(_gla_kernel, seq=seq)
    return pl.pallas_call(
        kern,
        grid=(bsz, BRANCH // pair),
        in_specs=[
            pl.BlockSpec((None, seq, LANES), lambda b, p: (b, 0, OFF_GQ // LANES + p)),
            pl.BlockSpec((None, seq, LANES), lambda b, p: (b, 0, OFF_GK // LANES + p)),
            pl.BlockSpec((None, seq, pair), lambda b, p: (b, 0, OFF_GV // pair + p)),
            pl.BlockSpec((None, seq, pair), lambda b, p: (b, 0, OFF_GATE_GLA // pair + p)),
            pl.BlockSpec((None, seq, LANES), lambda b, p: (b, 0, 0)),
            pl.BlockSpec((LANES, LANES), lambda b, p: (0, p)),
            pl.BlockSpec((1, LANES), lambda b, p: (0, p)),
            pl.BlockSpec((1, HEAD_DIM), lambda b, p: (0, 0)),
        ],
        out_specs=pl.BlockSpec((None, seq, pair), lambda b, p: (b, 0, p)),
        out_shape=jax.ShapeDtypeStruct((bsz, seq, BRANCH), BF16),
        compiler_params=_cparams(("parallel", "parallel")),
        name="gla",
    )(proj, proj, proj, proj, a_lr, w_up, b_up, norm_w)


def _swa_kernel(slope_ref, sink_ref, q_ref, k_ref, v_ref, gate_ref, qn_ref, kn_ref, o_ref,
                qs_ref, ks_ref, *, seq):
    w = SWA_WINDOW
    nb = seq // w
    g_n = SWA_GROUP
    kvh = pl.program_id(1)
    ks_ref[...] = _rms(k_ref[...].astype(F32), kn_ref[...]).astype(BF16)
    for g in range(g_n):
        qg = q_ref[:, g * HEAD_DIM:(g + 1) * HEAD_DIM].astype(F32)
        qs_ref[:, g * HEAD_DIM:(g + 1) * HEAD_DIM] = (
            _rms(qg, qn_ref[...]) * (HEAD_DIM ** -0.5)).astype(BF16)
    qi = lax.broadcasted_iota(jnp.int32, (w, 2 * w), 0)
    kj = lax.broadcasted_iota(jnp.int32, (w, 2 * w), 1)
    dist = qi + w - kj
    valid = (dist >= 0) & (dist < w)
    distf = dist.astype(F32)
    biases, sinks = [], []
    for g in range(g_n):
        slope = slope_ref[kvh * g_n + g]
        biases.append(jnp.where(valid, -slope * distf, -jnp.inf))
        sinks.append(sink_ref[kvh * g_n + g])

    def block(q_rows, k_win, v_win, bias_cols):
        q_st = jnp.concatenate(
            [qs_ref[q_rows, g * HEAD_DIM:(g + 1) * HEAD_DIM] for g in range(g_n)], axis=0)
        s = _nt_dot(q_st, k_win)
        ps, dens = [], []
        for g in range(g_n):
            sg = s[g * w:(g + 1) * w] + biases[g][:, bias_cols]
            m = jnp.maximum(jnp.max(sg, axis=-1, keepdims=True), sinks[g])
            p = jnp.exp(sg - m)
            dens.append(jnp.sum(p, axis=-1, keepdims=True) + jnp.exp(sinks[g] - m))
            ps.append(p.astype(BF16))
        o = jnp.dot(jnp.concatenate(ps, axis=0), v_win, preferred_element_type=F32)
        for g in range(g_n):
            cols = slice(g * HEAD_DIM, (g + 1) * HEAD_DIM)
            og = o[g * w:(g + 1) * w] / dens[g]
            o_ref[q_rows, cols] = (og * _silu(gate_ref[q_rows, cols].astype(F32))).astype(o_ref.dtype)

    block(pl.ds(0, w), ks_ref[0:w, :], v_ref[0:w, :], slice(w, 2 * w))

    def body(n, carry):
        q_rows = pl.ds(pl.multiple_of(n * w, w), w)
        win = pl.ds(pl.multiple_of((n - 1) * w, w), 2 * w)
        block(q_rows, ks_ref[win, :], v_ref[win, :], slice(0, 2 * w))
        return carry

    lax.fori_loop(1, nb, body, 0, unroll=3)


def _swa(proj, q_norm, k_norm, slopes, sinks):
    bsz, seq, _ = proj.shape
    qw = SWA_GROUP * HEAD_DIM
    kern = functools.partial(_swa_kernel, seq=seq)
    smem = pl.BlockSpec(memory_space=pltpu.SMEM)
    return pl.pallas_call(
        kern,
        grid=(bsz, SWA_KV_HEADS),
        in_specs=[
            smem, smem,
            pl.BlockSpec((None, seq, qw), lambda b, h: (b, 0, OFF_SQ // qw + h)),
            pl.BlockSpec((None, seq, HEAD_DIM), lambda b, h: (b, 0, OFF_SK // HEAD_DIM + h)),
            pl.BlockSpec((None, seq, HEAD_DIM), lambda b, h: (b, 0, OFF_SV // HEAD_DIM + h)),
            pl.BlockSpec((None, seq, qw), lambda b, h: (b, 0, OFF_GATE_SWA // qw + h)),
            pl.BlockSpec((1, HEAD_DIM), lambda b, h: (0, 0)),
            pl.BlockSpec((1, HEAD_DIM), lambda b, h: (0, 0)),
        ],
        out_specs=pl.BlockSpec((None, seq, qw), lambda b, h: (b, 0, h)),
        out_shape=jax.ShapeDtypeStruct((bsz, seq, BRANCH), BF16),
        scratch_shapes=[pltpu.VMEM((seq, qw), BF16), pltpu.VMEM((seq, HEAD_DIM), BF16)],
        compiler_params=_cparams(("parallel", "parallel")),
        name="swa",
    )(slopes, sinks, proj, proj, proj, proj, q_norm, k_norm)


DIFF_TQ = 256
DIFF_HEADS_PER_STEP = 2


def _group_sum(x, group):
    gi = lax.broadcasted_iota(jnp.int32, (LANES, LANES), 0) // group
    gj = lax.broadcasted_iota(jnp.int32, (LANES, LANES), 1) // group
    ones = jnp.where(gi == gj, 1.0, 0.0).astype(BF16)
    hi = x.astype(BF16)
    lo = (x - hi.astype(F32)).astype(BF16)
    return (jnp.dot(hi, ones, preferred_element_type=F32)
            + jnp.dot(lo, ones, preferred_element_type=F32))


def _half_rms(x, w2):
    ms = _group_sum(x * x, DIFF_DQK) * (1.0 / DIFF_DQK)
    return x * lax.rsqrt(ms + NORM_EPS) * w2


def _diff_kernel(slope_ref, q_ref, k_ref, v_ref, gate_ref, qn_ref, kn_ref, lq1_ref, lk1_ref,
                 lq2_ref, lk2_ref, onw_ref, o_ref, qs_ref, kt_ref, va_ref, *, seq, lambda_init):
    t = DIFF_TQ
    nq = seq // t
    n_terms = 3
    lam = (jnp.exp(jnp.sum(lq1_ref[...] * lk1_ref[...], axis=-1, keepdims=True))
           - jnp.exp(jnp.sum(lq2_ref[...] * lk2_ref[...], axis=-1, keepdims=True)) + lambda_init)
    lane = lax.broadcasted_iota(jnp.int32, (1, LANES), 1)
    sub = lax.broadcasted_iota(jnp.int32, (HEAD_DIM, 1), 0)
    key_pos = lax.broadcasted_iota(jnp.int32, (1, seq), 1).astype(F32)
    ones = jnp.broadcast_to(jnp.where(lane < n_terms, 1.0, 0.0).astype(BF16), (2 * t, HEAD_DIM))
    one_cols = jnp.ones((seq, HEAD_DIM), BF16)
    r = lax.broadcasted_iota(jnp.int32, (2 * t, t), 0) & (t - 1)
    c = lax.broadcasted_iota(jnp.int32, (2 * t, t), 1)
    causal = r >= c

    def prepare(hh):
        cols = slice(hh * HEAD_DIM, (hh + 1) * HEAD_DIM)
        slope = slope_ref[pl.program_id(1) * DIFF_HEADS_PER_STEP + hh]
        kt_ref[hh, 0:HEAD_DIM, :] = _half_rms(k_ref[:, cols].astype(F32), kn_ref[...]).T.astype(BF16)
        rest = key_pos * (slope * LOG2E)
        bias_rows = jnp.zeros((HEAD_DIM, seq), F32)
        for term in range(n_terms):
            piece = rest.astype(BF16).astype(F32)
            bias_rows = jnp.where(sub == term, piece, bias_rows)
            rest = rest - piece
        kt_ref[hh, HEAD_DIM:2 * HEAD_DIM, :] = bias_rows.astype(BF16)
        va_ref[hh, :, 0:HEAD_DIM] = v_ref[:, cols]
        va_ref[hh, :, HEAD_DIM:2 * HEAD_DIM] = one_cols
        qn = _half_rms(q_ref[:, cols].astype(F32), qn_ref[...]) * (DIFF_DQK ** -0.5 * LOG2E)
        q_lo = jnp.where(lane < DIFF_DQK, qn, 0.0).astype(BF16)
        q_hi = jnp.where(lane < DIFF_DQK, 0.0, qn).astype(BF16)
        for i in range(nq):
            qs_ref[hh, i, 0:t, 0:HEAD_DIM] = q_lo[i * t:(i + 1) * t]
            qs_ref[hh, i, t:2 * t, 0:HEAD_DIM] = q_hi[i * t:(i + 1) * t]
            qs_ref[hh, i, :, HEAD_DIM:2 * HEAD_DIM] = ones

    def scores(hh, i):
        kk = (i + 1) * t
        s = jnp.dot(qs_ref[hh, i], kt_ref[hh, :, 0:kk], preferred_element_type=F32)
        s_diag = jnp.where(causal, s[:, kk - t:kk], -jnp.inf)
        return s_diag if i == 0 else jnp.concatenate([s[:, 0:kk - t], s_diag], axis=1)

    def finish(hh, i, s):
        kk = (i + 1) * t
        rows = slice(i * t, kk)
        cols = slice(hh * HEAD_DIM, (hh + 1) * HEAD_DIM)
        m = jnp.max(s, axis=-1, keepdims=True)
        p = jnp.exp2(s - m).astype(BF16)
        o_aug = jnp.dot(p, va_ref[hh, 0:kk, :], preferred_element_type=F32)
        o12 = o_aug[:, 0:HEAD_DIM] * (1.0 / o_aug[:, HEAD_DIM:2 * HEAD_DIM])
        o = o12[0:t] - lam * o12[t:2 * t]
        o = _rms(o, onw_ref[...]) * (1.0 - lambda_init)
        o_ref[rows, cols] = (o * _silu(gate_ref[rows, cols].astype(F32))).astype(o_ref.dtype)

    heads = range(DIFF_HEADS_PER_STEP)
    order = list(range(nq - 1, -1, -1))
    for hh in heads:
        prepare(hh)
    s_next = [scores(hh, order[0]) for hh in heads]
    for pos, i in enumerate(order):
        for hh in heads:
            s = s_next[hh]
            if pos + 1 < nq:
                s_next[hh] = scores(hh, order[pos + 1])
            finish(hh, i, s)


def _diff(proj, slopes, q_norm2, k_norm2, lq1, lk1, lq2, lk2, out_norm, lambda_init):
    bsz, seq, _ = proj.shape
    t = DIFF_TQ
    hps = DIFF_HEADS_PER_STEP
    width = hps * HEAD_DIM
    kern = functools.partial(_diff_kernel, seq=seq, lambda_init=lambda_init)
    vec = lambda n: pl.BlockSpec((1, n), lambda b, h: (0, 0))
    blk = lambda off: pl.BlockSpec((None, seq, width), lambda b, h: (b, 0, off // width + h))
    return pl.pallas_call(
        kern,
        grid=(bsz, DIFF_HEADS // hps),
        in_specs=[
            pl.BlockSpec(memory_space=pltpu.SMEM),
            blk(OFF_DQ), blk(OFF_DK), blk(OFF_DV), blk(OFF_GATE_DIFF),
            vec(HEAD_DIM), vec(HEAD_DIM),
            vec(DIFF_DQK), vec(DIFF_DQK), vec(DIFF_DQK), vec(DIFF_DQK),
            vec(HEAD_DIM),
        ],
        out_specs=pl.BlockSpec((None, seq, width), lambda b, h: (b, 0, h)),
        out_shape=jax.ShapeDtypeStruct((bsz, seq, BRANCH), BF16),
        scratch_shapes=[
            pltpu.VMEM((hps, seq // t, 2 * t, 2 * HEAD_DIM), BF16),
            pltpu.VMEM((hps, 2 * HEAD_DIM, seq), BF16),
            pltpu.VMEM((hps, seq, 2 * HEAD_DIM), BF16),
        ],
        compiler_params=_cparams(("parallel", "parallel")),
        name="diffattn",
    )(slopes, proj, proj, proj, proj, q_norm2, k_norm2, lq1, lk1, lq2, lk2, out_norm)


LRU_TILE = 512


def _lru_kernel(x_ref, gate_ref, cw_ref, cb_ref, wg_ref, br_ref, bi_ref, lam_ref, o_ref,
                a_scr, h_scr, *, seq):
    width = LRU_TILE
    x = x_ref[...].astype(F32)
    row = lax.broadcasted_iota(jnp.int32, (seq, width), 0)
    xc = x * cw_ref[CONV_WIDTH - 1:CONV_WIDTH, :] + cb_ref[...]
    for s in range(1, CONV_WIDTH):
        tap = cw_ref[CONV_WIDTH - 1 - s:CONV_WIDTH - s, :]
        xc = xc + jnp.where(row >= s, pltpu.roll(x, s, 0), 0.0) * tap
    lam = lam_ref[...]
    softplus_neg = jnp.maximum(-lam, 0.0) + jnp.log1p(jnp.exp(-jnp.abs(lam)))
    sub = lax.broadcasted_iota(jnp.int32, (seq, HEAD_DIM), 0) & (SUBLANES - 1)
    for n in range(width // HEAD_DIM):
        cols = slice(n * HEAD_DIM, (n + 1) * HEAD_DIM)
        xn = xc[:, cols]
        ri = jnp.dot(xn.astype(BF16), wg_ref[n], preferred_element_type=F32)
        r = jax.nn.sigmoid(ri[:, 0:HEAD_DIM] + br_ref[:, cols])
        gi = jax.nn.sigmoid(ri[:, HEAD_DIM:2 * HEAD_DIM] + bi_ref[:, cols])
        log_a = (-LRU_C) * r * softplus_neg[:, cols]
        a = jnp.exp(log_a)
        u = jnp.sqrt(1.0 - a * a) * (gi * xn)
        shift = 1
        while shift < SUBLANES:
            keep = sub >= shift
            u = jnp.where(keep, a * pltpu.roll(u, shift, 0) + u, u)
            a = jnp.where(keep, a * pltpu.roll(a, shift, 0), a)
            shift *= 2
        a_scr[:, cols] = a
        h_scr[:, cols] = u

    def body(tile, carry):
        rows = pl.ds(pl.multiple_of(tile * SUBLANES, SUBLANES), SUBLANES)
        h = a_scr[rows, :] * carry + h_scr[rows, :]
        h_scr[rows, :] = h
        return jnp.broadcast_to(h[SUBLANES - 1:SUBLANES, :], (SUBLANES, width))

    lax.fori_loop(0, seq // SUBLANES, body, jnp.zeros((SUBLANES, width), F32), unroll=8)
    o_ref[...] = (h_scr[...] * _silu(gate_ref[...].astype(F32))).astype(o_ref.dtype)


def _lru(proj, conv_w, conv_b, w_gate, b_r, b_i, lam):
    bsz, seq, _ = proj.shape
    t = LRU_TILE
    nblk = t // HEAD_DIM
    kern = functools.partial(_lru_kernel, seq=seq)
    vec = pl.BlockSpec((1, t), lambda b, c: (0, c))
    return pl.pallas_call(
        kern,
        grid=(bsz, BRANCH // t),
        in_specs=[
            pl.BlockSpec((None, seq, t), lambda b, c: (b, 0, OFF_RX // t + c)),
            pl.BlockSpec((None, seq, t), lambda b, c: (b, 0, OFF_GATE_LRU // t + c)),
            pl.BlockSpec((CONV_WIDTH, t), lambda b, c: (0, c)),
            vec,
            pl.BlockSpec((nblk, HEAD_DIM, 2 * HEAD_DIM), lambda b, c: (c, 0, 0)),
            vec, vec, vec,
        ],
        out_specs=pl.BlockSpec((None, seq, t), lambda b, c: (b, 0, c)),
        out_shape=jax.ShapeDtypeStruct((bsz, seq, BRANCH), BF16),
        scratch_shapes=[pltpu.VMEM((seq, t), F32), pltpu.VMEM((seq, t), F32)],
        compiler_params=_cparams(("parallel", "parallel")),
        name="rglru",
    )(proj, proj, conv_w, conv_b, w_gate, b_r, b_i, lam)


def _outproj_kernel(y0, y1, y2, y3, w_ref, x_ref, o_ref):
    acc = x_ref[...]
    for k, y in enumerate((y0, y1, y2, y3)):
        acc = acc + jnp.dot(y[...], w_ref[k * BRANCH:(k + 1) * BRANCH, :],
                            preferred_element_type=F32)
    o_ref[...] = acc


def _outproj(ys, w_out_b, x2, tm=1024, tn=1024):
    m, d = x2.shape
    return pl.pallas_call(
        _outproj_kernel,
        grid=(m // tm, d // tn),
        in_specs=[pl.BlockSpec((tm, BRANCH), lambda i, j: (i, 0)) for _ in range(4)] + [
            pl.BlockSpec((d, tn), lambda i, j: (0, j)),
            pl.BlockSpec((tm, tn), lambda i, j: (i, j)),
        ],
        out_specs=pl.BlockSpec((tm, tn), lambda i, j: (i, j)),
        out_shape=jax.ShapeDtypeStruct((m, d), F32),
        compiler_params=_cparams(("parallel", "arbitrary")),
        name="outproj",
    )(*ys, w_out_b, x2)


def _alibi_slopes(n):
    return 2.0 ** (-8.0 * jnp.arange(1, n + 1, dtype=F32) / n)


def _layer(x2, bsz, seq, layer, w_in_t, w_out, p):
    h = _norm(x2, p['norm_w'].reshape(1, -1))
    proj2, a2, w_out_b = _inproj(h, w_in_t, w_out, layer)
    proj = proj2.reshape(bsz, seq, N_MAIN)
    a_lr = a2.reshape(bsz, seq, LANES)

    w_up = jnp.pad(p['gla_w_up'], ((0, LANES - GLA_RANK), (0, 0)))
    y_gla = _gla(proj, a_lr, w_up, p['gla_b_up'].reshape(1, -1), p['gla_norm_w'].reshape(1, -1))

    y_swa = _swa(proj, p['swa_q_norm'].reshape(1, -1), p['swa_k_norm'].reshape(1, -1),
                 _alibi_slopes(SWA_GROUP * SWA_KV_HEADS), p['swa_sinks'])

    lambda_init = 0.8 - 0.6 * math.exp(-0.3 * layer)
    tile2 = lambda v: jnp.concatenate([v, v]).reshape(1, -1)
    y_diff = _diff(proj, _alibi_slopes(DIFF_HEADS), tile2(p['diff_q_norm']), tile2(p['diff_k_norm']),
                   p['diff_lq1'].reshape(1, -1), p['diff_lk1'].reshape(1, -1),
                   p['diff_lq2'].reshape(1, -1), p['diff_lk2'].reshape(1, -1),
                   p['diff_out_norm'].reshape(1, -1), lambda_init)

    w_gate = jnp.concatenate([p['lru_w_r'], p['lru_w_i']], axis=-1).astype(BF16)
    y_lru = _lru(proj, p['lru_conv_w'], p['lru_conv_b'].reshape(1, -1), w_gate,
                 p['lru_b_r'].reshape(1, -1), p['lru_b_i'].reshape(1, -1),
                 p['lru_lambda'].reshape(1, -1))

    ys = [y.reshape(bsz * seq, BRANCH) for y in (y_gla, y_swa, y_diff, y_lru)]
    return _outproj(ys, w_out_b, x2)


def kernel(x, norm_w, w_in, w_out, gla_w_up, gla_b_up, gla_norm_w, swa_q_norm, swa_k_norm, swa_sinks, diff_q_norm, diff_k_norm, diff_lq1, diff_lk1, diff_lq2, diff_lk2, diff_out_norm, lru_conv_w, lru_conv_b, lru_w_r, lru_b_r, lru_w_i, lru_b_i, lru_lambda):
    params = dict(norm_w=norm_w, gla_w_up=gla_w_up, gla_b_up=gla_b_up,
                  gla_norm_w=gla_norm_w, swa_q_norm=swa_q_norm, swa_k_norm=swa_k_norm,
                  swa_sinks=swa_sinks, diff_q_norm=diff_q_norm, diff_k_norm=diff_k_norm,
                  diff_lq1=diff_lq1, diff_lk1=diff_lk1, diff_lq2=diff_lq2, diff_lk2=diff_lk2,
                  diff_out_norm=diff_out_norm, lru_conv_w=lru_conv_w, lru_conv_b=lru_conv_b,
                  lru_w_r=lru_w_r, lru_b_r=lru_b_r, lru_w_i=lru_w_i, lru_b_i=lru_b_i,
                  lru_lambda=lru_lambda)
    bsz, seq, d = x.shape
    w_in_t = jnp.swapaxes(w_in, 1, 2)
    x2 = x.reshape(bsz * seq, d)
    for layer in range(norm_w.shape[0]):
        x2 = _layer(x2, bsz, seq, layer, w_in_t, w_out,
                    {k: v[layer] for k, v in params.items()})
    return x2.reshape(bsz, seq, d)
```

```python
import functools
import math

import jax
import jax.numpy as jnp
from jax import lax
from jax.experimental import pallas as pl
from jax.experimental.pallas import tpu as pltpu

F32 = jnp.float32
BF16 = jnp.bfloat16

D_MODEL = 4096
HEAD_DIM = 128
BRANCH = D_MODEL // 4
NORM_EPS = 1e-6
GLA_DK = 64
GLA_RANK = 16
GLA_TAU = 16.0
GLA_CHUNK = 64
SWA_WINDOW = 128
SWA_GROUP = 4
SWA_KV_HEADS = 2
DIFF_DQK = 64
DIFF_HEADS = 8
LRU_C = 8.0
CONV_WIDTH = 4
LANES = 128
SUBLANES = 8
LOG2E = math.log2(math.e)

OFF_GQ, OFF_GK, OFF_GV = 0, 512, 1024
OFF_SQ = 2048
OFF_DQ, OFF_DK, OFF_DV = 3072, 4096, 5120
OFF_RX = 6144
OFF_GATE = 7168
OFF_SK, OFF_SV = 11264, 11520
N_MAIN = 11776
OFF_GATE_GLA = OFF_GATE
OFF_GATE_SWA = OFF_GATE + BRANCH
OFF_GATE_DIFF = OFF_GATE + 2 * BRANCH
OFF_GATE_LRU = OFF_GATE + 3 * BRANCH
PREP_TN = 512
ORIG_A_ROW = 2048
N_PLAIN_TILES = 2048 // PREP_TN
N_SQ_END_TILE = 3072 // PREP_TN
LAST_TILE = N_MAIN // PREP_TN - 1

VMEM_LIMIT = 60 * 1024 * 1024


def _cparams(sem):
    return pltpu.CompilerParams(dimension_semantics=sem, vmem_limit_bytes=VMEM_LIMIT)


def _rms(x, w):
    ms = jnp.mean(x * x, axis=-1, keepdims=True)
    return x * lax.rsqrt(ms + NORM_EPS) * w


def _silu(g):
    return g * jax.nn.sigmoid(g)


def _lane_fold(sq):
    acc = sq[:, 0:LANES]
    for g in range(1, sq.shape[1] // LANES):
        acc = acc + sq[:, g * LANES:(g + 1) * LANES]
    return acc


def _row_scale(folded, d):
    ms = jnp.sum(folded, axis=-1, keepdims=True) * (1.0 / d)
    return jnp.broadcast_to(lax.rsqrt(ms + NORM_EPS), folded.shape)


def _norm_kernel(x_ref, w_ref, o_ref, rs_ref):
    x = x_ref[...]
    o_ref[...] = (x * w_ref[...]).astype(o_ref.dtype)
    rs_ref[...] = _row_scale(_lane_fold(x * x), x.shape[1])


def _norm(x2, w, tm=512):
    m, d = x2.shape
    return pl.pallas_call(
        _norm_kernel,
        grid=(m // tm,),
        in_specs=[pl.BlockSpec((tm, d), lambda i: (i, 0)),
                  pl.BlockSpec((1, d), lambda i: (0, 0))],
        out_specs=[pl.BlockSpec((tm, d), lambda i: (i, 0)),
                   pl.BlockSpec((tm, LANES), lambda i: (i, 0))],
        out_shape=[jax.ShapeDtypeStruct((m, d), BF16),
                   jax.ShapeDtypeStruct((m, LANES), F32)],
        compiler_params=_cparams(("parallel",)),
        name="rmsnorm",
    )(x2, w)


def _nt_dot(a, b):
    return lax.dot_general(a, b, (((1,), (1,)), ((), ())), preferred_element_type=F32)


WOUT_CAST_ROWS = 32
WOUT_CAST_STEPS = 16


def _inproj_kernel(h_ref, rs_ref, w_ref, wa_ref, wo_ref, o_ref, a_ref, wob_ref):
    j = pl.program_id(1)
    h = h_ref[...]
    rs = rs_ref[...]
    acc = _nt_dot(h, w_ref[0].astype(BF16))
    o_ref[...] = (acc * jnp.concatenate([rs] * (acc.shape[1] // LANES), axis=1)).astype(o_ref.dtype)

    @pl.when(j == 0)
    def _():
        pad = jnp.zeros((LANES - GLA_RANK, wa_ref.shape[1]), BF16)
        a_ref[...] = _nt_dot(h, jnp.concatenate([wa_ref[...].astype(BF16), pad], axis=0)) * rs

    @pl.when(j < WOUT_CAST_STEPS)
    def _():
        wob_ref[...] = wo_ref[...].astype(BF16)


def _src_row(j):
    tile = jnp.where(j < N_SQ_END_TILE, j, jnp.where(j == LAST_TILE, N_SQ_END_TILE, j + 1))
    units = tile * (PREP_TN // GLA_RANK) + jnp.where(j < N_PLAIN_TILES, 0, 1)
    return pl.multiple_of(units * GLA_RANK, GLA_RANK)


def _inproj(h, rs, w_in_t, w_out, layer, tm=1024):
    m, d = h.shape
    tn = PREP_TN
    k_out, n_out = w_out.shape[1:]
    assert (m // tm) * WOUT_CAST_STEPS * WOUT_CAST_ROWS == k_out
    assert N_MAIN // tn >= WOUT_CAST_STEPS
    slab = lambda i, j: i * WOUT_CAST_STEPS + jnp.minimum(j, WOUT_CAST_STEPS - 1)
    return pl.pallas_call(
        _inproj_kernel,
        grid=(m // tm, N_MAIN // tn),
        in_specs=[pl.BlockSpec((tm, d), lambda i, j: (i, 0)),
                  pl.BlockSpec((tm, LANES), lambda i, j: (i, 0)),
                  pl.BlockSpec((pl.Element(1), pl.Element(tn), pl.Element(d)),
                               lambda i, j: (layer, _src_row(j), 0)),
                  pl.BlockSpec((None, GLA_RANK, d), lambda i, j: (layer, ORIG_A_ROW // GLA_RANK, 0)),
                  pl.BlockSpec((None, WOUT_CAST_ROWS, n_out), lambda i, j: (layer, slab(i, j), 0))],
        out_specs=[pl.BlockSpec((tm, tn), lambda i, j: (i, j)),
                   pl.BlockSpec((tm, LANES), lambda i, j: (i, 0)),
                   pl.BlockSpec((WOUT_CAST_ROWS, n_out), lambda i, j: (slab(i, j), 0))],
        out_shape=[jax.ShapeDtypeStruct((m, N_MAIN), BF16),
                   jax.ShapeDtypeStruct((m, LANES), F32),
                   jax.ShapeDtypeStruct((k_out, n_out), BF16)],
        compiler_params=_cparams(("parallel", "arbitrary")),
        name="inproj",
    )(h, rs, w_in_t, w_in_t, w_out)


def _gla_kernel(q_ref, k_ref, v_ref, gate_ref, a_ref, wup_ref, bup_ref, nw_ref, o_ref, *, seq):
    c = GLA_CHUNK
    nc = seq // c
    dv2 = 2 * HEAD_DIM
    logit = jnp.dot(a_ref[...], wup_ref[...], preferred_element_type=F32) + bup_ref[...]
    g = (jnp.minimum(logit, 0.0) - jnp.log(1.0 + jnp.exp(-jnp.abs(logit)))) * (LOG2E / GLA_TAU)
    row = lax.broadcasted_iota(jnp.int32, (seq, LANES), 0) & (c - 1)
    b = g
    shift = 1
    while shift < SUBLANES:
        b = b + jnp.where(row >= shift, pltpu.roll(b, shift, 0), 0.0)
        shift *= 2
    b4 = b.reshape(nc, c // SUBLANES, SUBLANES, LANES)
    skip = 1
    while skip < c // SUBLANES:
        b4 = jnp.concatenate([b4[:, :skip], b4[:, skip:] + b4[:, :-skip]], axis=1)
        skip *= 2
    b3 = b4.reshape(nc, c, LANES)
    b_last = b3[:, c - 1:c, :]
    q3 = q_ref[...].astype(F32).reshape(nc, c, LANES)
    k3 = k_ref[...].astype(F32).reshape(nc, c, LANES)
    v3 = v_ref[...].reshape(nc, c, dv2)
    q_dec = q3 * (GLA_DK ** -0.5) * jnp.exp2(b3)
    k_dec = (k3 * jnp.exp2(-b3)).astype(BF16)
    k_state = (k3 * jnp.exp2(b_last - b3)).astype(BF16)
    decay = jnp.exp2(b_last)
    lane = lax.broadcasted_iota(jnp.int32, (1, 1, LANES), 2)
    q_st = jnp.concatenate([jnp.where(lane < GLA_DK, q_dec, 0.0),
                            jnp.where(lane >= GLA_DK, q_dec, 0.0)], axis=1).astype(BF16)
    att = jnp.einsum('nid,njd->nij', q_st, k_dec, preferred_element_type=F32)
    ii = lax.broadcasted_iota(jnp.int32, (1, 2 * c, c), 1) & (c - 1)
    jj = lax.broadcasted_iota(jnp.int32, (1, 2 * c, c), 2)
    att = jnp.where(ii >= jj, att, 0.0).astype(BF16)
    heads = ((slice(0, c), slice(0, HEAD_DIM)), (slice(c, 2 * c), slice(HEAD_DIM, dv2)))
    o_heads = [jnp.einsum('nij,njv->niv', att[:, rs], v3[:, :, vs], preferred_element_type=F32)
               for rs, vs in heads]
    v3_t = jnp.swapaxes(v3.astype(F32), 1, 2).astype(BF16)
    u_t = jnp.einsum('nvj,njd->nvd', v3_t, k_state, preferred_element_type=F32)
    st = jnp.zeros((dv2, LANES), F32)
    starts = []
    for n in range(nc):
        starts.append(st.astype(BF16))
        st = st * decay[n] + u_t[n]
    s_start = jnp.stack(starts, axis=0)
    nw = nw_ref[...]
    for (rs, vs), o_intra in zip(heads, o_heads):
        o = o_intra + jnp.einsum('nid,nvd->niv', q_st[:, rs], s_start[:, vs],
                                 preferred_element_type=F32)
        o = _rms(o, nw).reshape(seq, HEAD_DIM)
        o_ref[:, vs] = (o * _silu(gate_ref[:, vs].astype(F32))).astype(o_ref.dtype)


def _gla(proj, a_lr, w_up, b_up, norm_w):
    bsz, seq, _ = proj.shape
    pair = 2 * HEAD_DIM
    kern = functools.partial(_gla_kernel, seq=seq)
    return pl.pallas_call(
        kern,
        grid=(bsz, BRANCH // pair),
        in_specs=[
            pl.BlockSpec((None, seq, LANES), lambda b, p: (b, 0, OFF_GQ // LANES + p)),
            pl.BlockSpec((None, seq, LANES), lambda b, p: (b, 0, OFF_GK // LANES + p)),
            pl.BlockSpec((None, seq, pair), lambda b, p: (b, 0, OFF_GV // pair + p)),
            pl.BlockSpec((None, seq, pair), lambda b, p: (b, 0, OFF_GATE_GLA // pair + p)),
            pl.BlockSpec((None, seq, LANES), lambda b, p: (b, 0, 0)),
            pl.BlockSpec((LANES, LANES), lambda b, p: (0, p)),
            pl.BlockSpec((1, LANES), lambda b, p: (0, p)),
            pl.BlockSpec((1, HEAD_DIM), lambda b, p: (0, 0)),
        ],
        out_specs=pl.BlockSpec((None, seq, pair), lambda b, p: (b, 0, p)),
        out_shape=jax.ShapeDtypeStruct((bsz, seq, BRANCH), BF16),
        compiler_params=_cparams(("parallel", "parallel")),
        name="gla",
    )(proj, proj, proj, proj, a_lr, w_up, b_up, norm_w)


def _swa_kernel(slope_ref, sink_ref, q_ref, k_ref, v_ref, gate_ref, qn_ref, kn_ref, o_ref,
                qs_ref, ks_ref, *, seq):
    w = SWA_WINDOW
    nb = seq // w
    g_n = SWA_GROUP
    kvh = pl.program_id(1)
    ks_ref[...] = _rms(k_ref[...].astype(F32), kn_ref[...]).astype(BF16)
    for g in range(g_n):
        qg = q_ref[:, g * HEAD_DIM:(g + 1) * HEAD_DIM].astype(F32)
        qs_ref[:, g * HEAD_DIM:(g + 1) * HEAD_DIM] = (
            _rms(qg, qn_ref[...]) * (HEAD_DIM ** -0.5)).astype(BF16)
    qi = lax.broadcasted_iota(jnp.int32, (w, 2 * w), 0)
    kj = lax.broadcasted_iota(jnp.int32, (w, 2 * w), 1)
    dist = qi + w - kj
    valid = (dist >= 0) & (dist < w)
    distf = dist.astype(F32)
    biases, sinks = [], []
    for g in range(g_n):
        slope = slope_ref[kvh * g_n + g]
        biases.append(jnp.where(valid, -slope * distf, -jnp.inf))
        sinks.append(sink_ref[kvh * g_n + g])

    def block(q_rows, k_win, v_win, bias_cols):
        q_st = jnp.concatenate(
            [qs_ref[q_rows, g * HEAD_DIM:(g + 1) * HEAD_DIM] for g in range(g_n)], axis=0)
        s = _nt_dot(q_st, k_win)
        ps, dens = [], []
        for g in range(g_n):
            sg = s[g * w:(g + 1) * w] + biases[g][:, bias_cols]
            m = jnp.maximum(jnp.max(sg, axis=-1, keepdims=True), sinks[g])
            p = jnp.exp(sg - m)
            dens.append(jnp.sum(p, axis=-1, keepdims=True) + jnp.exp(sinks[g] - m))
            ps.append(p.astype(BF16))
        o = jnp.dot(jnp.concatenate(ps, axis=0), v_win, preferred_element_type=F32)
        for g in range(g_n):
            cols = slice(g * HEAD_DIM, (g + 1) * HEAD_DIM)
            og = o[g * w:(g + 1) * w] / dens[g]
            o_ref[q_rows, cols] = (og * _silu(gate_ref[q_rows, cols].astype(F32))).astype(o_ref.dtype)

    block(pl.ds(0, w), ks_ref[0:w, :], v_ref[0:w, :], slice(w, 2 * w))

    def body(n, carry):
        q_rows = pl.ds(pl.multiple_of(n * w, w), w)
        win = pl.ds(pl.multiple_of((n - 1) * w, w), 2 * w)
        block(q_rows, ks_ref[win, :], v_ref[win, :], slice(0, 2 * w))
        return carry

    lax.fori_loop(1, nb, body, 0, unroll=3)


def _swa(proj, q_norm, k_norm, slopes, sinks):
    bsz, seq, _ = proj.shape
    qw = SWA_GROUP * HEAD_DIM
    kern = functools.partial(_swa_kernel, seq=seq)
    smem = pl.BlockSpec(memory_space=pltpu.SMEM)
    return pl.pallas_call(
        kern,
        grid=(bsz, SWA_KV_HEADS),
        in_specs=[
            smem, smem,
            pl.BlockSpec((None, seq, qw), lambda b, h: (b, 0, OFF_SQ // qw + h)),
            pl.BlockSpec((None, seq, HEAD_DIM), lambda b, h: (b, 0, OFF_SK // HEAD_DIM + h)),
            pl.BlockSpec((None, seq, HEAD_DIM), lambda b, h: (b, 0, OFF_SV // HEAD_DIM + h)),
            pl.BlockSpec((None, seq, qw), lambda b, h: (b, 0, OFF_GATE_SWA // qw + h)),
            pl.BlockSpec((1, HEAD_DIM), lambda b, h: (0, 0)),
            pl.BlockSpec((1, HEAD_DIM), lambda b, h: (0, 0)),
        ],
        out_specs=pl.BlockSpec((None, seq, qw), lambda b, h: (b, 0, h)),
        out_shape=jax.ShapeDtypeStruct((bsz, seq, BRANCH), BF16),
        scratch_shapes=[pltpu.VMEM((seq, qw), BF16), pltpu.VMEM((seq, HEAD_DIM), BF16)],
        compiler_params=_cparams(("parallel", "parallel")),
        name="swa",
    )(slopes, sinks, proj, proj, proj, proj, q_norm, k_norm)


DIFF_TQ = 256
DIFF_HEADS_PER_STEP = 2


def _group_sum(x, group):
    gi = lax.broadcasted_iota(jnp.int32, (LANES, LANES), 0) // group
    gj = lax.broadcasted_iota(jnp.int32, (LANES, LANES), 1) // group
    ones = jnp.where(gi == gj, 1.0, 0.0).astype(BF16)
    hi = x.astype(BF16)
    lo = (x - hi.astype(F32)).astype(BF16)
    return (jnp.dot(hi, ones, preferred_element_type=F32)
            + jnp.dot(lo, ones, preferred_element_type=F32))


def _half_rms(x, w2):
    ms = _group_sum(x * x, DIFF_DQK) * (1.0 / DIFF_DQK)
    return x * lax.rsqrt(ms + NORM_EPS) * w2


def _diff_kernel(slope_ref, q_ref, k_ref, v_ref, gate_ref, qn_ref, kn_ref, lq1_ref, lk1_ref,
                 lq2_ref, lk2_ref, onw_ref, o_ref, *scratch, seq, lambda_init):
    qs_refs, kt_refs, va_refs = scratch[0::3], scratch[1::3], scratch[2::3]
    t = DIFF_TQ
    nq = seq // t
    n_terms = 3
    lam = (jnp.exp(jnp.sum(lq1_ref[...] * lk1_ref[...], axis=-1, keepdims=True))
           - jnp.exp(jnp.sum(lq2_ref[...] * lk2_ref[...], axis=-1, keepdims=True)) + lambda_init)
    lane = lax.broadcasted_iota(jnp.int32, (1, LANES), 1)
    sub = lax.broadcasted_iota(jnp.int32, (HEAD_DIM, 1), 0)
    key_pos = lax.broadcasted_iota(jnp.int32, (1, seq), 1).astype(F32)
    ones = jnp.broadcast_to(jnp.where(lane < n_terms, 1.0, 0.0).astype(BF16), (2 * t, HEAD_DIM))
    one_cols = jnp.ones((seq, HEAD_DIM), BF16)
    r = lax.broadcasted_iota(jnp.int32, (2 * t, t), 0) & (t - 1)
    c = lax.broadcasted_iota(jnp.int32, (2 * t, t), 1)
    causal = r >= c

    def prepare(hh):
        cols = slice(hh * HEAD_DIM, (hh + 1) * HEAD_DIM)
        slope = slope_ref[pl.program_id(1) * DIFF_HEADS_PER_STEP + hh]
        kt_refs[hh][0:HEAD_DIM, :] = _half_rms(k_ref[:, cols].astype(F32), kn_ref[...]).T.astype(BF16)
        rest = key_pos * (slope * LOG2E)
        bias_rows = jnp.zeros((HEAD_DIM, seq), F32)
        for term in range(n_terms):
            piece = rest.astype(BF16).astype(F32)
            bias_rows = jnp.where(sub == term, piece, bias_rows)
            rest = rest - piece
        kt_refs[hh][HEAD_DIM:2 * HEAD_DIM, :] = bias_rows.astype(BF16)
        va_refs[hh][:, 0:HEAD_DIM] = v_ref[:, cols]
        va_refs[hh][:, HEAD_DIM:2 * HEAD_DIM] = one_cols
        qn = _half_rms(q_ref[:, cols].astype(F32), qn_ref[...]) * (DIFF_DQK ** -0.5 * LOG2E)
        q_lo = jnp.where(lane < DIFF_DQK, qn, 0.0).astype(BF16)
        q_hi = jnp.where(lane < DIFF_DQK, 0.0, qn).astype(BF16)
        for i in range(nq):
            qs_refs[hh][i, 0:t, 0:HEAD_DIM] = q_lo[i * t:(i + 1) * t]
            qs_refs[hh][i, t:2 * t, 0:HEAD_DIM] = q_hi[i * t:(i + 1) * t]
            qs_refs[hh][i, :, HEAD_DIM:2 * HEAD_DIM] = ones

    def scores(hh, i):
        kk = (i + 1) * t
        s = jnp.dot(qs_refs[hh][i], kt_refs[hh][:, 0:kk], preferred_element_type=F32)
        s_diag = jnp.where(causal, s[:, kk - t:kk], -jnp.inf)
        return s_diag if i == 0 else jnp.concatenate([s[:, 0:kk - t], s_diag], axis=1)

    def finish(hh, i, s):
        kk = (i + 1) * t
        rows = slice(i * t, kk)
        cols = slice(hh * HEAD_DIM, (hh + 1) * HEAD_DIM)
        m = jnp.max(s, axis=-1, keepdims=True)
        p = jnp.exp2(s - m).astype(BF16)
        o_aug = jnp.dot(p, va_refs[hh][0:kk, :], preferred_element_type=F32)
        o12 = o_aug[:, 0:HEAD_DIM] * (1.0 / o_aug[:, HEAD_DIM:2 * HEAD_DIM])
        o = o12[0:t] - lam * o12[t:2 * t]
        o = _rms(o, onw_ref[...]) * (1.0 - lambda_init)
        o_ref[rows, cols] = (o * _silu(gate_ref[rows, cols].astype(F32))).astype(o_ref.dtype)

    heads = range(DIFF_HEADS_PER_STEP)
    order = list(range(nq - 1, -1, -1))
    for hh in heads:
        prepare(hh)
    s_next = [scores(hh, order[0]) for hh in heads]
    for pos, i in enumerate(order):
        for hh in heads:
            s = s_next[hh]
            if pos + 1 < nq:
                s_next[hh] = scores(hh, order[pos + 1])
            finish(hh, i, s)


def _diff(proj, slopes, q_norm2, k_norm2, lq1, lk1, lq2, lk2, out_norm, lambda_init):
    bsz, seq, _ = proj.shape
    t = DIFF_TQ
    hps = DIFF_HEADS_PER_STEP
    width = hps * HEAD_DIM
    kern = functools.partial(_diff_kernel, seq=seq, lambda_init=lambda_init)
    vec = lambda n: pl.BlockSpec((1, n), lambda b, h: (0, 0))
    blk = lambda off: pl.BlockSpec((None, seq, width), lambda b, h: (b, 0, off // width + h))
    return pl.pallas_call(
        kern,
        grid=(bsz, DIFF_HEADS // hps),
        in_specs=[
            pl.BlockSpec(memory_space=pltpu.SMEM),
            blk(OFF_DQ), blk(OFF_DK), blk(OFF_DV), blk(OFF_GATE_DIFF),
            vec(HEAD_DIM), vec(HEAD_DIM),
            vec(DIFF_DQK), vec(DIFF_DQK), vec(DIFF_DQK), vec(DIFF_DQK),
            vec(HEAD_DIM),
        ],
        out_specs=pl.BlockSpec((None, seq, width), lambda b, h: (b, 0, h)),
        out_shape=jax.ShapeDtypeStruct((bsz, seq, BRANCH), BF16),
        scratch_shapes=[
            pltpu.VMEM((seq // t, 2 * t, 2 * HEAD_DIM), BF16),
            pltpu.VMEM((2 * HEAD_DIM, seq), BF16),
            pltpu.VMEM((seq, 2 * HEAD_DIM), BF16),
        ] * hps,
        compiler_params=_cparams(("parallel", "parallel")),
        name="diffattn",
    )(slopes, proj, proj, proj, proj, q_norm2, k_norm2, lq1, lk1, lq2, lk2, out_norm)


LRU_TILE = 512


def _lru_kernel(x_ref, gate_ref, cw_ref, cb_ref, wg_ref, br_ref, bi_ref, lam_ref, o_ref,
                a_scr, h_scr, *, seq):
    width = LRU_TILE
    x = x_ref[...].astype(F32)
    row = lax.broadcasted_iota(jnp.int32, (seq, width), 0)
    xc = x * cw_ref[CONV_WIDTH - 1:CONV_WIDTH, :] + cb_ref[...]
    for s in range(1, CONV_WIDTH):
        tap = cw_ref[CONV_WIDTH - 1 - s:CONV_WIDTH - s, :]
        xc = xc + jnp.where(row >= s, pltpu.roll(x, s, 0), 0.0) * tap
    lam = lam_ref[...]
    softplus_neg = jnp.maximum(-lam, 0.0) + jnp.log1p(jnp.exp(-jnp.abs(lam)))
    sub = lax.broadcasted_iota(jnp.int32, (seq, HEAD_DIM), 0) & (SUBLANES - 1)
    for n in range(width // HEAD_DIM):
        cols = slice(n * HEAD_DIM, (n + 1) * HEAD_DIM)
        xn = xc[:, cols]
        ri = jnp.dot(xn.astype(BF16), wg_ref[n], preferred_element_type=F32)
        r = jax.nn.sigmoid(ri[:, 0:HEAD_DIM] + br_ref[:, cols])
        gi = jax.nn.sigmoid(ri[:, HEAD_DIM:2 * HEAD_DIM] + bi_ref[:, cols])
        log_a = (-LRU_C) * r * softplus_neg[:, cols]
        a = jnp.exp(log_a)
        u = jnp.sqrt(1.0 - a * a) * (gi * xn)
        shift = 1
        while shift < SUBLANES:
            keep = sub >= shift
            u = jnp.where(keep, a * pltpu.roll(u, shift, 0) + u, u)
            a = jnp.where(keep, a * pltpu.roll(a, shift, 0), a)
            shift *= 2
        a_scr[:, cols] = a
        h_scr[:, cols] = u

    def body(tile, carry):
        rows = pl.ds(pl.multiple_of(tile * SUBLANES, SUBLANES), SUBLANES)
        h = a_scr[rows, :] * carry + h_scr[rows, :]
        h_scr[rows, :] = h
        return jnp.broadcast_to(h[SUBLANES - 1:SUBLANES, :], (SUBLANES, width))

    lax.fori_loop(0, seq // SUBLANES, body, jnp.zeros((SUBLANES, width), F32), unroll=8)
    o_ref[...] = (h_scr[...] * _silu(gate_ref[...].astype(F32))).astype(o_ref.dtype)


def _lru(proj, conv_w, conv_b, w_gate, b_r, b_i, lam):
    bsz, seq, _ = proj.shape
    t = LRU_TILE
    nblk = t // HEAD_DIM
    kern = functools.partial(_lru_kernel, seq=seq)
    vec = pl.BlockSpec((1, t), lambda b, c: (0, c))
    return pl.pallas_call(
        kern,
        grid=(bsz, BRANCH // t),
        in_specs=[
            pl.BlockSpec((None, seq, t), lambda b, c: (b, 0, OFF_RX // t + c)),
            pl.BlockSpec((None, seq, t), lambda b, c: (b, 0, OFF_GATE_LRU // t + c)),
            pl.BlockSpec((CONV_WIDTH, t), lambda b, c: (0, c)),
            vec,
            pl.BlockSpec((nblk, HEAD_DIM, 2 * HEAD_DIM), lambda b, c: (c, 0, 0)),
            vec, vec, vec,
        ],
        out_specs=pl.BlockSpec((None, seq, t), lambda b, c: (b, 0, c)),
        out_shape=jax.ShapeDtypeStruct((bsz, seq, BRANCH), BF16),
        scratch_shapes=[pltpu.VMEM((seq, t), F32), pltpu.VMEM((seq, t), F32)],
        compiler_params=_cparams(("parallel", "parallel")),
        name="rglru",
    )(proj, proj, conv_w, conv_b, w_gate, b_r, b_i, lam)


def _outproj_kernel(y0, y1, y2, y3, w_ref, x_ref, *rest, d):
    acc = x_ref[...]
    for k, y in enumerate((y0, y1, y2, y3)):
        acc = acc + jnp.dot(y[...], w_ref[k * BRANCH:(k + 1) * BRANCH, :],
                            preferred_element_type=F32)
    if len(rest) == 1:
        (o_ref,) = rest
        o_ref[...] = acc
        return
    nw_ref, o_ref, h_ref, rs_ref = rest
    j = pl.program_id(1)
    o_ref[...] = acc
    h_ref[...] = (acc * nw_ref[...]).astype(h_ref.dtype)
    folded = _lane_fold(acc * acc)

    @pl.when(j == 0)
    def _():
        rs_ref[...] = folded

    @pl.when(j > 0)
    def _():
        rs_ref[...] += folded

    @pl.when(j == pl.num_programs(1) - 1)
    def _():
        rs_ref[...] = _row_scale(rs_ref[...], d)


def _outproj(ys, w_out_b, x2, next_norm_w=None, tm=1024, tn=1024):
    m, d = x2.shape
    in_specs = [pl.BlockSpec((tm, BRANCH), lambda i, j: (i, 0)) for _ in range(4)] + [
        pl.BlockSpec((d, tn), lambda i, j: (0, j)),
        pl.BlockSpec((tm, tn), lambda i, j: (i, j)),
    ]
    out_specs = [pl.BlockSpec((tm, tn), lambda i, j: (i, j))]
    out_shape = [jax.ShapeDtypeStruct((m, d), F32)]
    args = [*ys, w_out_b, x2]
    if next_norm_w is not None:
        in_specs.append(pl.BlockSpec((1, tn), lambda i, j: (0, j)))
        out_specs += [pl.BlockSpec((tm, tn), lambda i, j: (i, j)),
                      pl.BlockSpec((tm, LANES), lambda i, j: (i, 0))]
        out_shape += [jax.ShapeDtypeStruct((m, d), BF16), jax.ShapeDtypeStruct((m, LANES), F32)]
        args.append(next_norm_w)
    return pl.pallas_call(
        functools.partial(_outproj_kernel, d=d),
        grid=(m // tm, d // tn),
        in_specs=in_specs,
        out_specs=out_specs,
        out_shape=out_shape,
        compiler_params=_cparams(("parallel", "arbitrary")),
        name="outproj",
    )(*args)


def _alibi_slopes(n):
    return 2.0 ** (-8.0 * jnp.arange(1, n + 1, dtype=F32) / n)


def _layer(x2, h, rs, bsz, seq, layer, w_in_t, w_out, p, next_norm_w):
    proj2, a2, w_out_b = _inproj(h, rs, w_in_t, w_out, layer)
    proj = proj2.reshape(bsz, seq, N_MAIN)
    a_lr = a2.reshape(bsz, seq, LANES)

    w_up = jnp.pad(p['gla_w_up'], ((0, LANES - GLA_RANK), (0, 0)))
    y_gla = _gla(proj, a_lr, w_up, p['gla_b_up'].reshape(1, -1), p['gla_norm_w'].reshape(1, -1))

    y_swa = _swa(proj, p['swa_q_norm'].reshape(1, -1), p['swa_k_norm'].reshape(1, -1),
                 _alibi_slopes(SWA_GROUP * SWA_KV_HEADS), p['swa_sinks'])

    lambda_init = 0.8 - 0.6 * math.exp(-0.3 * layer)
    tile2 = lambda v: jnp.concatenate([v, v]).reshape(1, -1)
    y_diff = _diff(proj, _alibi_slopes(DIFF_HEADS), tile2(p['diff_q_norm']), tile2(p['diff_k_norm']),
                   p['diff_lq1'].reshape(1, -1), p['diff_lk1'].reshape(1, -1),
                   p['diff_lq2'].reshape(1, -1), p['diff_lk2'].reshape(1, -1),
                   p['diff_out_norm'].reshape(1, -1), lambda_init)

    w_gate = jnp.concatenate([p['lru_w_r'], p['lru_w_i']], axis=-1).astype(BF16)
    y_lru = _lru(proj, p['lru_conv_w'], p['lru_conv_b'].reshape(1, -1), w_gate,
                 p['lru_b_r'].reshape(1, -1), p['lru_b_i'].reshape(1, -1),
                 p['lru_lambda'].reshape(1, -1))

    ys = [y.reshape(bsz * seq, BRANCH) for y in (y_gla, y_swa, y_diff, y_lru)]
    return _outproj(ys, w_out_b, x2, next_norm_w)


def kernel(x, norm_w, w_in, w_out, gla_w_up, gla_b_up, gla_norm_w, swa_q_norm, swa_k_norm, swa_sinks, diff_q_norm, diff_k_norm, diff_lq1, diff_lk1, diff_lq2, diff_lk2, diff_out_norm, lru_conv_w, lru_conv_b, lru_w_r, lru_b_r, lru_w_i, lru_b_i, lru_lambda):
    params = dict(norm_w=norm_w, gla_w_up=gla_w_up, gla_b_up=gla_b_up,
                  gla_norm_w=gla_norm_w, swa_q_norm=swa_q_norm, swa_k_norm=swa_k_norm,
                  swa_sinks=swa_sinks, diff_q_norm=diff_q_norm, diff_k_norm=diff_k_norm,
                  diff_lq1=diff_lq1, diff_lk1=diff_lk1, diff_lq2=diff_lq2, diff_lk2=diff_lk2,
                  diff_out_norm=diff_out_norm, lru_conv_w=lru_conv_w, lru_conv_b=lru_conv_b,
                  lru_w_r=lru_w_r, lru_b_r=lru_b_r, lru_w_i=lru_w_i, lru_b_i=lru_b_i,
                  lru_lambda=lru_lambda)
    bsz, seq, d = x.shape
    w_in_t = jnp.swapaxes(w_in, 1, 2)
    x2 = x.reshape(bsz * seq, d)
    depth = norm_w.shape[0]
    h, rs = _norm(x2, norm_w[0].reshape(1, -1))
    for layer in range(depth):
        p = {k: v[layer] for k, v in params.items()}
        if layer + 1 < depth:
            x2, h, rs = _layer(x2, h, rs, bsz, seq, layer, w_in_t, w_out, p,
                               norm_w[layer + 1].reshape(1, -1))
        else:
            (x2,) = _layer(x2, h, rs, bsz, seq, layer, w_in_t, w_out, p, None)
    return x2.reshape(bsz, seq, d)
```

```python
import functools
import math

import jax
import jax.numpy as jnp
from jax import lax
from jax.experimental import pallas as pl
from jax.experimental.pallas import tpu as pltpu

F32 = jnp.float32
BF16 = jnp.bfloat16

D_MODEL = 4096
HEAD_DIM = 128
BRANCH = D_MODEL // 4
NORM_EPS = 1e-6
GLA_DK = 64
GLA_RANK = 16
GLA_TAU = 16.0
GLA_CHUNK = 64
SWA_WINDOW = 128
SWA_GROUP = 4
SWA_KV_HEADS = 2
DIFF_DQK = 64
DIFF_HEADS = 8
LRU_C = 8.0
CONV_WIDTH = 4
LANES = 128
SUBLANES = 8
LOG2E = math.log2(math.e)

OFF_GQ, OFF_GK, OFF_GV = 0, 512, 1024
OFF_SQ = 2048
OFF_DQ, OFF_DK, OFF_DV = 3072, 4096, 5120
OFF_RX = 6144
OFF_GATE = 7168
OFF_SK, OFF_SV = 11264, 11520
N_MAIN = 11776
OFF_GATE_GLA = OFF_GATE
OFF_GATE_SWA = OFF_GATE + BRANCH
OFF_GATE_DIFF = OFF_GATE + 2 * BRANCH
OFF_GATE_LRU = OFF_GATE + 3 * BRANCH
INPROJ_TN = 512
ORIG_A_ROW = 2048
N_PLAIN_TILES = 2048 // INPROJ_TN
N_SQ_END_TILE = 3072 // INPROJ_TN
LAST_TILE = N_MAIN // INPROJ_TN - 1

V7X_VMEM_BYTES = 64 * 1024 * 1024
VMEM_LIMIT = V7X_VMEM_BYTES - 4 * 1024 * 1024


def _cparams(sem):
    return pltpu.CompilerParams(dimension_semantics=sem, vmem_limit_bytes=VMEM_LIMIT)


def _rms(x, w):
    ms = jnp.mean(x * x, axis=-1, keepdims=True)
    return x * lax.rsqrt(ms + NORM_EPS) * w


def _gated(y, gate):
    g = gate.astype(F32)
    return (y * (g * jax.nn.sigmoid(g))).astype(BF16)


def _lane_fold(sq):
    acc = sq[:, 0:LANES]
    for g in range(1, sq.shape[1] // LANES):
        acc = acc + sq[:, g * LANES:(g + 1) * LANES]
    return acc


def _row_scale(folded, d):
    ms = jnp.sum(folded, axis=-1, keepdims=True) * (1.0 / d)
    return jnp.broadcast_to(lax.rsqrt(ms + NORM_EPS), folded.shape)


def _norm_kernel(x_ref, w_ref, o_ref, rs_ref):
    x = x_ref[...]
    o_ref[...] = (x * w_ref[...]).astype(o_ref.dtype)
    rs_ref[...] = _row_scale(_lane_fold(x * x), x.shape[1])


def _norm(x2, w, tm=512):
    m, d = x2.shape
    return pl.pallas_call(
        _norm_kernel,
        grid=(m // tm,),
        in_specs=[pl.BlockSpec((tm, d), lambda i: (i, 0)),
                  pl.BlockSpec((1, d), lambda i: (0, 0))],
        out_specs=[pl.BlockSpec((tm, d), lambda i: (i, 0)),
                   pl.BlockSpec((tm, LANES), lambda i: (i, 0))],
        out_shape=[jax.ShapeDtypeStruct((m, d), BF16),
                   jax.ShapeDtypeStruct((m, LANES), F32)],
        compiler_params=_cparams(("parallel",)),
        name="rmsnorm",
    )(x2, w)


def _nt_dot(a, b):
    return lax.dot_general(a, b, (((1,), (1,)), ((), ())), preferred_element_type=F32)


WOUT_CAST_ROWS = 32
WOUT_CAST_STEPS = 16


def _inproj_kernel(h_ref, rs_ref, w_ref, wa_ref, wo_ref, o_ref, a_ref, wob_ref):
    j = pl.program_id(1)
    h = h_ref[...]
    rs = rs_ref[...]
    acc = _nt_dot(h, w_ref[0].astype(BF16))
    o_ref[...] = (acc * jnp.concatenate([rs] * (acc.shape[1] // LANES), axis=1)).astype(o_ref.dtype)

    @pl.when(j == 0)
    def _():
        pad = jnp.zeros((LANES - GLA_RANK, wa_ref.shape[1]), BF16)
        a_ref[...] = _nt_dot(h, jnp.concatenate([wa_ref[...].astype(BF16), pad], axis=0)) * rs

    @pl.when(j < WOUT_CAST_STEPS)
    def _():
        wob_ref[...] = wo_ref[...].astype(BF16)


def _src_row(j):
    tile = jnp.where(j < N_SQ_END_TILE, j, jnp.where(j == LAST_TILE, N_SQ_END_TILE, j + 1))
    units = tile * (INPROJ_TN // GLA_RANK) + jnp.where(j < N_PLAIN_TILES, 0, 1)
    return pl.multiple_of(units * GLA_RANK, GLA_RANK)


def _inproj(h, rs, w_in_t, w_out, layer, tm=1024):
    m, d = h.shape
    tn = INPROJ_TN
    k_out, n_out = w_out.shape[1:]
    assert (m // tm) * WOUT_CAST_STEPS * WOUT_CAST_ROWS == k_out
    assert N_MAIN // tn >= WOUT_CAST_STEPS
    slab = lambda i, j: i * WOUT_CAST_STEPS + jnp.minimum(j, WOUT_CAST_STEPS - 1)
    return pl.pallas_call(
        _inproj_kernel,
        grid=(m // tm, N_MAIN // tn),
        in_specs=[pl.BlockSpec((tm, d), lambda i, j: (i, 0)),
                  pl.BlockSpec((tm, LANES), lambda i, j: (i, 0)),
                  pl.BlockSpec((pl.Element(1), pl.Element(tn), pl.Element(d)),
                               lambda i, j: (layer, _src_row(j), 0)),
                  pl.BlockSpec((None, GLA_RANK, d), lambda i, j: (layer, ORIG_A_ROW // GLA_RANK, 0)),
                  pl.BlockSpec((None, WOUT_CAST_ROWS, n_out), lambda i, j: (layer, slab(i, j), 0))],
        out_specs=[pl.BlockSpec((tm, tn), lambda i, j: (i, j)),
                   pl.BlockSpec((tm, LANES), lambda i, j: (i, 0)),
                   pl.BlockSpec((WOUT_CAST_ROWS, n_out), lambda i, j: (slab(i, j), 0))],
        out_shape=[jax.ShapeDtypeStruct((m, N_MAIN), BF16),
                   jax.ShapeDtypeStruct((m, LANES), F32),
                   jax.ShapeDtypeStruct((k_out, n_out), BF16)],
        compiler_params=_cparams(("parallel", "arbitrary")),
        name="inproj",
    )(h, rs, w_in_t, w_in_t, w_out)


def _gla_kernel(q_ref, k_ref, v_ref, gate_ref, a_ref, wup_ref, bup_ref, nw_ref, o_ref, *, seq):
    c = GLA_CHUNK
    nc = seq // c
    dv2 = 2 * HEAD_DIM
    logit = jnp.dot(a_ref[...], wup_ref[...], preferred_element_type=F32) + bup_ref[...]
    g = (jnp.minimum(logit, 0.0) - jnp.log(1.0 + jnp.exp(-jnp.abs(logit)))) * (LOG2E / GLA_TAU)
    row = lax.broadcasted_iota(jnp.int32, (seq, LANES), 0) & (c - 1)
    b = g
    shift = 1
    while shift < SUBLANES:
        b = b + jnp.where(row >= shift, pltpu.roll(b, shift, 0), 0.0)
        shift *= 2
    b4 = b.reshape(nc, c // SUBLANES, SUBLANES, LANES)
    skip = 1
    while skip < c // SUBLANES:
        b4 = jnp.concatenate([b4[:, :skip], b4[:, skip:] + b4[:, :-skip]], axis=1)
        skip *= 2
    b3 = b4.reshape(nc, c, LANES)
    b_last = b3[:, c - 1:c, :]
    q3 = q_ref[...].astype(F32).reshape(nc, c, LANES)
    k3 = k_ref[...].astype(F32).reshape(nc, c, LANES)
    v3 = v_ref[...].reshape(nc, c, dv2)
    q_dec = q3 * (GLA_DK ** -0.5) * jnp.exp2(b3)
    k_dec = (k3 * jnp.exp2(-b3)).astype(BF16)
    k_state = (k3 * jnp.exp2(b_last - b3)).astype(BF16)
    decay = jnp.exp2(b_last)
    lane = lax.broadcasted_iota(jnp.int32, (1, 1, LANES), 2)
    q_st = jnp.concatenate([jnp.where(lane < GLA_DK, q_dec, 0.0),
                            jnp.where(lane >= GLA_DK, q_dec, 0.0)], axis=1).astype(BF16)
    att = jnp.einsum('nid,njd->nij', q_st, k_dec, preferred_element_type=F32)
    ii = lax.broadcasted_iota(jnp.int32, (1, 2 * c, c), 1) & (c - 1)
    jj = lax.broadcasted_iota(jnp.int32, (1, 2 * c, c), 2)
    att = jnp.where(ii >= jj, att, 0.0).astype(BF16)
    heads = ((slice(0, c), slice(0, HEAD_DIM)), (slice(c, 2 * c), slice(HEAD_DIM, dv2)))
    o_heads = [jnp.einsum('nij,njv->niv', att[:, rs], v3[:, :, vs], preferred_element_type=F32)
               for rs, vs in heads]
    v3_t = jnp.swapaxes(v3.astype(F32), 1, 2).astype(BF16)
    u_t = jnp.einsum('nvj,njd->nvd', v3_t, k_state, preferred_element_type=F32)
    st = jnp.zeros((dv2, LANES), F32)
    starts = []
    for n in range(nc):
        starts.append(st.astype(BF16))
        st = st * decay[n] + u_t[n]
    s_start = jnp.stack(starts, axis=0)
    nw = nw_ref[...]
    for (rs, vs), o_intra in zip(heads, o_heads):
        o = o_intra + jnp.einsum('nid,nvd->niv', q_st[:, rs], s_start[:, vs],
                                 preferred_element_type=F32)
        o = _rms(o, nw).reshape(seq, HEAD_DIM)
        o_ref[:, vs] = _gated(o, gate_ref[:, vs])


def _gla(proj, a_lr, w_up, b_up, norm_w):
    bsz, seq, _ = proj.shape
    pair = 2 * HEAD_DIM
    kern = functools.partial(_gla_kernel, seq=seq)
    return pl.pallas_call(
        kern,
        grid=(bsz, BRANCH // pair),
        in_specs=[
            pl.BlockSpec((None, seq, LANES), lambda b, p: (b, 0, OFF_GQ // LANES + p)),
            pl.BlockSpec((None, seq, LANES), lambda b, p: (b, 0, OFF_GK // LANES + p)),
            pl.BlockSpec((None, seq, pair), lambda b, p: (b, 0, OFF_GV // pair + p)),
            pl.BlockSpec((None, seq, pair), lambda b, p: (b, 0, OFF_GATE_GLA // pair + p)),
            pl.BlockSpec((None, seq, LANES), lambda b, p: (b, 0, 0)),
            pl.BlockSpec((LANES, LANES), lambda b, p: (0, p)),
            pl.BlockSpec((1, LANES), lambda b, p: (0, p)),
            pl.BlockSpec((1, HEAD_DIM), lambda b, p: (0, 0)),
        ],
        out_specs=pl.BlockSpec((None, seq, pair), lambda b, p: (b, 0, p)),
        out_shape=jax.ShapeDtypeStruct((bsz, seq, BRANCH), BF16),
        compiler_params=_cparams(("parallel", "parallel")),
        name="gla",
    )(proj, proj, proj, proj, a_lr, w_up, b_up, norm_w)


def _swa_kernel(slope_ref, sink_ref, q_ref, k_ref, v_ref, gate_ref, qn_ref, kn_ref, o_ref,
                qs_ref, ks_ref, *, seq):
    w = SWA_WINDOW
    nb = seq // w
    g_n = SWA_GROUP
    kvh = pl.program_id(1)
    ks_ref[...] = _rms(k_ref[...].astype(F32), kn_ref[...]).astype(BF16)
    q_weight = qn_ref[...] * (HEAD_DIM ** -0.5)
    for g in range(g_n):
        qg = q_ref[:, g * HEAD_DIM:(g + 1) * HEAD_DIM].astype(F32)
        qs_ref[:, g * HEAD_DIM:(g + 1) * HEAD_DIM] = _rms(qg, q_weight).astype(BF16)
    qi = lax.broadcasted_iota(jnp.int32, (w, 2 * w), 0)
    kj = lax.broadcasted_iota(jnp.int32, (w, 2 * w), 1)
    dist = qi + w - kj
    valid = (dist >= 0) & (dist < w)
    distf = dist.astype(F32)
    biases, sinks = [], []
    for g in range(g_n):
        slope = slope_ref[kvh * g_n + g]
        biases.append(jnp.where(valid, -slope * distf, -jnp.inf))
        sinks.append(sink_ref[kvh * g_n + g])

    def block(q_rows, k_win, v_win, bias_cols):
        q_st = jnp.concatenate(
            [qs_ref[q_rows, g * HEAD_DIM:(g + 1) * HEAD_DIM] for g in range(g_n)], axis=0)
        s = _nt_dot(q_st, k_win)
        ps, dens = [], []
        for g in range(g_n):
            sg = s[g * w:(g + 1) * w] + biases[g][:, bias_cols]
            m = jnp.maximum(jnp.max(sg, axis=-1, keepdims=True), sinks[g])
            p = jnp.exp(sg - m)
            dens.append(jnp.sum(p, axis=-1, keepdims=True) + jnp.exp(sinks[g] - m))
            ps.append(p.astype(BF16))
        o = jnp.dot(jnp.concatenate(ps, axis=0), v_win, preferred_element_type=F32)
        for g in range(g_n):
            cols = slice(g * HEAD_DIM, (g + 1) * HEAD_DIM)
            og = o[g * w:(g + 1) * w] / dens[g]
            o_ref[q_rows, cols] = _gated(og, gate_ref[q_rows, cols])

    block(pl.ds(0, w), ks_ref[0:w, :], v_ref[0:w, :], slice(w, 2 * w))

    def body(n, carry):
        q_rows = pl.ds(pl.multiple_of(n * w, w), w)
        win = pl.ds(pl.multiple_of((n - 1) * w, w), 2 * w)
        block(q_rows, ks_ref[win, :], v_ref[win, :], slice(0, 2 * w))
        return carry

    lax.fori_loop(1, nb, body, 0, unroll=3)


def _swa(proj, q_norm, k_norm, slopes, sinks):
    bsz, seq, _ = proj.shape
    qw = SWA_GROUP * HEAD_DIM
    kern = functools.partial(_swa_kernel, seq=seq)
    smem = pl.BlockSpec(memory_space=pltpu.SMEM)
    return pl.pallas_call(
        kern,
        grid=(bsz, SWA_KV_HEADS),
        in_specs=[
            smem, smem,
            pl.BlockSpec((None, seq, qw), lambda b, h: (b, 0, OFF_SQ // qw + h)),
            pl.BlockSpec((None, seq, HEAD_DIM), lambda b, h: (b, 0, OFF_SK // HEAD_DIM + h)),
            pl.BlockSpec((None, seq, HEAD_DIM), lambda b, h: (b, 0, OFF_SV // HEAD_DIM + h)),
            pl.BlockSpec((None, seq, qw), lambda b, h: (b, 0, OFF_GATE_SWA // qw + h)),
            pl.BlockSpec((1, HEAD_DIM), lambda b, h: (0, 0)),
            pl.BlockSpec((1, HEAD_DIM), lambda b, h: (0, 0)),
        ],
        out_specs=pl.BlockSpec((None, seq, qw), lambda b, h: (b, 0, h)),
        out_shape=jax.ShapeDtypeStruct((bsz, seq, BRANCH), BF16),
        scratch_shapes=[pltpu.VMEM((seq, qw), BF16), pltpu.VMEM((seq, HEAD_DIM), BF16)],
        compiler_params=_cparams(("parallel", "parallel")),
        name="swa",
    )(slopes, sinks, proj, proj, proj, proj, q_norm, k_norm)


DIFF_TQ = 256
DIFF_HEADS_PER_STEP = 2


def _group_mean(x, group):
    gi = lax.broadcasted_iota(jnp.int32, (LANES, LANES), 0) // group
    gj = lax.broadcasted_iota(jnp.int32, (LANES, LANES), 1) // group
    averager = jnp.where(gi == gj, 1.0 / group, 0.0).astype(BF16)
    hi = x.astype(BF16)
    lo = (x - hi.astype(F32)).astype(BF16)
    return (jnp.dot(hi, averager, preferred_element_type=F32)
            + jnp.dot(lo, averager, preferred_element_type=F32))


def _half_rms(x, w2):
    return x * lax.rsqrt(_group_mean(x * x, DIFF_DQK) + NORM_EPS) * w2


def _diff_kernel(slope_ref, q_ref, k_ref, v_ref, gate_ref, qn_ref, kn_ref, lq1_ref, lk1_ref,
                 lq2_ref, lk2_ref, onw_ref, o_ref, *scratch, seq, lambda_init):
    qs_refs, kt_refs, va_refs = scratch[0::3], scratch[1::3], scratch[2::3]
    t = DIFF_TQ
    nq = seq // t
    n_terms = 3
    lam = (jnp.exp(jnp.sum(lq1_ref[...] * lk1_ref[...], axis=-1, keepdims=True))
           - jnp.exp(jnp.sum(lq2_ref[...] * lk2_ref[...], axis=-1, keepdims=True)) + lambda_init)
    out_weight = onw_ref[...] * (1.0 - lambda_init)
    lane = lax.broadcasted_iota(jnp.int32, (1, LANES), 1)
    bias_tile = 2 * SUBLANES
    sub = lax.broadcasted_iota(jnp.int32, (bias_tile, 1), 0)
    key_pos = lax.broadcasted_iota(jnp.int32, (1, seq), 1).astype(F32)
    ones = jnp.broadcast_to(jnp.where(lane < n_terms, 1.0, 0.0).astype(BF16), (2 * t, HEAD_DIM))
    one_cols = jnp.ones((seq, HEAD_DIM), BF16)
    r = lax.broadcasted_iota(jnp.int32, (2 * t, t), 0) & (t - 1)
    c = lax.broadcasted_iota(jnp.int32, (2 * t, t), 1)
    causal = r >= c

    def prepare(hh):
        cols = slice(hh * HEAD_DIM, (hh + 1) * HEAD_DIM)
        slope = slope_ref[pl.program_id(1) * DIFF_HEADS_PER_STEP + hh]
        kt_refs[hh][0:HEAD_DIM, :] = _half_rms(k_ref[:, cols].astype(F32), kn_ref[...]).T.astype(BF16)
        rest = key_pos * (slope * LOG2E)
        bias_rows = jnp.zeros((bias_tile, seq), F32)
        for term in range(n_terms):
            piece = rest.astype(BF16).astype(F32)
            bias_rows = jnp.where(sub == term, piece, bias_rows)
            rest = rest - piece
        kt_refs[hh][HEAD_DIM:HEAD_DIM + bias_tile, :] = bias_rows.astype(BF16)
        kt_refs[hh][HEAD_DIM + bias_tile:2 * HEAD_DIM, :] = jnp.zeros(
            (HEAD_DIM - bias_tile, seq), BF16)
        va_refs[hh][:, 0:HEAD_DIM] = v_ref[:, cols]
        va_refs[hh][:, HEAD_DIM:2 * HEAD_DIM] = one_cols
        qn = _half_rms(q_ref[:, cols].astype(F32), qn_ref[...] * (DIFF_DQK ** -0.5 * LOG2E))
        q_lo = jnp.where(lane < DIFF_DQK, qn, 0.0).astype(BF16)
        q_hi = jnp.where(lane < DIFF_DQK, 0.0, qn).astype(BF16)
        for i in range(nq):
            qs_refs[hh][i, 0:t, 0:HEAD_DIM] = q_lo[i * t:(i + 1) * t]
            qs_refs[hh][i, t:2 * t, 0:HEAD_DIM] = q_hi[i * t:(i + 1) * t]
            qs_refs[hh][i, :, HEAD_DIM:2 * HEAD_DIM] = ones

    def scores(hh, i):
        kk = (i + 1) * t
        s = jnp.dot(qs_refs[hh][i], kt_refs[hh][:, 0:kk], preferred_element_type=F32)
        s_diag = jnp.where(causal, s[:, kk - t:kk], -jnp.inf)
        return s_diag if i == 0 else jnp.concatenate([s[:, 0:kk - t], s_diag], axis=1)

    def finish(hh, i, s):
        kk = (i + 1) * t
        rows = slice(i * t, kk)
        cols = slice(hh * HEAD_DIM, (hh + 1) * HEAD_DIM)
        m = jnp.max(s, axis=-1, keepdims=True)
        p = jnp.exp2(s - m).astype(BF16)
        o_aug = jnp.dot(p, va_refs[hh][0:kk, :], preferred_element_type=F32)
        o12 = o_aug[:, 0:HEAD_DIM] * (1.0 / o_aug[:, HEAD_DIM:2 * HEAD_DIM])
        o = o12[0:t] - lam * o12[t:2 * t]
        o = _rms(o, out_weight)
        o_ref[rows, cols] = _gated(o, gate_ref[rows, cols])

    heads = range(DIFF_HEADS_PER_STEP)
    order = list(range(nq - 1, -1, -1))
    for hh in heads:
        prepare(hh)
    s_next = [scores(hh, order[0]) for hh in heads]
    for pos, i in enumerate(order):
        for hh in heads:
            s = s_next[hh]
            if pos + 1 < nq:
                s_next[hh] = scores(hh, order[pos + 1])
            finish(hh, i, s)


def _diff(proj, slopes, q_norm2, k_norm2, lq1, lk1, lq2, lk2, out_norm, lambda_init):
    bsz, seq, _ = proj.shape
    t = DIFF_TQ
    hps = DIFF_HEADS_PER_STEP
    width = hps * HEAD_DIM
    kern = functools.partial(_diff_kernel, seq=seq, lambda_init=lambda_init)
    vec = lambda n: pl.BlockSpec((1, n), lambda b, h: (0, 0))
    blk = lambda off: pl.BlockSpec((None, seq, width), lambda b, h: (b, 0, off // width + h))
    return pl.pallas_call(
        kern,
        grid=(bsz, DIFF_HEADS // hps),
        in_specs=[
            pl.BlockSpec(memory_space=pltpu.SMEM),
            blk(OFF_DQ), blk(OFF_DK), blk(OFF_DV), blk(OFF_GATE_DIFF),
            vec(HEAD_DIM), vec(HEAD_DIM),
            vec(DIFF_DQK), vec(DIFF_DQK), vec(DIFF_DQK), vec(DIFF_DQK),
            vec(HEAD_DIM),
        ],
        out_specs=pl.BlockSpec((None, seq, width), lambda b, h: (b, 0, h)),
        out_shape=jax.ShapeDtypeStruct((bsz, seq, BRANCH), BF16),
        scratch_shapes=[
            pltpu.VMEM((seq // t, 2 * t, 2 * HEAD_DIM), BF16),
            pltpu.VMEM((2 * HEAD_DIM, seq), BF16),
            pltpu.VMEM((seq, 2 * HEAD_DIM), BF16),
        ] * hps,
        compiler_params=_cparams(("parallel", "parallel")),
        name="diffattn",
    )(slopes, proj, proj, proj, proj, q_norm2, k_norm2, lq1, lk1, lq2, lk2, out_norm)


LRU_TILE = 512


def _lru_kernel(x_ref, gate_ref, cw_ref, cb_ref, wg_ref, br_ref, bi_ref, lam_ref, o_ref,
                a_scr, h_scr, *, seq):
    width = LRU_TILE
    x = x_ref[...].astype(F32)
    row = lax.broadcasted_iota(jnp.int32, (seq, width), 0)
    xc = x * cw_ref[CONV_WIDTH - 1:CONV_WIDTH, :] + cb_ref[...]
    for s in range(1, CONV_WIDTH):
        tap = cw_ref[CONV_WIDTH - 1 - s:CONV_WIDTH - s, :]
        xc = xc + jnp.where(row >= s, pltpu.roll(x, s, 0), 0.0) * tap
    lam = lam_ref[...]
    decay_rate = (-LRU_C) * (jnp.maximum(-lam, 0.0) + jnp.log1p(jnp.exp(-jnp.abs(lam))))
    sub = lax.broadcasted_iota(jnp.int32, (seq, HEAD_DIM), 0) & (SUBLANES - 1)
    for n in range(width // HEAD_DIM):
        cols = slice(n * HEAD_DIM, (n + 1) * HEAD_DIM)
        xn = xc[:, cols]
        ri = jnp.dot(xn.astype(BF16), wg_ref[n], preferred_element_type=F32)
        r = jax.nn.sigmoid(ri[:, 0:HEAD_DIM] + br_ref[:, cols])
        gi = jax.nn.sigmoid(ri[:, HEAD_DIM:2 * HEAD_DIM] + bi_ref[:, cols])
        log_a = r * decay_rate[:, cols]
        a = jnp.exp(log_a)
        u = jnp.sqrt(1.0 - a * a) * (gi * xn)
        shift = 1
        while shift < SUBLANES:
            keep = sub >= shift
            u = jnp.where(keep, a * pltpu.roll(u, shift, 0) + u, u)
            a = jnp.where(keep, a * pltpu.roll(a, shift, 0), a)
            shift *= 2
        a_scr[:, cols] = a
        h_scr[:, cols] = u

    def body(tile, carry):
        rows = pl.ds(pl.multiple_of(tile * SUBLANES, SUBLANES), SUBLANES)
        h = a_scr[rows, :] * carry + h_scr[rows, :]
        h_scr[rows, :] = h
        return jnp.broadcast_to(h[SUBLANES - 1:SUBLANES, :], (SUBLANES, width))

    lax.fori_loop(0, seq // SUBLANES, body, jnp.zeros((SUBLANES, width), F32), unroll=8)
    o_ref[...] = _gated(h_scr[...], gate_ref[...])


def _lru(proj, conv_w, conv_b, w_gate, b_r, b_i, lam):
    bsz, seq, _ = proj.shape
    t = LRU_TILE
    nblk = t // HEAD_DIM
    kern = functools.partial(_lru_kernel, seq=seq)
    vec = pl.BlockSpec((1, t), lambda b, c: (0, c))
    return pl.pallas_call(
        kern,
        grid=(bsz, BRANCH // t),
        in_specs=[
            pl.BlockSpec((None, seq, t), lambda b, c: (b, 0, OFF_RX // t + c)),
            pl.BlockSpec((None, seq, t), lambda b, c: (b, 0, OFF_GATE_LRU // t + c)),
            pl.BlockSpec((CONV_WIDTH, t), lambda b, c: (0, c)),
            vec,
            pl.BlockSpec((nblk, HEAD_DIM, 2 * HEAD_DIM), lambda b, c: (c, 0, 0)),
            vec, vec, vec,
        ],
        out_specs=pl.BlockSpec((None, seq, t), lambda b, c: (b, 0, c)),
        out_shape=jax.ShapeDtypeStruct((bsz, seq, BRANCH), BF16),
        scratch_shapes=[pltpu.VMEM((seq, t), F32), pltpu.VMEM((seq, t), F32)],
        compiler_params=_cparams(("parallel", "parallel")),
        name="rglru",
    )(proj, proj, conv_w, conv_b, w_gate, b_r, b_i, lam)


def _outproj_kernel(y0, y1, y2, y3, w_ref, x_ref, *rest, d):
    acc = x_ref[...]
    for k, y in enumerate((y0, y1, y2, y3)):
        acc = acc + jnp.dot(y[...], w_ref[k * BRANCH:(k + 1) * BRANCH, :],
                            preferred_element_type=F32)
    if len(rest) == 1:
        (o_ref,) = rest
        o_ref[...] = acc
        return
    nw_ref, o_ref, h_ref, rs_ref = rest
    j = pl.program_id(1)
    o_ref[...] = acc
    h_ref[...] = (acc * nw_ref[...]).astype(h_ref.dtype)
    folded = _lane_fold(acc * acc)

    @pl.when(j == 0)
    def _():
        rs_ref[...] = folded

    @pl.when(j > 0)
    def _():
        rs_ref[...] += folded

    @pl.when(j == pl.num_programs(1) - 1)
    def _():
        rs_ref[...] = _row_scale(rs_ref[...], d)


def _outproj(ys, w_out_b, x2, next_norm_w=None, tm=1024, tn=1024):
    m, d = x2.shape
    in_specs = [pl.BlockSpec((tm, BRANCH), lambda i, j: (i, 0)) for _ in range(4)] + [
        pl.BlockSpec((d, tn), lambda i, j: (0, j)),
        pl.BlockSpec((tm, tn), lambda i, j: (i, j)),
    ]
    out_specs = [pl.BlockSpec((tm, tn), lambda i, j: (i, j))]
    out_shape = [jax.ShapeDtypeStruct((m, d), F32)]
    args = [*ys, w_out_b, x2]
    if next_norm_w is not None:
        in_specs.append(pl.BlockSpec((1, tn), lambda i, j: (0, j)))
        out_specs += [pl.BlockSpec((tm, tn), lambda i, j: (i, j)),
                      pl.BlockSpec((tm, LANES), lambda i, j: (i, 0))]
        out_shape += [jax.ShapeDtypeStruct((m, d), BF16), jax.ShapeDtypeStruct((m, LANES), F32)]
        args.append(next_norm_w)
    return pl.pallas_call(
        functools.partial(_outproj_kernel, d=d),
        grid=(m // tm, d // tn),
        in_specs=in_specs,
        out_specs=out_specs,
        out_shape=out_shape,
        compiler_params=_cparams(("parallel", "arbitrary")),
        name="outproj",
    )(*args)


def _alibi_slopes(n):
    return 2.0 ** (-8.0 * jnp.arange(1, n + 1, dtype=F32) / n)


def _layer(x2, h, rs, bsz, seq, layer, w_in_t, w_out, p, next_norm_w):
    proj2, a2, w_out_b = _inproj(h, rs, w_in_t, w_out, layer)
    proj = proj2.reshape(bsz, seq, N_MAIN)
    a_lr = a2.reshape(bsz, seq, LANES)

    w_up = jnp.pad(p['gla_w_up'], ((0, LANES - GLA_RANK), (0, 0)))
    y_gla = _gla(proj, a_lr, w_up, p['gla_b_up'].reshape(1, -1), p['gla_norm_w'].reshape(1, -1))

    y_swa = _swa(proj, p['swa_q_norm'].reshape(1, -1), p['swa_k_norm'].reshape(1, -1),
                 _alibi_slopes(SWA_GROUP * SWA_KV_HEADS), p['swa_sinks'])

    lambda_init = 0.8 - 0.6 * math.exp(-0.3 * layer)
    tile2 = lambda v: jnp.concatenate([v, v]).reshape(1, -1)
    y_diff = _diff(proj, _alibi_slopes(DIFF_HEADS), tile2(p['diff_q_norm']), tile2(p['diff_k_norm']),
                   p['diff_lq1'].reshape(1, -1), p['diff_lk1'].reshape(1, -1),
                   p['diff_lq2'].reshape(1, -1), p['diff_lk2'].reshape(1, -1),
                   p['diff_out_norm'].reshape(1, -1), lambda_init)

    w_gate = jnp.concatenate([p['lru_w_r'], p['lru_w_i']], axis=-1).astype(BF16)
    y_lru = _lru(proj, p['lru_conv_w'], p['lru_conv_b'].reshape(1, -1), w_gate,
                 p['lru_b_r'].reshape(1, -1), p['lru_b_i'].reshape(1, -1),
                 p['lru_lambda'].reshape(1, -1))

    ys = [y.reshape(bsz * seq, BRANCH) for y in (y_gla, y_swa, y_diff, y_lru)]
    return _outproj(ys, w_out_b, x2, next_norm_w)


def kernel(x, norm_w, w_in, w_out, gla_w_up, gla_b_up, gla_norm_w, swa_q_norm, swa_k_norm, swa_sinks, diff_q_norm, diff_k_norm, diff_lq1, diff_lk1, diff_lq2, diff_lk2, diff_out_norm, lru_conv_w, lru_conv_b, lru_w_r, lru_b_r, lru_w_i, lru_b_i, lru_lambda):
    params = dict(norm_w=norm_w, gla_w_up=gla_w_up, gla_b_up=gla_b_up,
                  gla_norm_w=gla_norm_w, swa_q_norm=swa_q_norm, swa_k_norm=swa_k_norm,
                  swa_sinks=swa_sinks, diff_q_norm=diff_q_norm, diff_k_norm=diff_k_norm,
                  diff_lq1=diff_lq1, diff_lk1=diff_lk1, diff_lq2=diff_lq2, diff_lk2=diff_lk2,
                  diff_out_norm=diff_out_norm, lru_conv_w=lru_conv_w, lru_conv_b=lru_conv_b,
                  lru_w_r=lru_w_r, lru_b_r=lru_b_r, lru_w_i=lru_w_i, lru_b_i=lru_b_i,
                  lru_lambda=lru_lambda)
    bsz, seq, d = x.shape
    w_in_t = jnp.swapaxes(w_in, 1, 2)
    x2 = x.reshape(bsz * seq, d)
    depth = norm_w.shape[0]
    h, rs = _norm(x2, norm_w[0].reshape(1, -1))
    for layer in range(depth):
        p = {k: v[layer] for k, v in params.items()}
        if layer + 1 < depth:
            x2, h, rs = _layer(x2, h, rs, bsz, seq, layer, w_in_t, w_out, p,
                               norm_w[layer + 1].reshape(1, -1))
        else:
            (x2,) = _layer(x2, h, rs, bsz, seq, layer, w_in_t, w_out, p, None)
    return x2.reshape(bsz, seq, d)
```

```python
import functools
import math

import jax
import jax.numpy as jnp
from jax import lax
from jax.experimental import pallas as pl
from jax.experimental.pallas import tpu as pltpu

F32 = jnp.float32
BF16 = jnp.bfloat16

D_MODEL = 4096
HEAD_DIM = 128
BRANCH = D_MODEL // 4
NORM_EPS = 1e-6
GLA_DK = 64
GLA_RANK = 16
GLA_TAU = 16.0
GLA_CHUNK = 64
SWA_WINDOW = 128
SWA_GROUP = 4
SWA_KV_HEADS = 2
DIFF_DQK = 64
DIFF_HEADS = 8
LRU_C = 8.0
CONV_WIDTH = 4
LANES = 128
SUBLANES = 8
LOG2E = math.log2(math.e)

OFF_GQ, OFF_GK, OFF_GV = 0, 512, 1024
OFF_SQ = 2048
OFF_DQ, OFF_DK, OFF_DV = 3072, 4096, 5120
OFF_RX = 6144
OFF_GATE = 7168
OFF_SK, OFF_SV = 11264, 11520
N_MAIN = 11776
OFF_GATE_GLA = OFF_GATE
OFF_GATE_SWA = OFF_GATE + BRANCH
OFF_GATE_DIFF = OFF_GATE + 2 * BRANCH
OFF_GATE_LRU = OFF_GATE + 3 * BRANCH
INPROJ_TN = 512
ORIG_A_ROW = 2048
N_PLAIN_TILES = 2048 // INPROJ_TN
N_SQ_END_TILE = 3072 // INPROJ_TN
LAST_TILE = N_MAIN // INPROJ_TN - 1

V7X_VMEM_BYTES = 64 * 1024 * 1024
VMEM_LIMIT = V7X_VMEM_BYTES - 4 * 1024 * 1024


def _cparams(sem):
    return pltpu.CompilerParams(dimension_semantics=sem, vmem_limit_bytes=VMEM_LIMIT)


def _rms(x, w):
    ms = jnp.mean(x * x, axis=-1, keepdims=True)
    return x * lax.rsqrt(ms + NORM_EPS) * w


def _gated(y, gate):
    g = gate.astype(F32)
    return (y * (g * jax.nn.sigmoid(g))).astype(BF16)


def _lane_fold(sq):
    acc = sq[:, 0:LANES]
    for g in range(1, sq.shape[1] // LANES):
        acc = acc + sq[:, g * LANES:(g + 1) * LANES]
    return acc


def _row_scale(folded, d):
    ms = jnp.sum(folded, axis=-1, keepdims=True) * (1.0 / d)
    return jnp.broadcast_to(lax.rsqrt(ms + NORM_EPS), folded.shape)


def _norm_kernel(x_ref, w_ref, o_ref, rs_ref):
    x = x_ref[...]
    o_ref[...] = (x * w_ref[...]).astype(o_ref.dtype)
    rs_ref[...] = _row_scale(_lane_fold(x * x), x.shape[1])


def _norm(x2, w, tm=512):
    m, d = x2.shape
    return pl.pallas_call(
        _norm_kernel,
        grid=(m // tm,),
        in_specs=[pl.BlockSpec((tm, d), lambda i: (i, 0)),
                  pl.BlockSpec((1, d), lambda i: (0, 0))],
        out_specs=[pl.BlockSpec((tm, d), lambda i: (i, 0)),
                   pl.BlockSpec((tm, LANES), lambda i: (i, 0))],
        out_shape=[jax.ShapeDtypeStruct((m, d), BF16),
                   jax.ShapeDtypeStruct((m, LANES), F32)],
        compiler_params=_cparams(("parallel",)),
        name="rmsnorm",
    )(x2, w)


def _nt_dot(a, b):
    return lax.dot_general(a, b, (((1,), (1,)), ((), ())), preferred_element_type=F32)


WOUT_CAST_ROWS = 32
WOUT_CAST_STEPS = 16


def _inproj_kernel(h_ref, rs_ref, w_ref, wa_ref, wo_ref, o_ref, a_ref, wob_ref):
    j = pl.program_id(1)
    h = h_ref[...]
    rs = rs_ref[...]
    acc = _nt_dot(h, w_ref[0].astype(BF16))
    o_ref[...] = (acc * jnp.concatenate([rs] * (acc.shape[1] // LANES), axis=1)).astype(o_ref.dtype)

    @pl.when(j == 0)
    def _():
        pad = jnp.zeros((LANES - GLA_RANK, wa_ref.shape[1]), BF16)
        a_ref[...] = _nt_dot(h, jnp.concatenate([wa_ref[...].astype(BF16), pad], axis=0)) * rs

    wob_ref[...] = wo_ref[...].astype(BF16)


def _src_row(j):
    tile = jnp.where(j < N_SQ_END_TILE, j, jnp.where(j == LAST_TILE, N_SQ_END_TILE, j + 1))
    units = tile * (INPROJ_TN // GLA_RANK) + jnp.where(j < N_PLAIN_TILES, 0, 1)
    return pl.multiple_of(units * GLA_RANK, GLA_RANK)


def _inproj(h, rs, w_in_t, w_out, layer, tm=1024):
    m, d = h.shape
    tn = INPROJ_TN
    k_out, n_out = w_out.shape[1:]
    assert (m // tm) * WOUT_CAST_STEPS * WOUT_CAST_ROWS == k_out
    assert N_MAIN // tn >= WOUT_CAST_STEPS
    slab = lambda i, j: i * WOUT_CAST_STEPS + jnp.minimum(j, WOUT_CAST_STEPS - 1)
    return pl.pallas_call(
        _inproj_kernel,
        grid=(m // tm, N_MAIN // tn),
        in_specs=[pl.BlockSpec((tm, d), lambda i, j: (i, 0)),
                  pl.BlockSpec((tm, LANES), lambda i, j: (i, 0)),
                  pl.BlockSpec((pl.Element(1), pl.Element(tn), pl.Element(d)),
                               lambda i, j: (layer, _src_row(j), 0)),
                  pl.BlockSpec((None, GLA_RANK, d), lambda i, j: (layer, ORIG_A_ROW // GLA_RANK, 0)),
                  pl.BlockSpec((None, WOUT_CAST_ROWS, n_out), lambda i, j: (layer, slab(i, j), 0))],
        out_specs=[pl.BlockSpec((tm, tn), lambda i, j: (i, j)),
                   pl.BlockSpec((tm, LANES), lambda i, j: (i, 0)),
                   pl.BlockSpec((WOUT_CAST_ROWS, n_out), lambda i, j: (slab(i, j), 0))],
        out_shape=[jax.ShapeDtypeStruct((m, N_MAIN), BF16),
                   jax.ShapeDtypeStruct((m, LANES), F32),
                   jax.ShapeDtypeStruct((k_out, n_out), BF16)],
        compiler_params=_cparams(("parallel", "arbitrary")),
        name="inproj",
    )(h, rs, w_in_t, w_in_t, w_out)


def _gla_kernel(q_ref, k_ref, v_ref, gate_ref, a_ref, wup_ref, bup_ref, nw_ref, o_ref, *, seq):
    c = GLA_CHUNK
    nc = seq // c
    dv2 = 2 * HEAD_DIM
    logit = jnp.dot(a_ref[...], wup_ref[...], preferred_element_type=F32) + bup_ref[...]
    g = (jnp.minimum(logit, 0.0) - jnp.log(1.0 + jnp.exp(-jnp.abs(logit)))) * (LOG2E / GLA_TAU)
    row = lax.broadcasted_iota(jnp.int32, (seq, LANES), 0) & (c - 1)
    b = g
    shift = 1
    while shift < SUBLANES:
        b = b + jnp.where(row >= shift, pltpu.roll(b, shift, 0), 0.0)
        shift *= 2
    b4 = b.reshape(nc, c // SUBLANES, SUBLANES, LANES)
    skip = 1
    while skip < c // SUBLANES:
        b4 = jnp.concatenate([b4[:, :skip], b4[:, skip:] + b4[:, :-skip]], axis=1)
        skip *= 2
    b3 = b4.reshape(nc, c, LANES)
    b_last = b3[:, c - 1:c, :]
    q3 = q_ref[...].astype(F32).reshape(nc, c, LANES)
    k3 = k_ref[...].astype(F32).reshape(nc, c, LANES)
    v3 = v_ref[...].reshape(nc, c, dv2)
    q_dec = q3 * (GLA_DK ** -0.5) * jnp.exp2(b3)
    k_dec = (k3 * jnp.exp2(-b3)).astype(BF16)
    k_state = (k3 * jnp.exp2(b_last - b3)).astype(BF16)
    decay = jnp.exp2(b_last)
    lane = lax.broadcasted_iota(jnp.int32, (1, 1, LANES), 2)
    q_st = jnp.concatenate([jnp.where(lane < GLA_DK, q_dec, 0.0),
                            jnp.where(lane >= GLA_DK, q_dec, 0.0)], axis=1).astype(BF16)
    att = jnp.einsum('nid,njd->nij', q_st, k_dec, preferred_element_type=F32)
    ii = lax.broadcasted_iota(jnp.int32, (1, 2 * c, c), 1) & (c - 1)
    jj = lax.broadcasted_iota(jnp.int32, (1, 2 * c, c), 2)
    att = jnp.where(ii >= jj, att, 0.0).astype(BF16)
    heads = ((slice(0, c), slice(0, HEAD_DIM)), (slice(c, 2 * c), slice(HEAD_DIM, dv2)))
    o_heads = [jnp.einsum('nij,njv->niv', att[:, rs], v3[:, :, vs], preferred_element_type=F32)
               for rs, vs in heads]
    v3_t = jnp.swapaxes(v3.astype(F32), 1, 2).astype(BF16)
    u_t = jnp.einsum('nvj,njd->nvd', v3_t, k_state, preferred_element_type=F32)
    st = jnp.zeros((dv2, LANES), F32)
    starts = []
    for n in range(nc):
        starts.append(st.astype(BF16))
        st = st * decay[n] + u_t[n]
    s_start = jnp.stack(starts, axis=0)
    nw = nw_ref[...]
    for (rs, vs), o_intra in zip(heads, o_heads):
        o = o_intra + jnp.einsum('nid,nvd->niv', q_st[:, rs], s_start[:, vs],
                                 preferred_element_type=F32)
        o = _rms(o, nw).reshape(seq, HEAD_DIM)
        o_ref[:, vs] = _gated(o, gate_ref[:, vs])


def _gla(proj, a_lr, w_up, b_up, norm_w):
    bsz, seq, _ = proj.shape
    pair = 2 * HEAD_DIM
    kern = functools.partial(_gla_kernel, seq=seq)
    return pl.pallas_call(
        kern,
        grid=(bsz, BRANCH // pair),
        in_specs=[
            pl.BlockSpec((None, seq, LANES), lambda b, p: (b, 0, OFF_GQ // LANES + p)),
            pl.BlockSpec((None, seq, LANES), lambda b, p: (b, 0, OFF_GK // LANES + p)),
            pl.BlockSpec((None, seq, pair), lambda b, p: (b, 0, OFF_GV // pair + p)),
            pl.BlockSpec((None, seq, pair), lambda b, p: (b, 0, OFF_GATE_GLA // pair + p)),
            pl.BlockSpec((None, seq, LANES), lambda b, p: (b, 0, 0)),
            pl.BlockSpec((LANES, LANES), lambda b, p: (0, p)),
            pl.BlockSpec((1, LANES), lambda b, p: (0, p)),
            pl.BlockSpec((1, HEAD_DIM), lambda b, p: (0, 0)),
        ],
        out_specs=pl.BlockSpec((None, seq, pair), lambda b, p: (b, 0, p)),
        out_shape=jax.ShapeDtypeStruct((bsz, seq, BRANCH), BF16),
        compiler_params=_cparams(("parallel", "parallel")),
        name="gla",
    )(proj, proj, proj, proj, a_lr, w_up, b_up, norm_w)


def _swa_kernel(slope_ref, sink_ref, q_ref, k_ref, v_ref, gate_ref, qn_ref, kn_ref, o_ref,
                qs_ref, ks_ref, *, seq):
    w = SWA_WINDOW
    nb = seq // w
    g_n = SWA_GROUP
    kvh = pl.program_id(1)
    ks_ref[...] = _rms(k_ref[...].astype(F32), kn_ref[...]).astype(BF16)
    q_weight = qn_ref[...] * (HEAD_DIM ** -0.5)
    for g in range(g_n):
        qg = q_ref[:, g * HEAD_DIM:(g + 1) * HEAD_DIM].astype(F32)
        qs_ref[:, g * HEAD_DIM:(g + 1) * HEAD_DIM] = _rms(qg, q_weight).astype(BF16)
    qi = lax.broadcasted_iota(jnp.int32, (w, 2 * w), 0)
    kj = lax.broadcasted_iota(jnp.int32, (w, 2 * w), 1)
    dist = qi + w - kj
    valid = (dist >= 0) & (dist < w)
    distf = dist.astype(F32)
    biases, sinks = [], []
    for g in range(g_n):
        slope = slope_ref[kvh * g_n + g]
        biases.append(jnp.where(valid, -slope * distf, -jnp.inf))
        sinks.append(sink_ref[kvh * g_n + g])

    def block(q_rows, k_win, v_win, bias_cols):
        q_st = jnp.concatenate(
            [qs_ref[q_rows, g * HEAD_DIM:(g + 1) * HEAD_DIM] for g in range(g_n)], axis=0)
        s = _nt_dot(q_st, k_win)
        ps, dens = [], []
        for g in range(g_n):
            sg = s[g * w:(g + 1) * w] + biases[g][:, bias_cols]
            m = jnp.maximum(jnp.max(sg, axis=-1, keepdims=True), sinks[g])
            p = jnp.exp(sg - m)
            dens.append(jnp.sum(p, axis=-1, keepdims=True) + jnp.exp(sinks[g] - m))
            ps.append(p.astype(BF16))
        o = jnp.dot(jnp.concatenate(ps, axis=0), v_win, preferred_element_type=F32)
        for g in range(g_n):
            cols = slice(g * HEAD_DIM, (g + 1) * HEAD_DIM)
            og = o[g * w:(g + 1) * w] / dens[g]
            o_ref[q_rows, cols] = _gated(og, gate_ref[q_rows, cols])

    block(pl.ds(0, w), ks_ref[0:w, :], v_ref[0:w, :], slice(w, 2 * w))

    def body(n, carry):
        q_rows = pl.ds(pl.multiple_of(n * w, w), w)
        win = pl.ds(pl.multiple_of((n - 1) * w, w), 2 * w)
        block(q_rows, ks_ref[win, :], v_ref[win, :], slice(0, 2 * w))
        return carry

    lax.fori_loop(1, nb, body, 0, unroll=3)


def _swa(proj, q_norm, k_norm, slopes, sinks):
    bsz, seq, _ = proj.shape
    qw = SWA_GROUP * HEAD_DIM
    kern = functools.partial(_swa_kernel, seq=seq)
    smem = pl.BlockSpec(memory_space=pltpu.SMEM)
    return pl.pallas_call(
        kern,
        grid=(bsz, SWA_KV_HEADS),
        in_specs=[
            smem, smem,
            pl.BlockSpec((None, seq, qw), lambda b, h: (b, 0, OFF_SQ // qw + h)),
            pl.BlockSpec((None, seq, HEAD_DIM), lambda b, h: (b, 0, OFF_SK // HEAD_DIM + h)),
            pl.BlockSpec((None, seq, HEAD_DIM), lambda b, h: (b, 0, OFF_SV // HEAD_DIM + h)),
            pl.BlockSpec((None, seq, qw), lambda b, h: (b, 0, OFF_GATE_SWA // qw + h)),
            pl.BlockSpec((1, HEAD_DIM), lambda b, h: (0, 0)),
            pl.BlockSpec((1, HEAD_DIM), lambda b, h: (0, 0)),
        ],
        out_specs=pl.BlockSpec((None, seq, qw), lambda b, h: (b, 0, h)),
        out_shape=jax.ShapeDtypeStruct((bsz, seq, BRANCH), BF16),
        scratch_shapes=[pltpu.VMEM((seq, qw), BF16), pltpu.VMEM((seq, HEAD_DIM), BF16)],
        compiler_params=_cparams(("parallel", "parallel")),
        name="swa",
    )(slopes, sinks, proj, proj, proj, proj, q_norm, k_norm)


DIFF_TQ = 256
DIFF_HEADS_PER_STEP = 2


def _group_mean(x, group):
    gi = lax.broadcasted_iota(jnp.int32, (LANES, LANES), 0) // group
    gj = lax.broadcasted_iota(jnp.int32, (LANES, LANES), 1) // group
    averager = jnp.where(gi == gj, 1.0 / group, 0.0).astype(BF16)
    return jnp.dot(x.astype(BF16), averager, preferred_element_type=F32)


def _half_rms(x, w2):
    return x * lax.rsqrt(_group_mean(x * x, DIFF_DQK) + NORM_EPS) * w2


def _diff_kernel(slope_ref, q_ref, k_ref, v_ref, gate_ref, qn_ref, kn_ref, lq1_ref, lk1_ref,
                 lq2_ref, lk2_ref, onw_ref, o_ref, *scratch, seq, lambda_init):
    qs_refs, kt_refs, va_refs = scratch[0::3], scratch[1::3], scratch[2::3]
    t = DIFF_TQ
    nq = seq // t
    n_terms = 3
    lam = (jnp.exp(jnp.sum(lq1_ref[...] * lk1_ref[...], axis=-1, keepdims=True))
           - jnp.exp(jnp.sum(lq2_ref[...] * lk2_ref[...], axis=-1, keepdims=True)) + lambda_init)
    out_weight = onw_ref[...] * (1.0 - lambda_init)
    lane = lax.broadcasted_iota(jnp.int32, (1, LANES), 1)
    bias_tile = 2 * SUBLANES
    sub = lax.broadcasted_iota(jnp.int32, (bias_tile, 1), 0)
    key_pos = lax.broadcasted_iota(jnp.int32, (1, seq), 1).astype(F32)
    ones = jnp.broadcast_to(jnp.where(lane < n_terms, 1.0, 0.0).astype(BF16), (2 * t, HEAD_DIM))
    one_cols = jnp.ones((seq, HEAD_DIM), BF16)
    r = lax.broadcasted_iota(jnp.int32, (2 * t, t), 0) & (t - 1)
    c = lax.broadcasted_iota(jnp.int32, (2 * t, t), 1)
    causal = r >= c

    def prepare(hh):
        cols = slice(hh * HEAD_DIM, (hh + 1) * HEAD_DIM)
        slope = slope_ref[pl.program_id(1) * DIFF_HEADS_PER_STEP + hh]
        kt_refs[hh][0:HEAD_DIM, :] = _half_rms(k_ref[:, cols].astype(F32), kn_ref[...]).T.astype(BF16)
        rest = key_pos * (slope * LOG2E)
        bias_rows = jnp.zeros((bias_tile, seq), F32)
        for term in range(n_terms):
            piece = rest.astype(BF16).astype(F32)
            bias_rows = jnp.where(sub == term, piece, bias_rows)
            rest = rest - piece
        kt_refs[hh][HEAD_DIM:HEAD_DIM + bias_tile, :] = bias_rows.astype(BF16)
        kt_refs[hh][HEAD_DIM + bias_tile:2 * HEAD_DIM, :] = jnp.zeros(
            (HEAD_DIM - bias_tile, seq), BF16)
        va_refs[hh][:, 0:HEAD_DIM] = v_ref[:, cols]
        va_refs[hh][:, HEAD_DIM:2 * HEAD_DIM] = one_cols
        qn = _half_rms(q_ref[:, cols].astype(F32), qn_ref[...] * (DIFF_DQK ** -0.5 * LOG2E))
        q_lo = jnp.where(lane < DIFF_DQK, qn, 0.0).astype(BF16)
        q_hi = jnp.where(lane < DIFF_DQK, 0.0, qn).astype(BF16)
        for i in range(nq):
            qs_refs[hh][i, 0:t, 0:HEAD_DIM] = q_lo[i * t:(i + 1) * t]
            qs_refs[hh][i, t:2 * t, 0:HEAD_DIM] = q_hi[i * t:(i + 1) * t]
            qs_refs[hh][i, :, HEAD_DIM:2 * HEAD_DIM] = ones

    def scores(hh, i):
        kk = (i + 1) * t
        s = jnp.dot(qs_refs[hh][i], kt_refs[hh][:, 0:kk], preferred_element_type=F32)
        s_diag = jnp.where(causal, s[:, kk - t:kk], -jnp.inf)
        return s_diag if i == 0 else jnp.concatenate([s[:, 0:kk - t], s_diag], axis=1)

    def finish(hh, i, s):
        kk = (i + 1) * t
        rows = slice(i * t, kk)
        cols = slice(hh * HEAD_DIM, (hh + 1) * HEAD_DIM)
        m = jnp.max(s, axis=-1, keepdims=True)
        p = jnp.exp2(s - m).astype(BF16)
        o_aug = jnp.dot(p, va_refs[hh][0:kk, :], preferred_element_type=F32)
        o12 = o_aug[:, 0:HEAD_DIM] * (1.0 / o_aug[:, HEAD_DIM:2 * HEAD_DIM])
        o = o12[0:t] - lam * o12[t:2 * t]
        o = _rms(o, out_weight)
        o_ref[rows, cols] = _gated(o, gate_ref[rows, cols])

    heads = range(DIFF_HEADS_PER_STEP)
    order = list(range(nq - 1, -1, -1))
    for hh in heads:
        prepare(hh)
    s_next = [scores(hh, order[0]) for hh in heads]
    for pos, i in enumerate(order):
        for hh in heads:
            s = s_next[hh]
            if pos + 1 < nq:
                s_next[hh] = scores(hh, order[pos + 1])
            finish(hh, i, s)


def _diff(proj, slopes, q_norm2, k_norm2, lq1, lk1, lq2, lk2, out_norm, lambda_init):
    bsz, seq, _ = proj.shape
    t = DIFF_TQ
    hps = DIFF_HEADS_PER_STEP
    width = hps * HEAD_DIM
    kern = functools.partial(_diff_kernel, seq=seq, lambda_init=lambda_init)
    vec = lambda n: pl.BlockSpec((1, n), lambda b, h: (0, 0))
    blk = lambda off: pl.BlockSpec((None, seq, width), lambda b, h: (b, 0, off // width + h))
    return pl.pallas_call(
        kern,
        grid=(bsz, DIFF_HEADS // hps),
        in_specs=[
            pl.BlockSpec(memory_space=pltpu.SMEM),
            blk(OFF_DQ), blk(OFF_DK), blk(OFF_DV), blk(OFF_GATE_DIFF),
            vec(HEAD_DIM), vec(HEAD_DIM),
            vec(DIFF_DQK), vec(DIFF_DQK), vec(DIFF_DQK), vec(DIFF_DQK),
            vec(HEAD_DIM),
        ],
        out_specs=pl.BlockSpec((None, seq, width), lambda b, h: (b, 0, h)),
        out_shape=jax.ShapeDtypeStruct((bsz, seq, BRANCH), BF16),
        scratch_shapes=[
            pltpu.VMEM((seq // t, 2 * t, 2 * HEAD_DIM), BF16),
            pltpu.VMEM((2 * HEAD_DIM, seq), BF16),
            pltpu.VMEM((seq, 2 * HEAD_DIM), BF16),
        ] * hps,
        compiler_params=_cparams(("parallel", "parallel")),
        name="diffattn",
    )(slopes, proj, proj, proj, proj, q_norm2, k_norm2, lq1, lk1, lq2, lk2, out_norm)


LRU_TILE = 512


def _lru_kernel(x_ref, gate_ref, cw_ref, cb_ref, wg_ref, br_ref, bi_ref, lam_ref, o_ref,
                a_scr, h_scr, *, seq):
    width = LRU_TILE
    x = x_ref[...].astype(F32)
    row = lax.broadcasted_iota(jnp.int32, (seq, width), 0)
    xc = x * cw_ref[CONV_WIDTH - 1:CONV_WIDTH, :] + cb_ref[...]
    for s in range(1, CONV_WIDTH):
        tap = cw_ref[CONV_WIDTH - 1 - s:CONV_WIDTH - s, :]
        xc = xc + jnp.where(row >= s, pltpu.roll(x, s, 0), 0.0) * tap
    lam = lam_ref[...]
    decay_rate = (-LRU_C) * (jnp.maximum(-lam, 0.0) + jnp.log1p(jnp.exp(-jnp.abs(lam))))
    sub = lax.broadcasted_iota(jnp.int32, (seq, HEAD_DIM), 0) & (SUBLANES - 1)
    for n in range(width // HEAD_DIM):
        cols = slice(n * HEAD_DIM, (n + 1) * HEAD_DIM)
        xn = xc[:, cols]
        ri = jnp.dot(xn.astype(BF16), wg_ref[n], preferred_element_type=F32)
        r = jax.nn.sigmoid(ri[:, 0:HEAD_DIM] + br_ref[:, cols])
        gi = jax.nn.sigmoid(ri[:, HEAD_DIM:2 * HEAD_DIM] + bi_ref[:, cols])
        log_a = r * decay_rate[:, cols]
        a = jnp.exp(log_a)
        u = jnp.sqrt(1.0 - a * a) * (gi * xn)
        shift = 1
        while shift < SUBLANES:
            keep = sub >= shift
            u = jnp.where(keep, a * pltpu.roll(u, shift, 0) + u, u)
            a = jnp.where(keep, a * pltpu.roll(a, shift, 0), a)
            shift *= 2
        a_scr[:, cols] = a
        h_scr[:, cols] = u

    def body(tile, carry):
        rows = pl.ds(pl.multiple_of(tile * SUBLANES, SUBLANES), SUBLANES)
        h = a_scr[rows, :] * carry + h_scr[rows, :]
        h_scr[rows, :] = h
        return jnp.broadcast_to(h[SUBLANES - 1:SUBLANES, :], (SUBLANES, width))

    lax.fori_loop(0, seq // SUBLANES, body, jnp.zeros((SUBLANES, width), F32), unroll=8)
    o_ref[...] = _gated(h_scr[...], gate_ref[...])


def _lru(proj, conv_w, conv_b, w_gate, b_r, b_i, lam):
    bsz, seq, _ = proj.shape
    t = LRU_TILE
    nblk = t // HEAD_DIM
    kern = functools.partial(_lru_kernel, seq=seq)
    vec = pl.BlockSpec((1, t), lambda b, c: (0, c))
    return pl.pallas_call(
        kern,
        grid=(bsz, BRANCH // t),
        in_specs=[
            pl.BlockSpec((None, seq, t), lambda b, c: (b, 0, OFF_RX // t + c)),
            pl.BlockSpec((None, seq, t), lambda b, c: (b, 0, OFF_GATE_LRU // t + c)),
            pl.BlockSpec((CONV_WIDTH, t), lambda b, c: (0, c)),
            vec,
            pl.BlockSpec((nblk, HEAD_DIM, 2 * HEAD_DIM), lambda b, c: (c, 0, 0)),
            vec, vec, vec,
        ],
        out_specs=pl.BlockSpec((None, seq, t), lambda b, c: (b, 0, c)),
        out_shape=jax.ShapeDtypeStruct((bsz, seq, BRANCH), BF16),
        scratch_shapes=[pltpu.VMEM((seq, t), F32), pltpu.VMEM((seq, t), F32)],
        compiler_params=_cparams(("parallel", "parallel")),
        name="rglru",
    )(proj, proj, conv_w, conv_b, w_gate, b_r, b_i, lam)


def _outproj_kernel(y0, y1, y2, y3, w_ref, x_ref, *rest, d):
    acc = x_ref[...]
    for k, y in enumerate((y0, y1, y2, y3)):
        acc = acc + jnp.dot(y[...], w_ref[k * BRANCH:(k + 1) * BRANCH, :],
                            preferred_element_type=F32)
    if len(rest) == 1:
        (o_ref,) = rest
        o_ref[...] = acc
        return
    nw_ref, o_ref, h_ref, rs_ref = rest
    j = pl.program_id(1)
    o_ref[...] = acc
    h_ref[...] = (acc * nw_ref[...]).astype(h_ref.dtype)
    folded = _lane_fold(acc * acc)

    @pl.when(j == 0)
    def _():
        rs_ref[...] = folded

    @pl.when(j > 0)
    def _():
        rs_ref[...] += folded

    @pl.when(j == pl.num_programs(1) - 1)
    def _():
        rs_ref[...] = _row_scale(rs_ref[...], d)


def _outproj(ys, w_out_b, x2, next_norm_w=None, tm=1024, tn=1024):
    m, d = x2.shape
    in_specs = [pl.BlockSpec((tm, BRANCH), lambda i, j: (i, 0)) for _ in range(4)] + [
        pl.BlockSpec((d, tn), lambda i, j: (0, j)),
        pl.BlockSpec((tm, tn), lambda i, j: (i, j)),
    ]
    out_specs = [pl.BlockSpec((tm, tn), lambda i, j: (i, j))]
    out_shape = [jax.ShapeDtypeStruct((m, d), F32)]
    args = [*ys, w_out_b, x2]
    if next_norm_w is not None:
        in_specs.append(pl.BlockSpec((1, tn), lambda i, j: (0, j)))
        out_specs += [pl.BlockSpec((tm, tn), lambda i, j: (i, j)),
                      pl.BlockSpec((tm, LANES), lambda i, j: (i, 0))]
        out_shape += [jax.ShapeDtypeStruct((m, d), BF16), jax.ShapeDtypeStruct((m, LANES), F32)]
        args.append(next_norm_w)
    return pl.pallas_call(
        functools.partial(_outproj_kernel, d=d),
        grid=(m // tm, d // tn),
        in_specs=in_specs,
        out_specs=out_specs,
        out_shape=out_shape,
        compiler_params=_cparams(("parallel", "arbitrary")),
        name="outproj",
    )(*args)


def _alibi_slopes(n):
    return 2.0 ** (-8.0 * jnp.arange(1, n + 1, dtype=F32) / n)


def _layer(x2, h, rs, bsz, seq, layer, w_in_t, w_out, p, next_norm_w):
    proj2, a2, w_out_b = _inproj(h, rs, w_in_t, w_out, layer)
    proj = proj2.reshape(bsz, seq, N_MAIN)
    a_lr = a2.reshape(bsz, seq, LANES)

    w_up = jnp.pad(p['gla_w_up'], ((0, LANES - GLA_RANK), (0, 0)))
    y_gla = _gla(proj, a_lr, w_up, p['gla_b_up'].reshape(1, -1), p['gla_norm_w'].reshape(1, -1))

    y_swa = _swa(proj, p['swa_q_norm'].reshape(1, -1), p['swa_k_norm'].reshape(1, -1),
                 _alibi_slopes(SWA_GROUP * SWA_KV_HEADS), p['swa_sinks'])

    lambda_init = 0.8 - 0.6 * math.exp(-0.3 * layer)
    tile2 = lambda v: jnp.concatenate([v, v]).reshape(1, -1)
    y_diff = _diff(proj, _alibi_slopes(DIFF_HEADS), tile2(p['diff_q_norm']), tile2(p['diff_k_norm']),
                   p['diff_lq1'].reshape(1, -1), p['diff_lk1'].reshape(1, -1),
                   p['diff_lq2'].reshape(1, -1), p['diff_lk2'].reshape(1, -1),
                   p['diff_out_norm'].reshape(1, -1), lambda_init)

    w_gate = jnp.concatenate([p['lru_w_r'], p['lru_w_i']], axis=-1).astype(BF16)
    y_lru = _lru(proj, p['lru_conv_w'], p['lru_conv_b'].reshape(1, -1), w_gate,
                 p['lru_b_r'].reshape(1, -1), p['lru_b_i'].reshape(1, -1),
                 p['lru_lambda'].reshape(1, -1))

    ys = [y.reshape(bsz * seq, BRANCH) for y in (y_gla, y_swa, y_diff, y_lru)]
    return _outproj(ys, w_out_b, x2, next_norm_w)


def kernel(x, norm_w, w_in, w_out, gla_w_up, gla_b_up, gla_norm_w, swa_q_norm, swa_k_norm, swa_sinks, diff_q_norm, diff_k_norm, diff_lq1, diff_lk1, diff_lq2, diff_lk2, diff_out_norm, lru_conv_w, lru_conv_b, lru_w_r, lru_b_r, lru_w_i, lru_b_i, lru_lambda):
    params = dict(norm_w=norm_w, gla_w_up=gla_w_up, gla_b_up=gla_b_up,
                  gla_norm_w=gla_norm_w, swa_q_norm=swa_q_norm, swa_k_norm=swa_k_norm,
                  swa_sinks=swa_sinks, diff_q_norm=diff_q_norm, diff_k_norm=diff_k_norm,
                  diff_lq1=diff_lq1, diff_lk1=diff_lk1, diff_lq2=diff_lq2, diff_lk2=diff_lk2,
                  diff_out_norm=diff_out_norm, lru_conv_w=lru_conv_w, lru_conv_b=lru_conv_b,
                  lru_w_r=lru_w_r, lru_b_r=lru_b_r, lru_w_i=lru_w_i, lru_b_i=lru_b_i,
                  lru_lambda=lru_lambda)
    bsz, seq, d = x.shape
    w_in_t = jnp.swapaxes(w_in, 1, 2)
    x2 = x.reshape(bsz * seq, d)
    depth = norm_w.shape[0]
    h, rs = _norm(x2, norm_w[0].reshape(1, -1))
    for layer in range(depth):
        p = {k: v[layer] for k, v in params.items()}
        if layer + 1 < depth:
            x2, h, rs = _layer(x2, h, rs, bsz, seq, layer, w_in_t, w_out, p,
                               norm_w[layer + 1].reshape(1, -1))
        else:
            (x2,) = _layer(x2, h, rs, bsz, seq, layer, w_in_t, w_out, p, None)
    return x2.reshape(bsz, seq, d)
```

```python
import functools
import math

import jax
import jax.numpy as jnp
from jax import lax
from jax.experimental import pallas as pl
from jax.experimental.pallas import tpu as pltpu

F32 = jnp.float32
BF16 = jnp.bfloat16

D_MODEL = 4096
HEAD_DIM = 128
BRANCH = D_MODEL // 4
NORM_EPS = 1e-6
GLA_DK = 64
GLA_RANK = 16
GLA_TAU = 16.0
GLA_CHUNK = 64
SWA_WINDOW = 128
SWA_GROUP = 4
SWA_KV_HEADS = 2
DIFF_DQK = 64
DIFF_HEADS = 8
LRU_C = 8.0
CONV_WIDTH = 4
LANES = 128
SUBLANES = 8
LOG2E = math.log2(math.e)

OFF_GQ, OFF_GK, OFF_GV = 0, 512, 1024
OFF_SQ = 2048
OFF_DQ, OFF_DK, OFF_DV = 3072, 4096, 5120
OFF_RX = 6144
OFF_GATE = 7168
OFF_SK, OFF_SV = 11264, 11520
N_MAIN = 11776
OFF_GATE_GLA = OFF_GATE
OFF_GATE_SWA = OFF_GATE + BRANCH
OFF_GATE_DIFF = OFF_GATE + 2 * BRANCH
OFF_GATE_LRU = OFF_GATE + 3 * BRANCH
INPROJ_TN = 512
ORIG_A_ROW = 2048
N_PLAIN_TILES = 2048 // INPROJ_TN
N_SQ_END_TILE = 3072 // INPROJ_TN
LAST_TILE = N_MAIN // INPROJ_TN - 1

V7X_VMEM_BYTES = 64 * 1024 * 1024
VMEM_LIMIT = V7X_VMEM_BYTES - 4 * 1024 * 1024


def _cparams(sem):
    return pltpu.CompilerParams(dimension_semantics=sem, vmem_limit_bytes=VMEM_LIMIT)


def _rms(x, w):
    ms = jnp.mean(x * x, axis=-1, keepdims=True)
    return x * lax.rsqrt(ms + NORM_EPS) * w


def _gated(y, gate):
    g = gate.astype(F32)
    return (y * (g * jax.nn.sigmoid(g))).astype(BF16)


def _lane_fold(sq):
    acc = sq[:, 0:LANES]
    for g in range(1, sq.shape[1] // LANES):
        acc = acc + sq[:, g * LANES:(g + 1) * LANES]
    return acc


def _row_scale(folded, d):
    ms = jnp.sum(folded, axis=-1, keepdims=True) * (1.0 / d)
    return jnp.broadcast_to(lax.rsqrt(ms + NORM_EPS), folded.shape)


def _norm_kernel(x_ref, w_ref, o_ref, rs_ref):
    x = x_ref[...]
    o_ref[...] = (x * w_ref[...]).astype(o_ref.dtype)
    rs_ref[...] = _row_scale(_lane_fold(x * x), x.shape[1])


def _norm(x2, w, tm=512):
    m, d = x2.shape
    return pl.pallas_call(
        _norm_kernel,
        grid=(m // tm,),
        in_specs=[pl.BlockSpec((tm, d), lambda i: (i, 0)),
                  pl.BlockSpec((1, d), lambda i: (0, 0))],
        out_specs=[pl.BlockSpec((tm, d), lambda i: (i, 0)),
                   pl.BlockSpec((tm, LANES), lambda i: (i, 0))],
        out_shape=[jax.ShapeDtypeStruct((m, d), BF16),
                   jax.ShapeDtypeStruct((m, LANES), F32)],
        compiler_params=_cparams(("parallel",)),
        name="rmsnorm",
    )(x2, w)


def _nt_dot(a, b):
    return lax.dot_general(a, b, (((1,), (1,)), ((), ())), preferred_element_type=F32)


WOUT_CAST_ROWS = 32
WOUT_CAST_STEPS = 16


def _inproj_kernel(h_ref, rs_ref, w_ref, wa_ref, wo_ref, o_ref, a_ref, wob_ref):
    j = pl.program_id(1)
    h = h_ref[...]
    rs = rs_ref[...]
    acc = _nt_dot(h, w_ref[0].astype(BF16))
    o_ref[...] = (acc * jnp.concatenate([rs] * (acc.shape[1] // LANES), axis=1)).astype(o_ref.dtype)

    @pl.when(j == 0)
    def _():
        pad = jnp.zeros((LANES - GLA_RANK, wa_ref.shape[1]), BF16)
        a_ref[...] = _nt_dot(h, jnp.concatenate([wa_ref[...].astype(BF16), pad], axis=0)) * rs

    wob_ref[...] = wo_ref[...].astype(BF16)


def _src_row(j):
    tile = jnp.where(j < N_SQ_END_TILE, j, jnp.where(j == LAST_TILE, N_SQ_END_TILE, j + 1))
    units = tile * (INPROJ_TN // GLA_RANK) + jnp.where(j < N_PLAIN_TILES, 0, 1)
    return pl.multiple_of(units * GLA_RANK, GLA_RANK)


def _inproj(h, rs, w_in_t, w_out, layer, tm=1024):
    m, d = h.shape
    tn = INPROJ_TN
    k_out, n_out = w_out.shape[1:]
    assert (m // tm) * WOUT_CAST_STEPS * WOUT_CAST_ROWS == k_out
    assert N_MAIN // tn >= WOUT_CAST_STEPS
    slab = lambda i, j: i * WOUT_CAST_STEPS + jnp.minimum(j, WOUT_CAST_STEPS - 1)
    return pl.pallas_call(
        _inproj_kernel,
        grid=(m // tm, N_MAIN // tn),
        in_specs=[pl.BlockSpec((tm, d), lambda i, j: (i, 0)),
                  pl.BlockSpec((tm, LANES), lambda i, j: (i, 0)),
                  pl.BlockSpec((pl.Element(1), pl.Element(tn), pl.Element(d)),
                               lambda i, j: (layer, _src_row(j), 0)),
                  pl.BlockSpec((None, GLA_RANK, d), lambda i, j: (layer, ORIG_A_ROW // GLA_RANK, 0)),
                  pl.BlockSpec((None, WOUT_CAST_ROWS, n_out), lambda i, j: (layer, slab(i, j), 0))],
        out_specs=[pl.BlockSpec((tm, tn), lambda i, j: (i, j)),
                   pl.BlockSpec((tm, LANES), lambda i, j: (i, 0)),
                   pl.BlockSpec((WOUT_CAST_ROWS, n_out), lambda i, j: (slab(i, j), 0))],
        out_shape=[jax.ShapeDtypeStruct((m, N_MAIN), BF16),
                   jax.ShapeDtypeStruct((m, LANES), F32),
                   jax.ShapeDtypeStruct((k_out, n_out), BF16)],
        compiler_params=_cparams(("parallel", "arbitrary")),
        name="inproj",
    )(h, rs, w_in_t, w_in_t, w_out)


def _gla_kernel(q_ref, k_ref, v_ref, gate_ref, a_ref, wup_ref, bup_ref, nw_ref, o_ref, *, seq):
    c = GLA_CHUNK
    nc = seq // c
    dv2 = 2 * HEAD_DIM
    logit = jnp.dot(a_ref[...], wup_ref[...], preferred_element_type=F32) + bup_ref[...]
    g = (jnp.minimum(logit, 0.0) - jnp.log(1.0 + jnp.exp(-jnp.abs(logit)))) * (LOG2E / GLA_TAU)
    row = lax.broadcasted_iota(jnp.int32, (seq, LANES), 0) & (c - 1)
    b = g
    shift = 1
    while shift < SUBLANES:
        b = b + jnp.where(row >= shift, pltpu.roll(b, shift, 0), 0.0)
        shift *= 2
    b4 = b.reshape(nc, c // SUBLANES, SUBLANES, LANES)
    skip = 1
    while skip < c // SUBLANES:
        b4 = jnp.concatenate([b4[:, :skip], b4[:, skip:] + b4[:, :-skip]], axis=1)
        skip *= 2
    b3 = b4.reshape(nc, c, LANES)
    b_last = b3[:, c - 1:c, :]
    q3 = q_ref[...].astype(F32).reshape(nc, c, LANES)
    k3 = k_ref[...].astype(F32).reshape(nc, c, LANES)
    v3 = v_ref[...].reshape(nc, c, dv2)
    q_dec = q3 * (GLA_DK ** -0.5) * jnp.exp2(b3)
    k_dec = (k3 * jnp.exp2(-b3)).astype(BF16)
    k_state = (k3 * jnp.exp2(b_last - b3)).astype(BF16)
    decay = jnp.exp2(b_last)
    lane = lax.broadcasted_iota(jnp.int32, (1, 1, LANES), 2)
    q_st = jnp.concatenate([jnp.where(lane < GLA_DK, q_dec, 0.0),
                            jnp.where(lane >= GLA_DK, q_dec, 0.0)], axis=1).astype(BF16)
    att = jnp.einsum('nid,njd->nij', q_st, k_dec, preferred_element_type=F32)
    ii = lax.broadcasted_iota(jnp.int32, (1, 2 * c, c), 1) & (c - 1)
    jj = lax.broadcasted_iota(jnp.int32, (1, 2 * c, c), 2)
    att = jnp.where(ii >= jj, att, 0.0).astype(BF16)
    heads = ((slice(0, c), slice(0, HEAD_DIM)), (slice(c, 2 * c), slice(HEAD_DIM, dv2)))
    o_heads = [jnp.einsum('nij,njv->niv', att[:, rs], v3[:, :, vs], preferred_element_type=F32)
               for rs, vs in heads]
    v3_t = jnp.swapaxes(v3.astype(F32), 1, 2).astype(BF16)
    u_t = jnp.einsum('nvj,njd->nvd', v3_t, k_state, preferred_element_type=F32)
    st = jnp.zeros((dv2, LANES), F32)
    starts = []
    for n in range(nc):
        starts.append(st.astype(BF16))
        st = st * decay[n] + u_t[n]
    s_start = jnp.stack(starts, axis=0)
    nw = nw_ref[...]
    for (rs, vs), o_intra in zip(heads, o_heads):
        o = o_intra + jnp.einsum('nid,nvd->niv', q_st[:, rs], s_start[:, vs],
                                 preferred_element_type=F32)
        o = _rms(o, nw).reshape(seq, HEAD_DIM)
        o_ref[:, vs] = _gated(o, gate_ref[:, vs])


def _gla(proj, a_lr, w_up, b_up, norm_w):
    bsz, seq, _ = proj.shape
    pair = 2 * HEAD_DIM
    kern = functools.partial(_gla_kernel, seq=seq)
    return pl.pallas_call(
        kern,
        grid=(bsz, BRANCH // pair),
        in_specs=[
            pl.BlockSpec((None, seq, LANES), lambda b, p: (b, 0, OFF_GQ // LANES + p)),
            pl.BlockSpec((None, seq, LANES), lambda b, p: (b, 0, OFF_GK // LANES + p)),
            pl.BlockSpec((None, seq, pair), lambda b, p: (b, 0, OFF_GV // pair + p)),
            pl.BlockSpec((None, seq, pair), lambda b, p: (b, 0, OFF_GATE_GLA // pair + p)),
            pl.BlockSpec((None, seq, LANES), lambda b, p: (b, 0, 0)),
            pl.BlockSpec((LANES, LANES), lambda b, p: (0, p)),
            pl.BlockSpec((1, LANES), lambda b, p: (0, p)),
            pl.BlockSpec((1, HEAD_DIM), lambda b, p: (0, 0)),
        ],
        out_specs=pl.BlockSpec((None, seq, pair), lambda b, p: (b, 0, p)),
        out_shape=jax.ShapeDtypeStruct((bsz, seq, BRANCH), BF16),
        compiler_params=_cparams(("parallel", "parallel")),
        name="gla",
    )(proj, proj, proj, proj, a_lr, w_up, b_up, norm_w)


def _swa_kernel(slope_ref, sink_ref, q_ref, k_ref, v_ref, gate_ref, qn_ref, kn_ref, o_ref,
                qs_ref, ks_ref, *, seq):
    w = SWA_WINDOW
    nb = seq // w
    g_n = SWA_GROUP
    kvh = pl.program_id(1)
    ks_ref[...] = _rms(k_ref[...].astype(F32), kn_ref[...]).astype(BF16)
    q_weight = qn_ref[...] * (HEAD_DIM ** -0.5)
    for g in range(g_n):
        qg = q_ref[:, g * HEAD_DIM:(g + 1) * HEAD_DIM].astype(F32)
        qs_ref[:, g * HEAD_DIM:(g + 1) * HEAD_DIM] = _rms(qg, q_weight).astype(BF16)
    qi = lax.broadcasted_iota(jnp.int32, (w, 2 * w), 0)
    kj = lax.broadcasted_iota(jnp.int32, (w, 2 * w), 1)
    dist = qi + w - kj
    valid = (dist >= 0) & (dist < w)
    distf = dist.astype(F32)
    biases, sinks = [], []
    for g in range(g_n):
        slope = slope_ref[kvh * g_n + g]
        biases.append(jnp.where(valid, -slope * distf, -jnp.inf))
        sinks.append(sink_ref[kvh * g_n + g])

    def block(q_rows, k_win, v_win, bias_cols):
        q_st = jnp.concatenate(
            [qs_ref[q_rows, g * HEAD_DIM:(g + 1) * HEAD_DIM] for g in range(g_n)], axis=0)
        s = _nt_dot(q_st, k_win)
        ps, dens = [], []
        for g in range(g_n):
            sg = s[g * w:(g + 1) * w] + biases[g][:, bias_cols]
            m = jnp.maximum(jnp.max(sg, axis=-1, keepdims=True), sinks[g])
            p = jnp.exp(sg - m)
            dens.append(jnp.sum(p, axis=-1, keepdims=True) + jnp.exp(sinks[g] - m))
            ps.append(p.astype(BF16))
        o = jnp.dot(jnp.concatenate(ps, axis=0), v_win, preferred_element_type=F32)
        for g in range(g_n):
            cols = slice(g * HEAD_DIM, (g + 1) * HEAD_DIM)
            og = o[g * w:(g + 1) * w] / dens[g]
            o_ref[q_rows, cols] = _gated(og, gate_ref[q_rows, cols])

    block(pl.ds(0, w), ks_ref[0:w, :], v_ref[0:w, :], slice(w, 2 * w))

    def body(n, carry):
        q_rows = pl.ds(pl.multiple_of(n * w, w), w)
        win = pl.ds(pl.multiple_of((n - 1) * w, w), 2 * w)
        block(q_rows, ks_ref[win, :], v_ref[win, :], slice(0, 2 * w))
        return carry

    lax.fori_loop(1, nb, body, 0, unroll=True)


def _swa(proj, q_norm, k_norm, slopes, sinks):
    bsz, seq, _ = proj.shape
    qw = SWA_GROUP * HEAD_DIM
    kern = functools.partial(_swa_kernel, seq=seq)
    smem = pl.BlockSpec(memory_space=pltpu.SMEM)
    return pl.pallas_call(
        kern,
        grid=(bsz, SWA_KV_HEADS),
        in_specs=[
            smem, smem,
            pl.BlockSpec((None, seq, qw), lambda b, h: (b, 0, OFF_SQ // qw + h)),
            pl.BlockSpec((None, seq, HEAD_DIM), lambda b, h: (b, 0, OFF_SK // HEAD_DIM + h)),
            pl.BlockSpec((None, seq, HEAD_DIM), lambda b, h: (b, 0, OFF_SV // HEAD_DIM + h)),
            pl.BlockSpec((None, seq, qw), lambda b, h: (b, 0, OFF_GATE_SWA // qw + h)),
            pl.BlockSpec((1, HEAD_DIM), lambda b, h: (0, 0)),
            pl.BlockSpec((1, HEAD_DIM), lambda b, h: (0, 0)),
        ],
        out_specs=pl.BlockSpec((None, seq, qw), lambda b, h: (b, 0, h)),
        out_shape=jax.ShapeDtypeStruct((bsz, seq, BRANCH), BF16),
        scratch_shapes=[pltpu.VMEM((seq, qw), BF16), pltpu.VMEM((seq, HEAD_DIM), BF16)],
        compiler_params=_cparams(("parallel", "parallel")),
        name="swa",
    )(slopes, sinks, proj, proj, proj, proj, q_norm, k_norm)


DIFF_TQ = 256
DIFF_HEADS_PER_STEP = 2


def _group_mean(x, group):
    gi = lax.broadcasted_iota(jnp.int32, (LANES, LANES), 0) // group
    gj = lax.broadcasted_iota(jnp.int32, (LANES, LANES), 1) // group
    averager = jnp.where(gi == gj, 1.0 / group, 0.0).astype(BF16)
    return jnp.dot(x.astype(BF16), averager, preferred_element_type=F32)


def _half_rms(x, w2):
    return x * lax.rsqrt(_group_mean(x * x, DIFF_DQK) + NORM_EPS) * w2


def _diff_kernel(slope_ref, q_ref, k_ref, v_ref, gate_ref, qn_ref, kn_ref, lq1_ref, lk1_ref,
                 lq2_ref, lk2_ref, onw_ref, o_ref, *scratch, seq, lambda_init):
    qs_refs, kt_refs, va_refs = scratch[0::3], scratch[1::3], scratch[2::3]
    t = DIFF_TQ
    nq = seq // t
    n_terms = 3
    lam = (jnp.exp(jnp.sum(lq1_ref[...] * lk1_ref[...], axis=-1, keepdims=True))
           - jnp.exp(jnp.sum(lq2_ref[...] * lk2_ref[...], axis=-1, keepdims=True)) + lambda_init)
    out_weight = onw_ref[...] * (1.0 - lambda_init)
    lane = lax.broadcasted_iota(jnp.int32, (1, LANES), 1)
    bias_tile = 2 * SUBLANES
    sub = lax.broadcasted_iota(jnp.int32, (bias_tile, 1), 0)
    key_pos = lax.broadcasted_iota(jnp.int32, (1, seq), 1).astype(F32)
    ones = jnp.broadcast_to(jnp.where(lane < n_terms, 1.0, 0.0).astype(BF16), (2 * t, HEAD_DIM))
    one_cols = jnp.ones((seq, HEAD_DIM), BF16)
    r = lax.broadcasted_iota(jnp.int32, (2 * t, t), 0) & (t - 1)
    c = lax.broadcasted_iota(jnp.int32, (2 * t, t), 1)
    causal = r >= c

    def prepare(hh):
        cols = slice(hh * HEAD_DIM, (hh + 1) * HEAD_DIM)
        slope = slope_ref[pl.program_id(1) * DIFF_HEADS_PER_STEP + hh]
        kt_refs[hh][0:HEAD_DIM, :] = _half_rms(k_ref[:, cols].astype(F32), kn_ref[...]).T.astype(BF16)
        rest = key_pos * (slope * LOG2E)
        bias_rows = jnp.zeros((bias_tile, seq), F32)
        for term in range(n_terms):
            piece = rest.astype(BF16).astype(F32)
            bias_rows = jnp.where(sub == term, piece, bias_rows)
            rest = rest - piece
        kt_refs[hh][HEAD_DIM:HEAD_DIM + bias_tile, :] = bias_rows.astype(BF16)
        kt_refs[hh][HEAD_DIM + bias_tile:2 * HEAD_DIM, :] = jnp.zeros(
            (HEAD_DIM - bias_tile, seq), BF16)
        va_refs[hh][:, 0:HEAD_DIM] = v_ref[:, cols]
        va_refs[hh][:, HEAD_DIM:2 * HEAD_DIM] = one_cols
        qn = _half_rms(q_ref[:, cols].astype(F32), qn_ref[...] * (DIFF_DQK ** -0.5 * LOG2E))
        q_lo = jnp.where(lane < DIFF_DQK, qn, 0.0).astype(BF16)
        q_hi = jnp.where(lane < DIFF_DQK, 0.0, qn).astype(BF16)
        for i in range(nq):
            qs_refs[hh][i, 0:t, 0:HEAD_DIM] = q_lo[i * t:(i + 1) * t]
            qs_refs[hh][i, t:2 * t, 0:HEAD_DIM] = q_hi[i * t:(i + 1) * t]
            qs_refs[hh][i, :, HEAD_DIM:2 * HEAD_DIM] = ones

    def scores(hh, i):
        kk = (i + 1) * t
        s = jnp.dot(qs_refs[hh][i], kt_refs[hh][:, 0:kk], preferred_element_type=F32)
        s_diag = jnp.where(causal, s[:, kk - t:kk], -jnp.inf)
        return s_diag if i == 0 else jnp.concatenate([s[:, 0:kk - t], s_diag], axis=1)

    def finish(hh, i, s):
        kk = (i + 1) * t
        rows = slice(i * t, kk)
        cols = slice(hh * HEAD_DIM, (hh + 1) * HEAD_DIM)
        m = jnp.max(s, axis=-1, keepdims=True)
        p = jnp.exp2(s - m).astype(BF16)
        o_aug = jnp.dot(p, va_refs[hh][0:kk, :], preferred_element_type=F32)
        o12 = o_aug[:, 0:HEAD_DIM] * (1.0 / o_aug[:, HEAD_DIM:2 * HEAD_DIM])
        o = o12[0:t] - lam * o12[t:2 * t]
        o = _rms(o, out_weight)
        o_ref[rows, cols] = _gated(o, gate_ref[rows, cols])

    heads = range(DIFF_HEADS_PER_STEP)
    order = list(range(nq - 1, -1, -1))
    for hh in heads:
        prepare(hh)
    s_next = [scores(hh, order[0]) for hh in heads]
    for pos, i in enumerate(order):
        for hh in heads:
            s = s_next[hh]
            if pos + 1 < nq:
                s_next[hh] = scores(hh, order[pos + 1])
            finish(hh, i, s)


def _diff(proj, slopes, q_norm2, k_norm2, lq1, lk1, lq2, lk2, out_norm, lambda_init):
    bsz, seq, _ = proj.shape
    t = DIFF_TQ
    hps = DIFF_HEADS_PER_STEP
    width = hps * HEAD_DIM
    kern = functools.partial(_diff_kernel, seq=seq, lambda_init=lambda_init)
    vec = lambda n: pl.BlockSpec((1, n), lambda b, h: (0, 0))
    blk = lambda off: pl.BlockSpec((None, seq, width), lambda b, h: (b, 0, off // width + h))
    return pl.pallas_call(
        kern,
        grid=(bsz, DIFF_HEADS // hps),
        in_specs=[
            pl.BlockSpec(memory_space=pltpu.SMEM),
            blk(OFF_DQ), blk(OFF_DK), blk(OFF_DV), blk(OFF_GATE_DIFF),
            vec(HEAD_DIM), vec(HEAD_DIM),
            vec(DIFF_DQK), vec(DIFF_DQK), vec(DIFF_DQK), vec(DIFF_DQK),
            vec(HEAD_DIM),
        ],
        out_specs=pl.BlockSpec((None, seq, width), lambda b, h: (b, 0, h)),
        out_shape=jax.ShapeDtypeStruct((bsz, seq, BRANCH), BF16),
        scratch_shapes=[
            pltpu.VMEM((seq // t, 2 * t, 2 * HEAD_DIM), BF16),
            pltpu.VMEM((2 * HEAD_DIM, seq), BF16),
            pltpu.VMEM((seq, 2 * HEAD_DIM), BF16),
        ] * hps,
        compiler_params=_cparams(("parallel", "parallel")),
        name="diffattn",
    )(slopes, proj, proj, proj, proj, q_norm2, k_norm2, lq1, lk1, lq2, lk2, out_norm)


LRU_TILE = 512


def _lru_kernel(x_ref, gate_ref, cw_ref, cb_ref, wg_ref, br_ref, bi_ref, lam_ref, o_ref,
                a_scr, h_scr, *, seq):
    width = LRU_TILE
    x = x_ref[...].astype(F32)
    row = lax.broadcasted_iota(jnp.int32, (seq, width), 0)
    xc = x * cw_ref[CONV_WIDTH - 1:CONV_WIDTH, :] + cb_ref[...]
    for s in range(1, CONV_WIDTH):
        tap = cw_ref[CONV_WIDTH - 1 - s:CONV_WIDTH - s, :]
        xc = xc + jnp.where(row >= s, pltpu.roll(x, s, 0), 0.0) * tap
    lam = lam_ref[...]
    decay_rate = (-LRU_C) * (jnp.maximum(-lam, 0.0) + jnp.log1p(jnp.exp(-jnp.abs(lam))))
    sub = lax.broadcasted_iota(jnp.int32, (seq, HEAD_DIM), 0) & (SUBLANES - 1)
    for n in range(width // HEAD_DIM):
        cols = slice(n * HEAD_DIM, (n + 1) * HEAD_DIM)
        xn = xc[:, cols]
        ri = jnp.dot(xn.astype(BF16), wg_ref[n], preferred_element_type=F32)
        r = jax.nn.sigmoid(ri[:, 0:HEAD_DIM] + br_ref[:, cols])
        gi = jax.nn.sigmoid(ri[:, HEAD_DIM:2 * HEAD_DIM] + bi_ref[:, cols])
        log_a = r * decay_rate[:, cols]
        a = jnp.exp(log_a)
        u = jnp.sqrt(1.0 - a * a) * (gi * xn)
        shift = 1
        while shift < SUBLANES:
            keep = sub >= shift
            u = jnp.where(keep, a * pltpu.roll(u, shift, 0) + u, u)
            a = jnp.where(keep, a * pltpu.roll(a, shift, 0), a)
            shift *= 2
        a_scr[:, cols] = a
        h_scr[:, cols] = u

    def body(tile, carry):
        rows = pl.ds(pl.multiple_of(tile * SUBLANES, SUBLANES), SUBLANES)
        h = a_scr[rows, :] * carry + h_scr[rows, :]
        h_scr[rows, :] = h
        return jnp.broadcast_to(h[SUBLANES - 1:SUBLANES, :], (SUBLANES, width))

    lax.fori_loop(0, seq // SUBLANES, body, jnp.zeros((SUBLANES, width), F32), unroll=8)
    o_ref[...] = _gated(h_scr[...], gate_ref[...])


def _lru(proj, conv_w, conv_b, w_gate, b_r, b_i, lam):
    bsz, seq, _ = proj.shape
    t = LRU_TILE
    nblk = t // HEAD_DIM
    kern = functools.partial(_lru_kernel, seq=seq)
    vec = pl.BlockSpec((1, t), lambda b, c: (0, c))
    return pl.pallas_call(
        kern,
        grid=(bsz, BRANCH // t),
        in_specs=[
            pl.BlockSpec((None, seq, t), lambda b, c: (b, 0, OFF_RX // t + c)),
            pl.BlockSpec((None, seq, t), lambda b, c: (b, 0, OFF_GATE_LRU // t + c)),
            pl.BlockSpec((CONV_WIDTH, t), lambda b, c: (0, c)),
            vec,
            pl.BlockSpec((nblk, HEAD_DIM, 2 * HEAD_DIM), lambda b, c: (c, 0, 0)),
            vec, vec, vec,
        ],
        out_specs=pl.BlockSpec((None, seq, t), lambda b, c: (b, 0, c)),
        out_shape=jax.ShapeDtypeStruct((bsz, seq, BRANCH), BF16),
        scratch_shapes=[pltpu.VMEM((seq, t), F32), pltpu.VMEM((seq, t), F32)],
        compiler_params=_cparams(("parallel", "parallel")),
        name="rglru",
    )(proj, proj, conv_w, conv_b, w_gate, b_r, b_i, lam)


def _outproj_kernel(y0, y1, y2, y3, w_ref, x_ref, *rest, d):
    acc = x_ref[...]
    for k, y in enumerate((y0, y1, y2, y3)):
        acc = acc + jnp.dot(y[...], w_ref[k * BRANCH:(k + 1) * BRANCH, :],
                            preferred_element_type=F32)
    if len(rest) == 1:
        (o_ref,) = rest
        o_ref[...] = acc
        return
    nw_ref, o_ref, h_ref, rs_ref = rest
    j = pl.program_id(1)
    o_ref[...] = acc
    h_ref[...] = (acc * nw_ref[...]).astype(h_ref.dtype)
    folded = _lane_fold(acc * acc)

    @pl.when(j == 0)
    def _():
        rs_ref[...] = folded

    @pl.when(j > 0)
    def _():
        rs_ref[...] += folded

    @pl.when(j == pl.num_programs(1) - 1)
    def _():
        rs_ref[...] = _row_scale(rs_ref[...], d)


def _outproj(ys, w_out_b, x2, next_norm_w=None, tm=1024, tn=1024):
    m, d = x2.shape
    in_specs = [pl.BlockSpec((tm, BRANCH), lambda i, j: (i, 0)) for _ in range(4)] + [
        pl.BlockSpec((d, tn), lambda i, j: (0, j)),
        pl.BlockSpec((tm, tn), lambda i, j: (i, j)),
    ]
    out_specs = [pl.BlockSpec((tm, tn), lambda i, j: (i, j))]
    out_shape = [jax.ShapeDtypeStruct((m, d), F32)]
    args = [*ys, w_out_b, x2]
    if next_norm_w is not None:
        in_specs.append(pl.BlockSpec((1, tn), lambda i, j: (0, j)))
        out_specs += [pl.BlockSpec((tm, tn), lambda i, j: (i, j)),
                      pl.BlockSpec((tm, LANES), lambda i, j: (i, 0))]
        out_shape += [jax.ShapeDtypeStruct((m, d), BF16), jax.ShapeDtypeStruct((m, LANES), F32)]
        args.append(next_norm_w)
    return pl.pallas_call(
        functools.partial(_outproj_kernel, d=d),
        grid=(m // tm, d // tn),
        in_specs=in_specs,
        out_specs=out_specs,
        out_shape=out_shape,
        compiler_params=_cparams(("parallel", "arbitrary")),
        name="outproj",
    )(*args)


def _alibi_slopes(n):
    return 2.0 ** (-8.0 * jnp.arange(1, n + 1, dtype=F32) / n)


def _layer(x2, h, rs, bsz, seq, layer, w_in_t, w_out, p, next_norm_w):
    proj2, a2, w_out_b = _inproj(h, rs, w_in_t, w_out, layer)
    proj = proj2.reshape(bsz, seq, N_MAIN)
    a_lr = a2.reshape(bsz, seq, LANES)

    w_up = jnp.pad(p['gla_w_up'], ((0, LANES - GLA_RANK), (0, 0)))
    y_gla = _gla(proj, a_lr, w_up, p['gla_b_up'].reshape(1, -1), p['gla_norm_w'].reshape(1, -1))

    y_swa = _swa(proj, p['swa_q_norm'].reshape(1, -1), p['swa_k_norm'].reshape(1, -1),
                 _alibi_slopes(SWA_GROUP * SWA_KV_HEADS), p['swa_sinks'])

    lambda_init = 0.8 - 0.6 * math.exp(-0.3 * layer)
    tile2 = lambda v: jnp.concatenate([v, v]).reshape(1, -1)
    y_diff = _diff(proj, _alibi_slopes(DIFF_HEADS), tile2(p['diff_q_norm']), tile2(p['diff_k_norm']),
                   p['diff_lq1'].reshape(1, -1), p['diff_lk1'].reshape(1, -1),
                   p['diff_lq2'].reshape(1, -1), p['diff_lk2'].reshape(1, -1),
                   p['diff_out_norm'].reshape(1, -1), lambda_init)

    w_gate = jnp.concatenate([p['lru_w_r'], p['lru_w_i']], axis=-1).astype(BF16)
    y_lru = _lru(proj, p['lru_conv_w'], p['lru_conv_b'].reshape(1, -1), w_gate,
                 p['lru_b_r'].reshape(1, -1), p['lru_b_i'].reshape(1, -1),
                 p['lru_lambda'].reshape(1, -1))

    ys = [y.reshape(bsz * seq, BRANCH) for y in (y_gla, y_swa, y_diff, y_lru)]
    return _outproj(ys, w_out_b, x2, next_norm_w)


def kernel(x, norm_w, w_in, w_out, gla_w_up, gla_b_up, gla_norm_w, swa_q_norm, swa_k_norm, swa_sinks, diff_q_norm, diff_k_norm, diff_lq1, diff_lk1, diff_lq2, diff_lk2, diff_out_norm, lru_conv_w, lru_conv_b, lru_w_r, lru_b_r, lru_w_i, lru_b_i, lru_lambda):
    params = dict(norm_w=norm_w, gla_w_up=gla_w_up, gla_b_up=gla_b_up,
                  gla_norm_w=gla_norm_w, swa_q_norm=swa_q_norm, swa_k_norm=swa_k_norm,
                  swa_sinks=swa_sinks, diff_q_norm=diff_q_norm, diff_k_norm=diff_k_norm,
                  diff_lq1=diff_lq1, diff_lk1=diff_lk1, diff_lq2=diff_lq2, diff_lk2=diff_lk2,
                  diff_out_norm=diff_out_norm, lru_conv_w=lru_conv_w, lru_conv_b=lru_conv_b,
                  lru_w_r=lru_w_r, lru_b_r=lru_b_r, lru_w_i=lru_w_i, lru_b_i=lru_b_i,
                  lru_lambda=lru_lambda)
    bsz, seq, d = x.shape
    w_in_t = jnp.swapaxes(w_in, 1, 2)
    x2 = x.reshape(bsz * seq, d)
    depth = norm_w.shape[0]
    h, rs = _norm(x2, norm_w[0].reshape(1, -1))
    for layer in range(depth):
        p = {k: v[layer] for k, v in params.items()}
        if layer + 1 < depth:
            x2, h, rs = _layer(x2, h, rs, bsz, seq, layer, w_in_t, w_out, p,
                               norm_w[layer + 1].reshape(1, -1))
        else:
            (x2,) = _layer(x2, h, rs, bsz, seq, layer, w_in_t, w_out, p, None)
    return x2.reshape(bsz, seq, d)
```

```python
import functools
import math

import jax
import jax.numpy as jnp
from jax import lax
from jax.experimental import pallas as pl
from jax.experimental.pallas import tpu as pltpu

F32 = jnp.float32
BF16 = jnp.bfloat16

D_MODEL = 4096
HEAD_DIM = 128
BRANCH = D_MODEL // 4
NORM_EPS = 1e-6
GLA_DK = 64
GLA_RANK = 16
GLA_TAU = 16.0
GLA_CHUNK = 64
SWA_WINDOW = 128
SWA_GROUP = 4
SWA_KV_HEADS = 2
DIFF_DQK = 64
DIFF_HEADS = 8
LRU_C = 8.0
CONV_WIDTH = 4
LANES = 128
SUBLANES = 8
LOG2E = math.log2(math.e)

OFF_GQ, OFF_GK, OFF_GV = 0, 512, 1024
OFF_SQ = 2048
OFF_DQ, OFF_DK, OFF_DV = 3072, 4096, 5120
OFF_RX = 6144
OFF_GATE = 7168
OFF_SK, OFF_SV = 11264, 11520
N_MAIN = 11776
OFF_GATE_GLA = OFF_GATE
OFF_GATE_SWA = OFF_GATE + BRANCH
OFF_GATE_DIFF = OFF_GATE + 2 * BRANCH
OFF_GATE_LRU = OFF_GATE + 3 * BRANCH
INPROJ_TN = 512
ORIG_A_ROW = 2048
N_PLAIN_TILES = 2048 // INPROJ_TN
N_SQ_END_TILE = 3072 // INPROJ_TN
LAST_TILE = N_MAIN // INPROJ_TN - 1

V7X_VMEM_BYTES = 64 * 1024 * 1024
VMEM_LIMIT = V7X_VMEM_BYTES - 4 * 1024 * 1024


def _cparams(sem):
    return pltpu.CompilerParams(dimension_semantics=sem, vmem_limit_bytes=VMEM_LIMIT)


def _rms(x, w):
    ms = jnp.mean(x * x, axis=-1, keepdims=True)
    return x * lax.rsqrt(ms + NORM_EPS) * w


def _gated(y, gate):
    g = gate.astype(F32)
    return (y * (g * jax.nn.sigmoid(g))).astype(BF16)


def _lane_fold(sq):
    acc = sq[:, 0:LANES]
    for g in range(1, sq.shape[1] // LANES):
        acc = acc + sq[:, g * LANES:(g + 1) * LANES]
    return acc


def _row_scale(folded, d):
    ms = jnp.sum(folded, axis=-1, keepdims=True) * (1.0 / d)
    return jnp.broadcast_to(lax.rsqrt(ms + NORM_EPS), folded.shape)


def _norm_kernel(x_ref, w_ref, o_ref, rs_ref):
    x = x_ref[...]
    o_ref[...] = (x * w_ref[...]).astype(o_ref.dtype)
    rs_ref[...] = _row_scale(_lane_fold(x * x), x.shape[1])


def _norm(x2, w, tm=512):
    m, d = x2.shape
    return pl.pallas_call(
        _norm_kernel,
        grid=(m // tm,),
        in_specs=[pl.BlockSpec((tm, d), lambda i: (i, 0)),
                  pl.BlockSpec((1, d), lambda i: (0, 0))],
        out_specs=[pl.BlockSpec((tm, d), lambda i: (i, 0)),
                   pl.BlockSpec((tm, LANES), lambda i: (i, 0))],
        out_shape=[jax.ShapeDtypeStruct((m, d), BF16),
                   jax.ShapeDtypeStruct((m, LANES), F32)],
        compiler_params=_cparams(("parallel",)),
        name="rmsnorm",
    )(x2, w)


def _nt_dot(a, b):
    return lax.dot_general(a, b, (((1,), (1,)), ((), ())), preferred_element_type=F32)


WOUT_CAST_ROWS = 32
WOUT_CAST_STEPS = 16


def _inproj_kernel(h_ref, rs_ref, w_ref, wa_ref, wo_ref, o_ref, a_ref, wob_ref):
    j = pl.program_id(1)
    h = h_ref[...]
    rs = rs_ref[...]
    acc = _nt_dot(h, w_ref[0].astype(BF16))
    o_ref[...] = (acc * jnp.concatenate([rs] * (acc.shape[1] // LANES), axis=1)).astype(o_ref.dtype)

    @pl.when(j == 0)
    def _():
        pad = jnp.zeros((LANES - GLA_RANK, wa_ref.shape[1]), BF16)
        a_ref[...] = _nt_dot(h, jnp.concatenate([wa_ref[...].astype(BF16), pad], axis=0)) * rs

    wob_ref[...] = wo_ref[...].astype(BF16)


def _src_row(j):
    tile = jnp.where(j < N_SQ_END_TILE, j, jnp.where(j == LAST_TILE, N_SQ_END_TILE, j + 1))
    units = tile * (INPROJ_TN // GLA_RANK) + jnp.where(j < N_PLAIN_TILES, 0, 1)
    return pl.multiple_of(units * GLA_RANK, GLA_RANK)


def _inproj(h, rs, w_in_t, w_out, layer, tm=1024):
    m, d = h.shape
    tn = INPROJ_TN
    k_out, n_out = w_out.shape[1:]
    assert (m // tm) * WOUT_CAST_STEPS * WOUT_CAST_ROWS == k_out
    assert N_MAIN // tn >= WOUT_CAST_STEPS
    slab = lambda i, j: i * WOUT_CAST_STEPS + jnp.minimum(j, WOUT_CAST_STEPS - 1)
    return pl.pallas_call(
        _inproj_kernel,
        grid=(m // tm, N_MAIN // tn),
        in_specs=[pl.BlockSpec((tm, d), lambda i, j: (i, 0)),
                  pl.BlockSpec((tm, LANES), lambda i, j: (i, 0)),
                  pl.BlockSpec((pl.Element(1), pl.Element(tn), pl.Element(d)),
                               lambda i, j: (layer, _src_row(j), 0)),
                  pl.BlockSpec((None, GLA_RANK, d), lambda i, j: (layer, ORIG_A_ROW // GLA_RANK, 0)),
                  pl.BlockSpec((None, WOUT_CAST_ROWS, n_out), lambda i, j: (layer, slab(i, j), 0))],
        out_specs=[pl.BlockSpec((tm, tn), lambda i, j: (i, j)),
                   pl.BlockSpec((tm, LANES), lambda i, j: (i, 0)),
                   pl.BlockSpec((WOUT_CAST_ROWS, n_out), lambda i, j: (slab(i, j), 0))],
        out_shape=[jax.ShapeDtypeStruct((m, N_MAIN), BF16),
                   jax.ShapeDtypeStruct((m, LANES), F32),
                   jax.ShapeDtypeStruct((k_out, n_out), BF16)],
        compiler_params=_cparams(("parallel", "arbitrary")),
        name="inproj",
    )(h, rs, w_in_t, w_in_t, w_out)


GLA_PAIRS_PER_STEP = 2


def _gla_kernel(q_ref, k_ref, v_ref, gate_ref, a_ref, wup_ref, bup_ref, nw_ref, o_ref, *, seq):
    for pp in range(GLA_PAIRS_PER_STEP):
        kc = slice(pp * LANES, (pp + 1) * LANES)
        vc = slice(pp * 2 * HEAD_DIM, (pp + 1) * 2 * HEAD_DIM)
        _gla_pair(q_ref.at[:, kc], k_ref.at[:, kc], v_ref.at[:, vc], gate_ref.at[:, vc], a_ref,
                  wup_ref.at[:, kc], bup_ref.at[:, kc], nw_ref, o_ref.at[:, vc], seq=seq)


def _gla_pair(q_ref, k_ref, v_ref, gate_ref, a_ref, wup_ref, bup_ref, nw_ref, o_ref, *, seq):
    c = GLA_CHUNK
    nc = seq // c
    dv2 = 2 * HEAD_DIM
    logit = jnp.dot(a_ref[...], wup_ref[...], preferred_element_type=F32) + bup_ref[...]
    g = (jnp.minimum(logit, 0.0) - jnp.log(1.0 + jnp.exp(-jnp.abs(logit)))) * (LOG2E / GLA_TAU)
    row = lax.broadcasted_iota(jnp.int32, (seq, LANES), 0) & (c - 1)
    b = g
    shift = 1
    while shift < SUBLANES:
        b = b + jnp.where(row >= shift, pltpu.roll(b, shift, 0), 0.0)
        shift *= 2
    b4 = b.reshape(nc, c // SUBLANES, SUBLANES, LANES)
    skip = 1
    while skip < c // SUBLANES:
        b4 = jnp.concatenate([b4[:, :skip], b4[:, skip:] + b4[:, :-skip]], axis=1)
        skip *= 2
    b3 = b4.reshape(nc, c, LANES)
    b_last = b3[:, c - 1:c, :]
    q3 = q_ref[...].astype(F32).reshape(nc, c, LANES)
    k3 = k_ref[...].astype(F32).reshape(nc, c, LANES)
    v3 = v_ref[...].reshape(nc, c, dv2)
    q_dec = q3 * (GLA_DK ** -0.5) * jnp.exp2(b3)
    k_dec = (k3 * jnp.exp2(-b3)).astype(BF16)
    k_state = (k3 * jnp.exp2(b_last - b3)).astype(BF16)
    decay = jnp.exp2(b_last)
    lane = lax.broadcasted_iota(jnp.int32, (1, 1, LANES), 2)
    q_st = jnp.concatenate([jnp.where(lane < GLA_DK, q_dec, 0.0),
                            jnp.where(lane >= GLA_DK, q_dec, 0.0)], axis=1).astype(BF16)
    att = jnp.einsum('nid,njd->nij', q_st, k_dec, preferred_element_type=F32)
    ii = lax.broadcasted_iota(jnp.int32, (1, 2 * c, c), 1) & (c - 1)
    jj = lax.broadcasted_iota(jnp.int32, (1, 2 * c, c), 2)
    att = jnp.where(ii >= jj, att, 0.0).astype(BF16)
    heads = ((slice(0, c), slice(0, HEAD_DIM)), (slice(c, 2 * c), slice(HEAD_DIM, dv2)))
    o_heads = [jnp.einsum('nij,njv->niv', att[:, rs], v3[:, :, vs], preferred_element_type=F32)
               for rs, vs in heads]
    v3_t = jnp.swapaxes(v3.astype(F32), 1, 2).astype(BF16)
    u_t = jnp.einsum('nvj,njd->nvd', v3_t, k_state, preferred_element_type=F32)
    st = jnp.zeros((dv2, LANES), F32)
    starts = []
    for n in range(nc):
        starts.append(st.astype(BF16))
        st = st * decay[n] + u_t[n]
    s_start = jnp.stack(starts, axis=0)
    nw = nw_ref[...]
    for (rs, vs), o_intra in zip(heads, o_heads):
        o = o_intra + jnp.einsum('nid,nvd->niv', q_st[:, rs], s_start[:, vs],
                                 preferred_element_type=F32)
        o = _rms(o, nw).reshape(seq, HEAD_DIM)
        o_ref[:, vs] = _gated(o, gate_ref[:, vs])


def _gla(proj, a_lr, w_up, b_up, norm_w):
    bsz, seq, _ = proj.shape
    kw = GLA_PAIRS_PER_STEP * LANES
    vw = GLA_PAIRS_PER_STEP * 2 * HEAD_DIM
    kern = functools.partial(_gla_kernel, seq=seq)
    return pl.pallas_call(
        kern,
        grid=(bsz, BRANCH // vw),
        in_specs=[
            pl.BlockSpec((None, seq, kw), lambda b, p: (b, 0, OFF_GQ // kw + p)),
            pl.BlockSpec((None, seq, kw), lambda b, p: (b, 0, OFF_GK // kw + p)),
            pl.BlockSpec((None, seq, vw), lambda b, p: (b, 0, OFF_GV // vw + p)),
            pl.BlockSpec((None, seq, vw), lambda b, p: (b, 0, OFF_GATE_GLA // vw + p)),
            pl.BlockSpec((None, seq, LANES), lambda b, p: (b, 0, 0)),
            pl.BlockSpec((LANES, kw), lambda b, p: (0, p)),
            pl.BlockSpec((1, kw), lambda b, p: (0, p)),
            pl.BlockSpec((1, HEAD_DIM), lambda b, p: (0, 0)),
        ],
        out_specs=pl.BlockSpec((None, seq, vw), lambda b, p: (b, 0, p)),
        out_shape=jax.ShapeDtypeStruct((bsz, seq, BRANCH), BF16),
        compiler_params=_cparams(("parallel", "parallel")),
        name="gla",
    )(proj, proj, proj, proj, a_lr, w_up, b_up, norm_w)


def _swa_kernel(slope_ref, sink_ref, q_ref, k_ref, v_ref, gate_ref, qn_ref, kn_ref, o_ref,
                qs_ref, ks_ref, *, seq):
    w = SWA_WINDOW
    nb = seq // w
    g_n = SWA_GROUP
    kvh = pl.program_id(1)
    ks_ref[...] = _rms(k_ref[...].astype(F32), kn_ref[...]).astype(BF16)
    q_weight = qn_ref[...] * (HEAD_DIM ** -0.5)
    for g in range(g_n):
        qg = q_ref[:, g * HEAD_DIM:(g + 1) * HEAD_DIM].astype(F32)
        qs_ref[:, g * HEAD_DIM:(g + 1) * HEAD_DIM] = _rms(qg, q_weight).astype(BF16)
    qi = lax.broadcasted_iota(jnp.int32, (w, 2 * w), 0)
    kj = lax.broadcasted_iota(jnp.int32, (w, 2 * w), 1)
    dist = qi + w - kj
    valid = (dist >= 0) & (dist < w)
    distf = dist.astype(F32)
    biases, sinks = [], []
    for g in range(g_n):
        slope = slope_ref[kvh * g_n + g]
        biases.append(jnp.where(valid, -slope * distf, -jnp.inf))
        sinks.append(sink_ref[kvh * g_n + g])

    def block(q_rows, k_win, v_win, bias_cols):
        q_st = jnp.concatenate(
            [qs_ref[q_rows, g * HEAD_DIM:(g + 1) * HEAD_DIM] for g in range(g_n)], axis=0)
        s = _nt_dot(q_st, k_win)
        ps, dens = [], []
        for g in range(g_n):
            sg = s[g * w:(g + 1) * w] + biases[g][:, bias_cols]
            m = jnp.maximum(jnp.max(sg, axis=-1, keepdims=True), sinks[g])
            p = jnp.exp(sg - m)
            dens.append(jnp.sum(p, axis=-1, keepdims=True) + jnp.exp(sinks[g] - m))
            ps.append(p.astype(BF16))
        o = jnp.dot(jnp.concatenate(ps, axis=0), v_win, preferred_element_type=F32)
        for g in range(g_n):
            cols = slice(g * HEAD_DIM, (g + 1) * HEAD_DIM)
            og = o[g * w:(g + 1) * w] / dens[g]
            o_ref[q_rows, cols] = _gated(og, gate_ref[q_rows, cols])

    block(pl.ds(0, w), ks_ref[0:w, :], v_ref[0:w, :], slice(w, 2 * w))

    def body(n, carry):
        q_rows = pl.ds(pl.multiple_of(n * w, w), w)
        win = pl.ds(pl.multiple_of((n - 1) * w, w), 2 * w)
        block(q_rows, ks_ref[win, :], v_ref[win, :], slice(0, 2 * w))
        return carry

    lax.fori_loop(1, nb, body, 0, unroll=True)


def _swa(proj, q_norm, k_norm, slopes, sinks):
    bsz, seq, _ = proj.shape
    qw = SWA_GROUP * HEAD_DIM
    kern = functools.partial(_swa_kernel, seq=seq)
    smem = pl.BlockSpec(memory_space=pltpu.SMEM)
    return pl.pallas_call(
        kern,
        grid=(bsz, SWA_KV_HEADS),
        in_specs=[
            smem, smem,
            pl.BlockSpec((None, seq, qw), lambda b, h: (b, 0, OFF_SQ // qw + h)),
            pl.BlockSpec((None, seq, HEAD_DIM), lambda b, h: (b, 0, OFF_SK // HEAD_DIM + h)),
            pl.BlockSpec((None, seq, HEAD_DIM), lambda b, h: (b, 0, OFF_SV // HEAD_DIM + h)),
            pl.BlockSpec((None, seq, qw), lambda b, h: (b, 0, OFF_GATE_SWA // qw + h)),
            pl.BlockSpec((1, HEAD_DIM), lambda b, h: (0, 0)),
            pl.BlockSpec((1, HEAD_DIM), lambda b, h: (0, 0)),
        ],
        out_specs=pl.BlockSpec((None, seq, qw), lambda b, h: (b, 0, h)),
        out_shape=jax.ShapeDtypeStruct((bsz, seq, BRANCH), BF16),
        scratch_shapes=[pltpu.VMEM((seq, qw), BF16), pltpu.VMEM((seq, HEAD_DIM), BF16)],
        compiler_params=_cparams(("parallel", "parallel")),
        name="swa",
    )(slopes, sinks, proj, proj, proj, proj, q_norm, k_norm)


DIFF_TQ = 256
DIFF_HEADS_PER_STEP = 2


def _group_mean(x, group):
    gi = lax.broadcasted_iota(jnp.int32, (LANES, LANES), 0) // group
    gj = lax.broadcasted_iota(jnp.int32, (LANES, LANES), 1) // group
    averager = jnp.where(gi == gj, 1.0 / group, 0.0).astype(BF16)
    return jnp.dot(x.astype(BF16), averager, preferred_element_type=F32)


def _half_rms(x, w2):
    return x * lax.rsqrt(_group_mean(x * x, DIFF_DQK) + NORM_EPS) * w2


def _diff_kernel(slope_ref, q_ref, k_ref, v_ref, gate_ref, qn_ref, kn_ref, lq1_ref, lk1_ref,
                 lq2_ref, lk2_ref, onw_ref, o_ref, *scratch, seq, lambda_init):
    qs_refs, kt_refs, va_refs = scratch[0::3], scratch[1::3], scratch[2::3]
    t = DIFF_TQ
    nq = seq // t
    n_terms = 3
    lam = (jnp.exp(jnp.sum(lq1_ref[...] * lk1_ref[...], axis=-1, keepdims=True))
           - jnp.exp(jnp.sum(lq2_ref[...] * lk2_ref[...], axis=-1, keepdims=True)) + lambda_init)
    out_weight = onw_ref[...] * (1.0 - lambda_init)
    lane = lax.broadcasted_iota(jnp.int32, (1, LANES), 1)
    bias_tile = 2 * SUBLANES
    sub = lax.broadcasted_iota(jnp.int32, (bias_tile, 1), 0)
    key_pos = lax.broadcasted_iota(jnp.int32, (1, seq), 1).astype(F32)
    ones = jnp.broadcast_to(jnp.where(lane < n_terms, 1.0, 0.0).astype(BF16), (2 * t, HEAD_DIM))
    one_cols = jnp.ones((seq, HEAD_DIM), BF16)
    r = lax.broadcasted_iota(jnp.int32, (2 * t, t), 0) & (t - 1)
    c = lax.broadcasted_iota(jnp.int32, (2 * t, t), 1)
    causal = r >= c

    def prepare(hh):
        cols = slice(hh * HEAD_DIM, (hh + 1) * HEAD_DIM)
        slope = slope_ref[pl.program_id(1) * DIFF_HEADS_PER_STEP + hh]
        kt_refs[hh][0:HEAD_DIM, :] = _half_rms(k_ref[:, cols].astype(F32), kn_ref[...]).T.astype(BF16)
        rest = key_pos * (slope * LOG2E)
        bias_rows = jnp.zeros((bias_tile, seq), F32)
        for term in range(n_terms):
            piece = rest.astype(BF16).astype(F32)
            bias_rows = jnp.where(sub == term, piece, bias_rows)
            rest = rest - piece
        kt_refs[hh][HEAD_DIM:HEAD_DIM + bias_tile, :] = bias_rows.astype(BF16)
        kt_refs[hh][HEAD_DIM + bias_tile:2 * HEAD_DIM, :] = jnp.zeros(
            (HEAD_DIM - bias_tile, seq), BF16)
        va_refs[hh][:, 0:HEAD_DIM] = v_ref[:, cols]
        va_refs[hh][:, HEAD_DIM:2 * HEAD_DIM] = one_cols
        qn = _half_rms(q_ref[:, cols].astype(F32), qn_ref[...] * (DIFF_DQK ** -0.5 * LOG2E))
        q_lo = jnp.where(lane < DIFF_DQK, qn, 0.0).astype(BF16)
        q_hi = jnp.where(lane < DIFF_DQK, 0.0, qn).astype(BF16)
        for i in range(nq):
            qs_refs[hh][i, 0:t, 0:HEAD_DIM] = q_lo[i * t:(i + 1) * t]
            qs_refs[hh][i, t:2 * t, 0:HEAD_DIM] = q_hi[i * t:(i + 1) * t]
            qs_refs[hh][i, :, HEAD_DIM:2 * HEAD_DIM] = ones

    def scores(hh, i):
        kk = (i + 1) * t
        s = jnp.dot(qs_refs[hh][i], kt_refs[hh][:, 0:kk], preferred_element_type=F32)
        s_diag = jnp.where(causal, s[:, kk - t:kk], -jnp.inf)
        return s_diag if i == 0 else jnp.concatenate([s[:, 0:kk - t], s_diag], axis=1)

    def finish(hh, i, s):
        kk = (i + 1) * t
        rows = slice(i * t, kk)
        cols = slice(hh * HEAD_DIM, (hh + 1) * HEAD_DIM)
        m = jnp.max(s, axis=-1, keepdims=True)
        p = jnp.exp2(s - m).astype(BF16)
        o_aug = jnp.dot(p, va_refs[hh][0:kk, :], preferred_element_type=F32)
        o12 = o_aug[:, 0:HEAD_DIM] * (1.0 / o_aug[:, HEAD_DIM:2 * HEAD_DIM])
        o = o12[0:t] - lam * o12[t:2 * t]
        o = _rms(o, out_weight)
        o_ref[rows, cols] = _gated(o, gate_ref[rows, cols])

    heads = range(DIFF_HEADS_PER_STEP)
    order = list(range(nq - 1, -1, -1))
    for hh in heads:
        prepare(hh)
    s_next = [scores(hh, order[0]) for hh in heads]
    for pos, i in enumerate(order):
        for hh in heads:
            s = s_next[hh]
            if pos + 1 < nq:
                s_next[hh] = scores(hh, order[pos + 1])
            finish(hh, i, s)


def _diff(proj, slopes, q_norm2, k_norm2, lq1, lk1, lq2, lk2, out_norm, lambda_init):
    bsz, seq, _ = proj.shape
    t = DIFF_TQ
    hps = DIFF_HEADS_PER_STEP
    width = hps * HEAD_DIM
    kern = functools.partial(_diff_kernel, seq=seq, lambda_init=lambda_init)
    vec = lambda n: pl.BlockSpec((1, n), lambda b, h: (0, 0))
    blk = lambda off: pl.BlockSpec((None, seq, width), lambda b, h: (b, 0, off // width + h))
    return pl.pallas_call(
        kern,
        grid=(bsz, DIFF_HEADS // hps),
        in_specs=[
            pl.BlockSpec(memory_space=pltpu.SMEM),
            blk(OFF_DQ), blk(OFF_DK), blk(OFF_DV), blk(OFF_GATE_DIFF),
            vec(HEAD_DIM), vec(HEAD_DIM),
            vec(DIFF_DQK), vec(DIFF_DQK), vec(DIFF_DQK), vec(DIFF_DQK),
            vec(HEAD_DIM),
        ],
        out_specs=pl.BlockSpec((None, seq, width), lambda b, h: (b, 0, h)),
        out_shape=jax.ShapeDtypeStruct((bsz, seq, BRANCH), BF16),
        scratch_shapes=[
            pltpu.VMEM((seq // t, 2 * t, 2 * HEAD_DIM), BF16),
            pltpu.VMEM((2 * HEAD_DIM, seq), BF16),
            pltpu.VMEM((seq, 2 * HEAD_DIM), BF16),
        ] * hps,
        compiler_params=_cparams(("parallel", "parallel")),
        name="diffattn",
    )(slopes, proj, proj, proj, proj, q_norm2, k_norm2, lq1, lk1, lq2, lk2, out_norm)


LRU_TILE = 512


def _lru_kernel(x_ref, gate_ref, cw_ref, cb_ref, wg_ref, br_ref, bi_ref, lam_ref, o_ref,
                a_scr, h_scr, *, seq):
    width = LRU_TILE
    x = x_ref[...].astype(F32)
    row = lax.broadcasted_iota(jnp.int32, (seq, width), 0)
    xc = x * cw_ref[CONV_WIDTH - 1:CONV_WIDTH, :] + cb_ref[...]
    for s in range(1, CONV_WIDTH):
        tap = cw_ref[CONV_WIDTH - 1 - s:CONV_WIDTH - s, :]
        xc = xc + jnp.where(row >= s, pltpu.roll(x, s, 0), 0.0) * tap
    lam = lam_ref[...]
    decay_rate = (-LRU_C) * (jnp.maximum(-lam, 0.0) + jnp.log1p(jnp.exp(-jnp.abs(lam))))
    sub = lax.broadcasted_iota(jnp.int32, (seq, HEAD_DIM), 0) & (SUBLANES - 1)
    for n in range(width // HEAD_DIM):
        cols = slice(n * HEAD_DIM, (n + 1) * HEAD_DIM)
        xn = xc[:, cols]
        ri = jnp.dot(xn.astype(BF16), wg_ref[n], preferred_element_type=F32)
        r = jax.nn.sigmoid(ri[:, 0:HEAD_DIM] + br_ref[:, cols])
        gi = jax.nn.sigmoid(ri[:, HEAD_DIM:2 * HEAD_DIM] + bi_ref[:, cols])
        log_a = r * decay_rate[:, cols]
        a = jnp.exp(log_a)
        u = jnp.sqrt(1.0 - a * a) * (gi * xn)
        shift = 1
        while shift < SUBLANES:
            keep = sub >= shift
            u = jnp.where(keep, a * pltpu.roll(u, shift, 0) + u, u)
            a = jnp.where(keep, a * pltpu.roll(a, shift, 0), a)
            shift *= 2
        a_scr[:, cols] = a
        h_scr[:, cols] = u

    def body(tile, carry):
        rows = pl.ds(pl.multiple_of(tile * SUBLANES, SUBLANES), SUBLANES)
        h = a_scr[rows, :] * carry + h_scr[rows, :]
        h_scr[rows, :] = h
        return jnp.broadcast_to(h[SUBLANES - 1:SUBLANES, :], (SUBLANES, width))

    lax.fori_loop(0, seq // SUBLANES, body, jnp.zeros((SUBLANES, width), F32), unroll=8)
    o_ref[...] = _gated(h_scr[...], gate_ref[...])


def _lru(proj, conv_w, conv_b, w_gate, b_r, b_i, lam):
    bsz, seq, _ = proj.shape
    t = LRU_TILE
    nblk = t // HEAD_DIM
    kern = functools.partial(_lru_kernel, seq=seq)
    vec = pl.BlockSpec((1, t), lambda b, c: (0, c))
    return pl.pallas_call(
        kern,
        grid=(bsz, BRANCH // t),
        in_specs=[
            pl.BlockSpec((None, seq, t), lambda b, c: (b, 0, OFF_RX // t + c)),
            pl.BlockSpec((None, seq, t), lambda b, c: (b, 0, OFF_GATE_LRU // t + c)),
            pl.BlockSpec((CONV_WIDTH, t), lambda b, c: (0, c)),
            vec,
            pl.BlockSpec((nblk, HEAD_DIM, 2 * HEAD_DIM), lambda b, c: (c, 0, 0)),
            vec, vec, vec,
        ],
        out_specs=pl.BlockSpec((None, seq, t), lambda b, c: (b, 0, c)),
        out_shape=jax.ShapeDtypeStruct((bsz, seq, BRANCH), BF16),
        scratch_shapes=[pltpu.VMEM((seq, t), F32), pltpu.VMEM((seq, t), F32)],
        compiler_params=_cparams(("parallel", "parallel")),
        name="rglru",
    )(proj, proj, conv_w, conv_b, w_gate, b_r, b_i, lam)


def _outproj_kernel(y0, y1, y2, y3, w_ref, x_ref, *rest, d):
    acc = x_ref[...]
    for k, y in enumerate((y0, y1, y2, y3)):
        acc = acc + jnp.dot(y[...], w_ref[k * BRANCH:(k + 1) * BRANCH, :],
                            preferred_element_type=F32)
    if len(rest) == 1:
        (o_ref,) = rest
        o_ref[...] = acc
        return
    nw_ref, o_ref, h_ref, rs_ref = rest
    j = pl.program_id(1)
    o_ref[...] = acc
    h_ref[...] = (acc * nw_ref[...]).astype(h_ref.dtype)
    folded = _lane_fold(acc * acc)

    @pl.when(j == 0)
    def _():
        rs_ref[...] = folded

    @pl.when(j > 0)
    def _():
        rs_ref[...] += folded

    @pl.when(j == pl.num_programs(1) - 1)
    def _():
        rs_ref[...] = _row_scale(rs_ref[...], d)


def _outproj(ys, w_out_b, x2, next_norm_w=None, tm=1024, tn=1024):
    m, d = x2.shape
    in_specs = [pl.BlockSpec((tm, BRANCH), lambda i, j: (i, 0)) for _ in range(4)] + [
        pl.BlockSpec((d, tn), lambda i, j: (0, j)),
        pl.BlockSpec((tm, tn), lambda i, j: (i, j)),
    ]
    out_specs = [pl.BlockSpec((tm, tn), lambda i, j: (i, j))]
    out_shape = [jax.ShapeDtypeStruct((m, d), F32)]
    args = [*ys, w_out_b, x2]
    if next_norm_w is not None:
        in_specs.append(pl.BlockSpec((1, tn), lambda i, j: (0, j)))
        out_specs += [pl.BlockSpec((tm, tn), lambda i, j: (i, j)),
                      pl.BlockSpec((tm, LANES), lambda i, j: (i, 0))]
        out_shape += [jax.ShapeDtypeStruct((m, d), BF16), jax.ShapeDtypeStruct((m, LANES), F32)]
        args.append(next_norm_w)
    return pl.pallas_call(
        functools.partial(_outproj_kernel, d=d),
        grid=(m // tm, d // tn),
        in_specs=in_specs,
        out_specs=out_specs,
        out_shape=out_shape,
        compiler_params=_cparams(("parallel", "arbitrary")),
        name="outproj",
    )(*args)


def _alibi_slopes(n):
    return 2.0 ** (-8.0 * jnp.arange(1, n + 1, dtype=F32) / n)


def _layer(x2, h, rs, bsz, seq, layer, w_in_t, w_out, p, next_norm_w):
    proj2, a2, w_out_b = _inproj(h, rs, w_in_t, w_out, layer)
    proj = proj2.reshape(bsz, seq, N_MAIN)
    a_lr = a2.reshape(bsz, seq, LANES)

    w_up = jnp.pad(p['gla_w_up'], ((0, LANES - GLA_RANK), (0, 0)))
    y_gla = _gla(proj, a_lr, w_up, p['gla_b_up'].reshape(1, -1), p['gla_norm_w'].reshape(1, -1))

    y_swa = _swa(proj, p['swa_q_norm'].reshape(1, -1), p['swa_k_norm'].reshape(1, -1),
                 _alibi_slopes(SWA_GROUP * SWA_KV_HEADS), p['swa_sinks'])

    lambda_init = 0.8 - 0.6 * math.exp(-0.3 * layer)
    tile2 = lambda v: jnp.concatenate([v, v]).reshape(1, -1)
    y_diff = _diff(proj, _alibi_slopes(DIFF_HEADS), tile2(p['diff_q_norm']), tile2(p['diff_k_norm']),
                   p['diff_lq1'].reshape(1, -1), p['diff_lk1'].reshape(1, -1),
                   p['diff_lq2'].reshape(1, -1), p['diff_lk2'].reshape(1, -1),
                   p['diff_out_norm'].reshape(1, -1), lambda_init)

    w_gate = jnp.concatenate([p['lru_w_r'], p['lru_w_i']], axis=-1).astype(BF16)
    y_lru = _lru(proj, p['lru_conv_w'], p['lru_conv_b'].reshape(1, -1), w_gate,
                 p['lru_b_r'].reshape(1, -1), p['lru_b_i'].reshape(1, -1),
                 p['lru_lambda'].reshape(1, -1))

    ys = [y.reshape(bsz * seq, BRANCH) for y in (y_gla, y_swa, y_diff, y_lru)]
    return _outproj(ys, w_out_b, x2, next_norm_w)


def kernel(x, norm_w, w_in, w_out, gla_w_up, gla_b_up, gla_norm_w, swa_q_norm, swa_k_norm, swa_sinks, diff_q_norm, diff_k_norm, diff_lq1, diff_lk1, diff_lq2, diff_lk2, diff_out_norm, lru_conv_w, lru_conv_b, lru_w_r, lru_b_r, lru_w_i, lru_b_i, lru_lambda):
    params = dict(norm_w=norm_w, gla_w_up=gla_w_up, gla_b_up=gla_b_up,
                  gla_norm_w=gla_norm_w, swa_q_norm=swa_q_norm, swa_k_norm=swa_k_norm,
                  swa_sinks=swa_sinks, diff_q_norm=diff_q_norm, diff_k_norm=diff_k_norm,
                  diff_lq1=diff_lq1, diff_lk1=diff_lk1, diff_lq2=diff_lq2, diff_lk2=diff_lk2,
                  diff_out_norm=diff_out_norm, lru_conv_w=lru_conv_w, lru_conv_b=lru_conv_b,
                  lru_w_r=lru_w_r, lru_b_r=lru_b_r, lru_w_i=lru_w_i, lru_b_i=lru_b_i,
                  lru_lambda=lru_lambda)
    bsz, seq, d = x.shape
    w_in_t = jnp.swapaxes(w_in, 1, 2)
    x2 = x.reshape(bsz * seq, d)
    depth = norm_w.shape[0]
    h, rs = _norm(x2, norm_w[0].reshape(1, -1))
    for layer in range(depth):
        p = {k: v[layer] for k, v in params.items()}
        if layer + 1 < depth:
            x2, h, rs = _layer(x2, h, rs, bsz, seq, layer, w_in_t, w_out, p,
                               norm_w[layer + 1].reshape(1, -1))
        else:
            (x2,) = _layer(x2, h, rs, bsz, seq, layer, w_in_t, w_out, p, None)
    return x2.reshape(bsz, seq, d)
```
